```python
import jax, jax.numpy as jnp
from jax import lax
import numpy as np

D_MODEL = 1024
BATCH = 8
SEQ = 4096
DEPTH = 2

GRID_W = 64
CTX_LEN = 256
HEAD_DIM = 64
N_MIXERS = 4
GROUP_HEADS = D_MODEL // (N_MIXERS * HEAD_DIM)
GROUP_WIDTH = GROUP_HEADS * HEAD_DIM
KV_HEADS = GROUP_HEADS // 2
KV_WIDTH = KV_HEADS * HEAD_DIM
Q_BLOCK = 128
WINDOW = 128
ROPE_THETA = 10000.0
GLA_DK = HEAD_DIM // 2
GLA_KW = GROUP_HEADS * GLA_DK
GLA_GATE_RANK = 16
GLA_GATE_TAU = 16.0
GLA_CHUNK = 64
MLSTM_CHUNK = 64
D_FF = 4 * D_MODEL
N_MOD = 6
EPS = 1e-6
NEG = -1e30
IN_SPLITS = (GROUP_WIDTH, KV_WIDTH, KV_WIDTH,
             GROUP_WIDTH, KV_WIDTH, KV_WIDTH,
             GLA_KW, GLA_KW, GROUP_WIDTH, GROUP_WIDTH, 2 * GLA_GATE_RANK,
             GROUP_WIDTH, GROUP_WIDTH, GROUP_WIDTH, GROUP_WIDTH,
             2 * GROUP_HEADS, 2 * GROUP_HEADS)
IN_WIDTH = sum(IN_SPLITS)

kernel_name = 'hybrid_parallel_heads_dit_block'


def rms_norm(x, g):
    xf = x.astype(jnp.float32)
    y = xf * lax.rsqrt(jnp.mean(xf * xf, axis=-1, keepdims=True) + EPS)
    return (y * g.astype(jnp.float32)).astype(x.dtype)


def heads(a, dim=HEAD_DIM):
    return a.reshape(a.shape[:-1] + (-1, dim))


def _flip(a):
    return jnp.flip(a, axis=1)


def _ident(a):
    return a


def axial_rope(n_tokens):
    rows = n_tokens // GRID_W
    row = jnp.repeat(jnp.arange(rows, dtype=jnp.float32), GRID_W)
    col = jnp.tile(jnp.arange(GRID_W, dtype=jnp.float32), rows)
    n_freq = HEAD_DIM // 4
    inv = ROPE_THETA ** (-jnp.arange(n_freq, dtype=jnp.float32) / n_freq)
    ang = jnp.concatenate([row[:, None] * inv, col[:, None] * inv], axis=-1)
    return jnp.cos(ang), jnp.sin(ang)


def apply_rope(x, cos, sin):
    xf = x.astype(jnp.float32)
    x1, x2 = xf[..., 0::2], xf[..., 1::2]
    c, s = cos[:, None, :], sin[:, None, :]
    y = jnp.stack([x1 * c - x2 * s, x1 * s + x2 * c], axis=-1).reshape(x.shape)
    return y.astype(x.dtype)


def context_attention(q, k, v, sink):
    B_, L, H, d = q.shape
    G = k.shape[2]
    R = H // G
    qg = q.reshape(B_, L, G, R, d)
    s = jnp.einsum('blgrd,bmgd->bgrlm', qg, k, preferred_element_type=jnp.float32) * d ** -0.5
    if sink is not None:
        s_sink = jnp.broadcast_to(sink.astype(jnp.float32).reshape(1, G, R, 1, 1), s.shape[:-1] + (1,))
        s = jnp.concatenate([s, s_sink], axis=-1)
    w = jax.nn.softmax(s, axis=-1)[..., :L]
    o = jnp.einsum('bgrlm,bmgd->blgrd', w.astype(v.dtype), v)
    return o.reshape(B_, L, H * d)


def global_block_attention(q, k, v, k_c, v_c):
    B_, S, H, d = q.shape
    G = k.shape[2]
    R = H // G
    nb = S // Q_BLOCK
    scale = d ** -0.5
    qb = q.reshape(B_, nb, Q_BLOCK, G, R, d).transpose(1, 0, 2, 3, 4, 5)

    def one_block(qi):
        s_lat = jnp.einsum('bqgrd,bkgd->bgrqk', qi, k, preferred_element_type=jnp.float32) * scale
        s_ctx = jnp.einsum('bqgrd,bcgd->bgrqc', qi, k_c, preferred_element_type=jnp.float32) * scale
        w = jax.nn.softmax(jnp.concatenate([s_lat, s_ctx], axis=-1), axis=-1).astype(v.dtype)
        return (jnp.einsum('bgrqk,bkgd->bqgrd', w[..., :S], v)
                + jnp.einsum('bgrqc,bcgd->bqgrd', w[..., S:], v_c))

    o = lax.map(one_block, qb)
    return o.transpose(1, 0, 2, 3, 4, 5).reshape(B_, S, H * d)


def window_block_attention(q, k, v, k_c, v_c, sink):
    B_, S, H, d = q.shape
    G = k.shape[2]
    R = H // G
    L = k_c.shape[1]
    nb = S // Q_BLOCK
    scale = d ** -0.5
    pad = ((0, 0), (Q_BLOCK, Q_BLOCK), (0, 0), (0, 0))
    kp = jnp.pad(k, pad).reshape(B_, nb + 2, Q_BLOCK, G, d)
    vp = jnp.pad(v, pad).reshape(B_, nb + 2, Q_BLOCK, G, d)
    kw = jnp.concatenate([kp[:, :-2], kp[:, 1:-1], kp[:, 2:]], axis=2)
    vw = jnp.concatenate([vp[:, :-2], vp[:, 1:-1], vp[:, 2:]], axis=2)
    qb = q.reshape(B_, nb, Q_BLOCK, G, R, d)
    s_win = jnp.einsum('bnqgrd,bnkgd->bngrqk', qb, kw, preferred_element_type=jnp.float32) * scale
    qpos = jnp.arange(nb)[:, None] * Q_BLOCK + jnp.arange(Q_BLOCK)[None, :]
    kpos = jnp.arange(nb)[:, None] * Q_BLOCK - Q_BLOCK + jnp.arange(3 * Q_BLOCK)[None, :]
    ok = ((jnp.abs(qpos[:, :, None] - kpos[:, None, :]) <= WINDOW)
          & (kpos >= 0)[:, None, :] & (kpos < S)[:, None, :])
    s_win = jnp.where(ok[None, :, None, None], s_win, NEG)
    s_ctx = jnp.einsum('bnqgrd,bcgd->bngrqc', qb, k_c, preferred_element_type=jnp.float32) * scale
    s_sink = jnp.broadcast_to(sink.astype(jnp.float32).reshape(1, 1, G, R, 1, 1), s_ctx.shape[:-1] + (1,))
    w = jax.nn.softmax(jnp.concatenate([s_win, s_ctx, s_sink], axis=-1), axis=-1).astype(v.dtype)
    K3 = 3 * Q_BLOCK
    o = (jnp.einsum('bngrqk,bnkgd->bnqgrd', w[..., :K3], vw)
         + jnp.einsum('bngrqc,bcgd->bnqgrd', w[..., K3:K3 + L], v_c))
    return o.reshape(B_, S, H * d)


def gla_scan(q, k, v, log_a, state):
    B_, T, H, _ = q.shape
    dv = v.shape[-1]
    nc = T // GLA_CHUNK
    tri = jnp.tril(jnp.ones((GLA_CHUNK, GLA_CHUNK), dtype=bool))

    def chunks(a):
        return a.astype(jnp.float32).reshape(B_, nc, GLA_CHUNK, H, a.shape[-1]).transpose(1, 0, 3, 2, 4)

    def step(S, inp):
        qc, kc, vc, la = inp
        b = jnp.cumsum(la, axis=2)
        rel = jnp.where(tri[:, :, None], b[:, :, :, None, :] - b[:, :, None, :, :], -jnp.inf)
        scores = jnp.einsum('bhtd,bhsd,bhtsd->bhts', qc, kc, jnp.exp(rel))
        o = (jnp.einsum('bhts,bhsv->bhtv', scores, vc)
             + jnp.einsum('bhtd,bhdv->bhtv', qc * jnp.exp(b), S))
        b_end = b[:, :, -1:, :]
        S = (jnp.exp(b_end[:, :, 0, :])[..., None] * S
             + jnp.einsum('bhsd,bhsv->bhdv', kc * jnp.exp(b_end - b), vc))
        return S, o

    S, o = lax.scan(step, state, (chunks(q), chunks(k), chunks(v), chunks(log_a)))
    return o.transpose(1, 0, 3, 2, 4).reshape(B_, T, H, dv), S


def mlstm_scan(q, k, v, i_pre, log_f, state):
    B_, T, H, _ = q.shape
    dv = v.shape[-1]
    nc = T // MLSTM_CHUNK
    tri = jnp.tril(jnp.ones((MLSTM_CHUNK, MLSTM_CHUNK), dtype=bool))

    def chunks4(a):
        return a.astype(jnp.float32).reshape(B_, nc, MLSTM_CHUNK, H, a.shape[-1]).transpose(1, 0, 3, 2, 4)

    def chunks3(a):
        return a.astype(jnp.float32).reshape(B_, nc, MLSTM_CHUNK, H).transpose(1, 0, 3, 2)

    def step(carry, inp):
        C, n, m = carry
        qc, kc, vc, ic, fc = inp
        F = jnp.cumsum(fc, axis=-1)
        logw = jnp.where(tri, F[..., :, None] - F[..., None, :] + ic[..., None, :], -jnp.inf)
        log_inter = F + m[..., None]
        m_t = jnp.maximum(log_inter, jnp.max(logw, axis=-1))
        w = jnp.exp(logw - m_t[..., None])
        w_inter = jnp.exp(log_inter - m_t)
        qk = jnp.einsum('bhtd,bhsd->bhts', qc, kc) * w
        num = (jnp.einsum('bhts,bhsv->bhtv', qk, vc)
               + w_inter[..., None] * jnp.einsum('bhtd,bhdv->bhtv', qc, C))
        den = jnp.sum(qk, axis=-1) + w_inter * jnp.einsum('bhtd,bhd->bht', qc, n)
        h = num / jnp.maximum(jnp.abs(den), jnp.exp(-m_t))[..., None]
        F_end = F[..., -1]
        log_end = F_end[..., None] - F + ic
        m_new = jnp.maximum(F_end + m, jnp.max(log_end, axis=-1))
        w_end = jnp.exp(log_end - m_new[..., None])
        decay = jnp.exp(F_end + m - m_new)
        C = decay[..., None, None] * C + jnp.einsum('bhs,bhsd,bhsv->bhdv', w_end, kc, vc)
        n = decay[..., None] * n + jnp.einsum('bhs,bhsd->bhd', w_end, kc)
        return (C, n, m_new), h

    state, h = lax.scan(step, state, (chunks4(q), chunks4(k), chunks4(v), chunks3(i_pre), chunks3(log_f)))
    return h.transpose(1, 0, 3, 2, 4).reshape(B_, T, H, dv), state


def mixer_global_gqa(z, zc, p, cos, sin, ctx_out):
    q = apply_rope(rms_norm(heads(z[0]), p['g_q_a']), cos, sin)
    k = apply_rope(rms_norm(heads(z[1]), p['g_k_a']), cos, sin)
    v = heads(z[2])
    k_c = rms_norm(heads(zc[1]), p['g_k_a'])
    v_c = heads(zc[2])
    o = global_block_attention(q, k, v, k_c, v_c)
    o_c = context_attention(rms_norm(heads(zc[0]), p['g_q_a']), k_c, v_c, None) if ctx_out else None
    return o, o_c


def mixer_window_gqa(z, zc, p, cos, sin, ctx_out):
    q = apply_rope(rms_norm(heads(z[0]), p['g_q_b']), cos, sin)
    k = apply_rope(rms_norm(heads(z[1]), p['g_k_b']), cos, sin)
    v = heads(z[2])
    k_c = rms_norm(heads(zc[1]), p['g_k_b'])
    v_c = heads(zc[2])
    o = window_block_attention(q, k, v, k_c, v_c, p['sink_b'])
    o_c = context_attention(rms_norm(heads(zc[0]), p['g_q_b']), k_c, v_c, p['sink_b']) if ctx_out else None
    return o, o_c


def mixer_gla(z, zc, p):
    def prep(parts):
        q, k, v, r, g = parts
        return (heads(q, GLA_DK) * GLA_DK ** -0.5, heads(k, GLA_DK), heads(v), heads(r), g)

    lat, ctx = prep(z), prep(zc)

    def log_decay(g, d):
        g_d = g[..., d * GLA_GATE_RANK:(d + 1) * GLA_GATE_RANK]
        pre = g_d @ p['w_gla_gate'][d] + p['b_gla_gate'][d]
        return heads(jax.nn.log_sigmoid(pre.astype(jnp.float32)) / GLA_GATE_TAU, GLA_DK)

    B_ = lat[0].shape[0]
    outs, outs_c = [], []
    for d in range(2):
        fl = _flip if d == 1 else _ident
        s0 = jnp.zeros((B_, GROUP_HEADS, GLA_DK, HEAD_DIM), jnp.float32)
        o_c, s_ctx = gla_scan(fl(ctx[0]), fl(ctx[1]), fl(ctx[2]), fl(log_decay(ctx[4], d)), s0)
        o_l, _ = gla_scan(fl(lat[0]), fl(lat[1]), fl(lat[2]), fl(log_decay(lat[4], d)), s_ctx)
        outs.append(fl(o_l))
        outs_c.append(fl(o_c))

    def finish(o, r):
        y = rms_norm(o, p['g_gla_out']) * jax.nn.silu(r.astype(jnp.float32))
        return y.reshape(y.shape[:2] + (GROUP_WIDTH,)).astype(r.dtype)

    return finish(outs[0] + outs[1], lat[3]), finish(outs_c[0] + outs_c[1], ctx[3])


def mixer_mlstm(z, zc, p):
    def prep(parts):
        q, k, v, o, i_pre, f_pre = parts
        return (heads(q), heads(k) * HEAD_DIM ** -0.5, heads(v), heads(o), i_pre, f_pre)

    lat, ctx = prep(z), prep(zc)

    def gates(parts, d):
        sl = slice(d * GROUP_HEADS, (d + 1) * GROUP_HEADS)
        ig = parts[4][..., sl].astype(jnp.float32) + p['b_mlstm_i'][d].astype(jnp.float32)
        lf = jax.nn.log_sigmoid(parts[5][..., sl].astype(jnp.float32) + p['b_mlstm_f'][d].astype(jnp.float32))
        return ig, lf

    B_ = lat[0].shape[0]
    outs, outs_c = [], []
    for d in range(2):
        fl = _flip if d == 1 else _ident
        st0 = (jnp.zeros((B_, GROUP_HEADS, HEAD_DIM, HEAD_DIM), jnp.float32),
               jnp.zeros((B_, GROUP_HEADS, HEAD_DIM), jnp.float32),
               jnp.zeros((B_, GROUP_HEADS), jnp.float32))
        ig_c, lf_c = gates(ctx, d)
        ig_l, lf_l = gates(lat, d)
        h_c, st_ctx = mlstm_scan(fl(ctx[0]), fl(ctx[1]), fl(ctx[2]), fl(ig_c), fl(lf_c), st0)
        h_l, _ = mlstm_scan(fl(lat[0]), fl(lat[1]), fl(lat[2]), fl(ig_l), fl(lf_l), st_ctx)
        outs.append(fl(h_l))
        outs_c.append(fl(h_c))

    def finish(h, o):
        y = jax.nn.sigmoid(o.astype(jnp.float32)) * rms_norm(h, p['g_mlstm_out'])
        return y.reshape(y.shape[:2] + (GROUP_WIDTH,)).astype(o.dtype)

    return finish(outs[0] + outs[1], lat[3]), finish(outs_c[0] + outs_c[1], ctx[3])


def squared_relu_mlp(h, w1, w2):
    return jnp.square(jax.nn.relu(h @ w1)) @ w2


def hybrid_layer(x, xc, mod, mod_c, p, cos, sin, ctx_out):
    sh1, sc1, ga1, sh2, sc2, ga2 = jnp.split(mod[:, None, :], N_MOD, axis=-1)
    sh1c, sc1c, ga1c, sh2c, sc2c, ga2c = jnp.split(mod_c, N_MOD, axis=-1)
    offs = np.cumsum(IN_SPLITS)[:-1].tolist()
    h = rms_norm(x, p['g_norm1']) * (1.0 + sc1) + sh1
    hc = rms_norm(xc, p['g_norm1']) * (1.0 + sc1c) + sh1c
    z = jnp.split(h @ p['w_in'], offs, axis=-1)
    zc = jnp.split(hc @ p['w_in'], offs, axis=-1)
    o_a, oc_a = mixer_global_gqa(z[0:3], zc[0:3], p, cos, sin, ctx_out)
    o_b, oc_b = mixer_window_gqa(z[3:6], zc[3:6], p, cos, sin, ctx_out)
    o_c, oc_c = mixer_gla(z[6:11], zc[6:11], p)
    o_d, oc_d = mixer_mlstm(z[11:17], zc[11:17], p)
    y = jnp.concatenate([o_a, o_b, o_c, o_d], axis=-1) @ p['w_out']
    x = x + ga1 * y
    x = x + ga2 * squared_relu_mlp(rms_norm(x, p['g_norm2']) * (1.0 + sc2) + sh2, p['w_mlp1'], p['w_mlp2'])
    if ctx_out:
        yc = jnp.concatenate([oc_a, oc_b, oc_c, oc_d], axis=-1) @ p['w_out']
        xc = xc + ga1c * yc
        xc = xc + ga2c * squared_relu_mlp(rms_norm(xc, p['g_norm2']) * (1.0 + sc2c) + sh2c, p['w_mlp1'], p['w_mlp2'])
    return x, xc


def setup_inputs(seed: int = 0) -> dict:
    key = jax.random.key(seed)
    ks = jax.random.split(key, 24)
    f32 = jnp.float32
    nrm = lambda k, shape, s: jax.random.normal(k, shape, f32) * s
    D = D_MODEL
    return {
        'x': nrm(ks[0], (BATCH, SEQ, D), 1.0),
        'c': nrm(ks[1], (BATCH, D), 1.0),
        'ctx': nrm(ks[2], (BATCH, CTX_LEN, D), 1.0),
        'c_ctx': nrm(ks[3], (D,), 1.0),
        'w_mod': nrm(ks[4], (DEPTH, D, N_MOD * D), 0.5 * D ** -0.5),
        'b_mod': nrm(ks[5], (DEPTH, N_MOD * D), 0.02),
        'g_norm1': 1.0 + nrm(ks[6], (DEPTH, D), 0.02),
        'g_norm2': 1.0 + nrm(ks[7], (DEPTH, D), 0.02),
        'w_in': nrm(ks[8], (DEPTH, D, IN_WIDTH), D ** -0.5),
        'g_q_a': 1.0 + nrm(ks[9], (DEPTH, HEAD_DIM), 0.02),
        'g_k_a': 1.0 + nrm(ks[10], (DEPTH, HEAD_DIM), 0.02),
        'g_q_b': 1.0 + nrm(ks[11], (DEPTH, HEAD_DIM), 0.02),
        'g_k_b': 1.0 + nrm(ks[12], (DEPTH, HEAD_DIM), 0.02),
        'sink_b': nrm(ks[13], (DEPTH, GROUP_HEADS), 0.5),
        'w_gla_gate': nrm(ks[14], (DEPTH, 2, GLA_GATE_RANK, GLA_KW), GLA_GATE_RANK ** -0.5),
        'b_gla_gate': nrm(ks[15], (DEPTH, 2, GLA_KW), 0.1),
        'g_gla_out': 1.0 + nrm(ks[16], (DEPTH, HEAD_DIM), 0.02),
        'b_mlstm_i': nrm(ks[17], (DEPTH, 2, GROUP_HEADS), 0.1),
        'b_mlstm_f': 3.0 + nrm(ks[18], (DEPTH, 2, GROUP_HEADS), 0.5),
        'g_mlstm_out': 1.0 + nrm(ks[19], (DEPTH, HEAD_DIM), 0.02),
        'w_out': nrm(ks[20], (DEPTH, D, D), D ** -0.5),
        'w_mlp1': nrm(ks[21], (DEPTH, D, D_FF), D ** -0.5),
        'w_mlp2': nrm(ks[22], (DEPTH, D_FF, D), D_FF ** -0.5),
    }


def reference(x, c, ctx, c_ctx, w_mod, b_mod, g_norm1, g_norm2, w_in, g_q_a, g_k_a, g_q_b, g_k_b, sink_b,
              w_gla_gate, b_gla_gate, g_gla_out, b_mlstm_i, b_mlstm_f, g_mlstm_out, w_out, w_mlp1, w_mlp2):
    cos, sin = axial_rope(x.shape[1])
    s_c = jax.nn.silu(c)
    s_cc = jax.nn.silu(c_ctx)
    xc = ctx
    for l in range(DEPTH):
        mod = s_c @ w_mod[l] + b_mod[l]
        mod_c = s_cc @ w_mod[l] + b_mod[l]
        p = {'g_norm1': g_norm1[l], 'g_norm2': g_norm2[l], 'w_in': w_in[l],
             'g_q_a': g_q_a[l], 'g_k_a': g_k_a[l], 'g_q_b': g_q_b[l], 'g_k_b': g_k_b[l], 'sink_b': sink_b[l],
             'w_gla_gate': w_gla_gate[l], 'b_gla_gate': b_gla_gate[l], 'g_gla_out': g_gla_out[l],
             'b_mlstm_i': b_mlstm_i[l], 'b_mlstm_f': b_mlstm_f[l], 'g_mlstm_out': g_mlstm_out[l],
             'w_out': w_out[l], 'w_mlp1': w_mlp1[l], 'w_mlp2': w_mlp2[l]}
        x, xc = hybrid_layer(x, xc, mod, mod_c, p, cos, sin, l < DEPTH - 1)
    return x
```

```python
import functools

import numpy as np
import jax
import jax.numpy as jnp
from jax import lax
from jax.experimental import pallas as pl
from jax.experimental.pallas import tpu as pltpu

F32 = jnp.float32
BF16 = jnp.bfloat16

HEAD_DIM = 64
GROUP_HEADS = 4
GROUP_WIDTH = GROUP_HEADS * HEAD_DIM
KV_HEADS = 2
GRID_W = 64
Q_BLOCK = 128
ROPE_THETA = 10000.0
GLA_DK = 32
GLA_KW = GROUP_HEADS * GLA_DK
GLA_GATE_RANK = 16
GLA_GATE_TAU = 16.0
N_MOD = 6
EPS = 1e-6
NEG = -1e30
CHUNK = 64
LANES = 128
MOD_ROWS = 16
VMEM_LIMIT = 56 * 1024 * 1024

W_A = 3 * GROUP_WIDTH
W_C = 896
W_D = 1152
W_ALL = 2 * W_A + W_C + W_D


def _cparams(sem):
    return pltpu.CompilerParams(dimension_semantics=sem, vmem_limit_bytes=VMEM_LIMIT)


def _dot(a, b):
    return jnp.dot(a, b, preferred_element_type=F32)


def _dot_nt(a, b):
    return lax.dot_general(a, b, (((1,), (1,)), ((), ())), preferred_element_type=F32)


def _dot_tn(a, b):
    return lax.dot_general(a, b, (((0,), (0,)), ((), ())), preferred_element_type=F32)


def _split3(x):
    hi = x.astype(BF16)
    r1 = x - hi.astype(F32)
    mid = r1.astype(BF16)
    lo = (r1 - mid.astype(F32)).astype(BF16)
    return hi, mid, lo


def _dot01(a01, x):
    hi, mid, lo = _split3(x)
    return _dot(a01, hi) + _dot(a01, mid) + _dot(a01, lo)


def _dot01_r(x, b01):
    hi, mid, lo = _split3(x)
    return _dot(hi, b01) + _dot(mid, b01) + _dot(lo, b01)


def _log_sigmoid(x):
    return jnp.minimum(x, 0.0) - jnp.log(1.0 + jnp.exp(-jnp.abs(x)))


def _sigmoid(x):
    return 1.0 / (1.0 + jnp.exp(-x))


def _iota(shape, dim):
    return lax.broadcasted_iota(jnp.int32, shape, dim)


def _group_ones(n, group):
    return (_iota((n, n), 0) // group == _iota((n, n), 1) // group).astype(BF16)


def _mod_kernel(c_ref, w_ref, b_ref, o_ref):
    c = c_ref[...]
    s = (c * _sigmoid(c)).astype(BF16)
    o_ref[0] = _dot(s, w_ref[0].astype(BF16)) + b_ref[0]


def _modulation(cc, w_mod, b_mod):
    depth, d, n = w_mod.shape
    tn = 1536
    return pl.pallas_call(
        _mod_kernel,
        out_shape=jax.ShapeDtypeStruct((depth, MOD_ROWS, n), F32),
        grid=(depth, n // tn),
        in_specs=[pl.BlockSpec((MOD_ROWS, d), lambda l, j: (0, 0)),
                  pl.BlockSpec((1, d, tn), lambda l, j: (l, 0, j)),
                  pl.BlockSpec((1, 1, tn), lambda l, j: (l, 0, j))],
        out_specs=pl.BlockSpec((1, MOD_ROWS, tn), lambda l, j: (l, 0, j)),
        compiler_params=_cparams(("arbitrary", "arbitrary")),
        name="modulation",
    )(cc, w_mod, b_mod.reshape(depth, 1, n))


def _qk_norm_rope(z, gain, cos, sin, ones_bd, scale):
    ss = _dot01_r(z * z, ones_bd)
    y = z * lax.rsqrt(ss * (1.0 / HEAD_DIM) + EPS) * gain
    first_half = (_iota((z.shape[0], LANES), 1) % HEAD_DIM) < (HEAD_DIM // 2)
    outs = []
    for cb in range(z.shape[1] // LANES):
        yc = y[:, cb * LANES:(cb + 1) * LANES]
        partner = jnp.where(first_half, pltpu.roll(yc, LANES - HEAD_DIM // 2, 1),
                            pltpu.roll(yc, HEAD_DIM // 2, 1))
        outs.append((yc * cos + partner * sin) * scale)
    return jnp.concatenate(outs, axis=1)


def _proj_kernel(x_ref, mod_ref, g1_ref, w_ref, gains_ref, cos_ref, sin_ref, wg_ref, bg_ref, bm_ref,
                 qkva_ref, qkvb_ref, zc_ref, zd_ref):
    d = x_ref.shape[2]
    tm = x_ref.shape[1]
    x = x_ref[0]
    mod = mod_ref[0, 0]
    sh1 = mod[:, 0:d]
    sc1 = mod[:, d:2 * d]
    ms = jnp.mean(x * x, axis=-1, keepdims=True)
    h = (x * lax.rsqrt(ms + EPS)) * g1_ref[...]
    hb = (h * (1.0 + sc1) + sh1).astype(BF16)

    ones_bd = _group_ones(GROUP_WIDTH, HEAD_DIM)
    cos = cos_ref[...]
    sin = sin_ref[...]
    gains = gains_ref[...]
    gw = GROUP_WIDTH

    for n, out_ref in enumerate((qkva_ref, qkvb_ref)):
        z = _dot(hb, w_ref[:, n * W_A:(n + 1) * W_A])
        q = _qk_norm_rope(z[:, 0:gw], gains[2 * n:2 * n + 1], cos, sin, ones_bd, HEAD_DIM ** -0.5)
        k = _qk_norm_rope(z[:, gw:2 * gw], gains[2 * n + 1:2 * n + 2], cos, sin, ones_bd, 1.0)
        out_ref[0, :, 0:gw] = q.astype(BF16)
        out_ref[0, :, gw:2 * gw] = k.astype(BF16)
        out_ref[0, :, 2 * gw:3 * gw] = z[:, 2 * gw:3 * gw].astype(BF16)

    z = _dot(hb, w_ref[:, 2 * W_A:2 * W_A + W_C])
    zc_ref[0, :, 0:GLA_KW] = z[:, 0:GLA_KW] * (GLA_DK ** -0.5)
    zc_ref[0, :, GLA_KW:768] = z[:, GLA_KW:768]
    pre = _dot(z[:, 768:896].astype(BF16), wg_ref[...]) + bg_ref[...]
    zc_ref[0, :, 768:1024] = _log_sigmoid(pre) * (1.0 / GLA_GATE_TAU)

    z = _dot(hb, w_ref[:, 2 * W_A + W_C:W_ALL])
    zd_ref[0, :, 0:gw] = z[:, 0:gw]
    zd_ref[0, :, gw:2 * gw] = z[:, gw:2 * gw] * (HEAD_DIM ** -0.5)
    zd_ref[0, :, 2 * gw:4 * gw] = z[:, 2 * gw:4 * gw]
    gates = z[:, 4 * gw:4 * gw + LANES] + bm_ref[...]
    is_input_gate = _iota((tm, LANES), 1) < 2 * GROUP_HEADS
    zd_ref[0, :, 4 * gw:4 * gw + LANES] = jnp.where(is_input_gate, gates, _log_sigmoid(gates))


def _projection(xt, modsel, g1, w_big, gains, cos_t, sin_t, wg, bg, bm, tm, ncb):
    b, t, d = xt.shape
    nblk = t // tm
    const = lambda bi, j: (0, 0)
    return pl.pallas_call(
        _proj_kernel,
        out_shape=(jax.ShapeDtypeStruct((b, t, W_A), BF16),
                   jax.ShapeDtypeStruct((b, t, W_A), BF16),
                   jax.ShapeDtypeStruct((b, t, 1024), F32),
                   jax.ShapeDtypeStruct((b, t, W_D), F32)),
        grid=(b, nblk),
        in_specs=[pl.BlockSpec((1, tm, d), lambda bi, j: (bi, j, 0)),
                  pl.BlockSpec((1, 1, 1, N_MOD * d), lambda bi, j: (bi, jnp.where(j >= ncb, 1, 0), 0, 0)),
                  pl.BlockSpec((1, d), const),
                  pl.BlockSpec((d, W_ALL), const),
                  pl.BlockSpec((4, GROUP_WIDTH), const),
                  pl.BlockSpec((tm, LANES), lambda bi, j: (j, 0)),
                  pl.BlockSpec((tm, LANES), lambda bi, j: (j, 0)),
                  pl.BlockSpec((LANES, 2 * GLA_KW), const),
                  pl.BlockSpec((1, 2 * GLA_KW), const),
                  pl.BlockSpec((1, LANES), const)],
        out_specs=(pl.BlockSpec((1, tm, W_A), lambda bi, j: (bi, j, 0)),
                   pl.BlockSpec((1, tm, W_A), lambda bi, j: (bi, j, 0)),
                   pl.BlockSpec((1, tm, 1024), lambda bi, j: (bi, j, 0)),
                   pl.BlockSpec((1, tm, W_D), lambda bi, j: (bi, j, 0))),
        compiler_params=_cparams(("arbitrary", "arbitrary")),
        name="projection",
    )(xt, modsel, g1, w_big, gains, cos_t, sin_t, wg, bg, bm)


def _stack_heads(q128):
    low = _iota(q128.shape, 1) < HEAD_DIM
    zero = jnp.zeros_like(q128)
    return jnp.concatenate([jnp.where(low, q128, zero), jnp.where(low, zero, q128)], axis=0)


def _unstack_heads(acc, tq):
    low = _iota((tq, LANES), 1) < HEAD_DIM
    return jnp.where(low, acc[:tq], acc[tq:])


def _attn_global_kernel(q_ref, k_ref, v_ref, o_ref, *, lc, tk, nk):
    tq = q_ref.shape[1]
    j = pl.program_id(1)
    n_lat = jnp.where(j * tq >= lc, nk, 0)
    outs = []
    for g in range(KV_HEADS):
        cols = slice(g * LANES, (g + 1) * LANES)
        qs = _stack_heads(q_ref[0, :, cols])
        s = _dot_nt(qs, k_ref[0, 0:lc, cols])
        m = jnp.max(s, axis=-1, keepdims=True)
        p = jnp.exp(s - m)
        l = jnp.sum(p, axis=-1, keepdims=True)
        acc = _dot(p.astype(BF16), v_ref[0, 0:lc, cols])

        def body(i, carry, cols=cols, qs=qs):
            m, l, acc = carry
            start = pl.multiple_of(lc + i * tk, LANES)
            s = _dot_nt(qs, k_ref[0, pl.ds(start, tk), cols])
            m_new = jnp.maximum(m, jnp.max(s, axis=-1, keepdims=True))
            alpha = jnp.exp(m - m_new)
            p = jnp.exp(s - m_new)
            l = alpha * l + jnp.sum(p, axis=-1, keepdims=True)
            acc = alpha * acc + _dot(p.astype(BF16), v_ref[0, pl.ds(start, tk), cols])
            return m_new, l, acc

        m, l, acc = lax.fori_loop(0, n_lat, body, (m, l, acc))
        outs.append(_unstack_heads(acc / l, tq))
    o_ref[0] = jnp.concatenate(outs, axis=1).astype(BF16)


def _attn_global(qkv, lc, tq, tk):
    b, t, _ = qkv.shape
    nk = (t - lc) // tk
    return pl.pallas_call(
        functools.partial(_attn_global_kernel, lc=lc, tk=tk, nk=nk),
        out_shape=jax.ShapeDtypeStruct((b, t, GROUP_WIDTH), BF16),
        grid=(b, t // tq),
        in_specs=[pl.BlockSpec((1, tq, GROUP_WIDTH), lambda bi, j: (bi, j, 0)),
                  pl.BlockSpec((1, t, GROUP_WIDTH), lambda bi, j: (bi, 0, 1)),
                  pl.BlockSpec((1, t, GROUP_WIDTH), lambda bi, j: (bi, 0, 2))],
        out_specs=pl.BlockSpec((1, tq, GROUP_WIDTH), lambda bi, j: (bi, j, 0)),
        compiler_params=_cparams(("arbitrary", "arbitrary")),
        name="attn_global",
    )(qkv, qkv, qkv)


def _attn_window_kernel(sink_ref, q_ref, kp_ref, kc_ref, kn_ref, vp_ref, vc_ref, vn_ref, kx_ref, vx_ref, o_ref,
                        *, nctx, nblk):
    tq = Q_BLOCK
    j = pl.program_id(1)
    empty = 2 * tq
    lo_prev = jnp.where(j - 1 >= nctx, 0, empty)
    lo_cur = jnp.where(j >= nctx, -tq, empty)
    lo_next = jnp.where(jnp.logical_and(j >= nctx, j + 1 <= nblk - 1), -tq, empty)
    iq = _iota((2 * tq, 3 * tq), 0) % tq
    col = _iota((2 * tq, 3 * tq), 1)
    diff = col % tq - iq
    blk = col // tq
    lo = jnp.where(blk == 0, lo_prev, jnp.where(blk == 1, lo_cur, lo_next))
    hi = jnp.where(blk == 2, 0, tq)
    ok = jnp.logical_and(diff >= lo, diff <= hi)
    top = _iota((2 * tq, 1), 0) < tq
    outs = []
    for g in range(KV_HEADS):
        cols = slice(g * LANES, (g + 1) * LANES)
        qs = _stack_heads(q_ref[0, :, cols])
        kw = jnp.concatenate([kp_ref[0, :, cols], kc_ref[0, :, cols], kn_ref[0, :, cols]], axis=0)
        vw = jnp.concatenate([vp_ref[0, :, cols], vc_ref[0, :, cols], vn_ref[0, :, cols]], axis=0)
        s_w = jnp.where(ok, _dot_nt(qs, kw), NEG)
        s_x = _dot_nt(qs, kx_ref[0, :, cols])
        sink = jnp.where(top, sink_ref[2 * g], sink_ref[2 * g + 1])
        m = jnp.maximum(jnp.maximum(jnp.max(s_w, axis=-1, keepdims=True),
                                    jnp.max(s_x, axis=-1, keepdims=True)), sink)
        p_w = jnp.exp(s_w - m)
        p_x = jnp.exp(s_x - m)
        l = (jnp.sum(p_w, axis=-1, keepdims=True) + jnp.sum(p_x, axis=-1, keepdims=True)
             + jnp.exp(sink - m))
        acc = _dot(p_w.astype(BF16), vw) + _dot(p_x.astype(BF16), vx_ref[0, :, cols])
        outs.append(_unstack_heads(acc / l, tq))
    o_ref[0] = jnp.concatenate(outs, axis=1).astype(BF16)


def _attn_window(qkv, sink, lc):
    b, t, _ = qkv.shape
    tq = Q_BLOCK
    nblk = t // tq
    nctx = lc // tq
    gw = GROUP_WIDTH

    def blk(col, shift):
        return pl.BlockSpec((1, tq, gw), lambda bi, j: (bi, jnp.clip(j + shift, 0, nblk - 1), col))

    return pl.pallas_call(
        functools.partial(_attn_window_kernel, nctx=nctx, nblk=nblk),
        out_shape=jax.ShapeDtypeStruct((b, t, gw), BF16),
        grid=(b, nblk),
        in_specs=[pl.BlockSpec(memory_space=pltpu.SMEM),
                  blk(0, 0),
                  blk(1, -1), blk(1, 0), blk(1, 1),
                  blk(2, -1), blk(2, 0), blk(2, 1),
                  pl.BlockSpec((1, lc, gw), lambda bi, j: (bi, 0, 1)),
                  pl.BlockSpec((1, lc, gw), lambda bi, j: (bi, 0, 2))],
        out_specs=pl.BlockSpec((1, tq, gw), lambda bi, j: (bi, j, 0)),
        compiler_params=_cparams(("arbitrary", "arbitrary")),
        name="attn_window",
    )(sink, qkv, qkv, qkv, qkv, qkv, qkv, qkv, qkv, qkv)


def _chunk_tri(rows, reverse):
    t = _iota((rows, rows), 0)
    u = _iota((rows, rows), 1)
    same = t // CHUNK == u // CHUNK
    return jnp.logical_and(same, (u >= t) if reverse else (u <= t)).astype(BF16)


def _tile4(x):
    return jnp.concatenate([x, x, x, x], axis=0)


def _rev_block(i, ncb, nblk):
    return jnp.where(i < ncb, ncb - 1 - i, nblk - 1 - (i - ncb))


def _gla_direction(qkv_ref, la_ref, o_ref, s_ref, reverse):
    rows = qkv_ref.shape[1]
    nchunk = rows // CHUNK
    b = _dot01(_chunk_tri(rows, reverse), la_ref[0])
    q = qkv_ref[0, :, 0:GLA_KW]
    k = qkv_ref[0, :, GLA_KW:2 * GLA_KW]
    v = qkv_ref[0, :, 2 * GLA_KW:2 * GLA_KW + GROUP_WIDTH].astype(BF16)
    same_kd = _iota((GROUP_WIDTH, GLA_KW), 0) // CHUNK == _iota((GROUP_WIDTH, GLA_KW), 1) // GLA_DK
    same_kv = _iota((GROUP_WIDTH, GROUP_WIDTH), 0) // CHUNK == _iota((GROUP_WIDTH, GROUP_WIDTH), 1) // HEAD_DIM
    t_pos = _iota((CHUNK, GROUP_WIDTH), 0)
    s_pos = _iota((CHUNK, GROUP_WIDTH), 1) % CHUNK
    causal = (s_pos >= t_pos) if reverse else (s_pos <= t_pos)
    end = 0 if reverse else CHUNK - 1
    state = s_ref[...]
    order = range(nchunk - 1, -1, -1) if reverse else range(nchunk)
    for c in order:
        sl = slice(c * CHUNK, (c + 1) * CHUNK)
        bc = b[sl]
        b_end = bc[end:end + 1]
        b_mid = bc[CHUNK // 2:CHUNK // 2 + 1]
        qc, kc, vc = q[sl], k[sl], v[sl]
        q_inter = (qc * jnp.exp(bc)).astype(BF16)
        q_intra = (qc * jnp.exp(bc - b_mid)).astype(BF16)
        k_intra = (kc * jnp.exp(b_mid - bc)).astype(BF16)
        k_state = (kc * jnp.exp(b_end - bc)).astype(BF16)
        k_bd = jnp.where(same_kd, _tile4(k_intra), jnp.zeros((), BF16))
        v_bd = jnp.where(same_kv, _tile4(vc), jnp.zeros((), BF16))
        scores = jnp.where(causal, _dot_nt(q_intra, k_bd), 0.0).astype(BF16)
        o_ref[0, sl, :] = _dot(scores, v_bd) + _dot_nt(q_inter, state.astype(BF16))
        state = state * jnp.exp(b_end) + jnp.where(same_kd, _dot_tn(vc, k_state), 0.0)
    s_ref[...] = state


def _gla_kernel(qkv_f_ref, la_f_ref, qkv_b_ref, la_b_ref, of_ref, ob_ref, sf_ref, sb_ref):
    @pl.when(pl.program_id(1) == 0)
    def _():
        sf_ref[...] = jnp.zeros_like(sf_ref)
        sb_ref[...] = jnp.zeros_like(sb_ref)

    _gla_direction(qkv_f_ref, la_f_ref, of_ref, sf_ref, False)
    _gla_direction(qkv_b_ref, la_b_ref, ob_ref, sb_ref, True)


def _gla_scan(zc, lc, rows):
    b, t, _ = zc.shape
    nblk = t // rows
    ncb = lc // rows
    fwd = lambda col: (lambda bi, i: (bi, i, col))
    rev = lambda col: (lambda bi, i: (bi, _rev_block(i, ncb, nblk), col))
    qkv_w = 2 * GLA_KW + GROUP_WIDTH
    return pl.pallas_call(
        _gla_kernel,
        out_shape=(jax.ShapeDtypeStruct((b, t, GROUP_WIDTH), F32),
                   jax.ShapeDtypeStruct((b, t, GROUP_WIDTH), F32)),
        grid=(b, nblk),
        in_specs=[pl.BlockSpec((1, rows, qkv_w), fwd(0)),
                  pl.BlockSpec((1, rows, GLA_KW), fwd(6)),
                  pl.BlockSpec((1, rows, qkv_w), rev(0)),
                  pl.BlockSpec((1, rows, GLA_KW), rev(7))],
        out_specs=(pl.BlockSpec((1, rows, GROUP_WIDTH), fwd(0)),
                   pl.BlockSpec((1, rows, GROUP_WIDTH), rev(0))),
        scratch_shapes=[pltpu.VMEM((GROUP_WIDTH, GLA_KW), F32),
                        pltpu.VMEM((GROUP_WIDTH, GLA_KW), F32)],
        compiler_params=_cparams(("arbitrary", "arbitrary")),
        name="gla_scan",
    )(zc, zc, zc, zc)


def _expand_heads(g, base, rows):
    hid = _iota((rows, GROUP_WIDTH), 1) // HEAD_DIM
    cols = [jnp.broadcast_to(g[:, base + h:base + h + 1], (rows, GROUP_WIDTH)) for h in range(GROUP_HEADS)]
    return jnp.where(hid == 0, cols[0], jnp.where(hid == 1, cols[1], jnp.where(hid == 2, cols[2], cols[3])))


def _mlstm_direction(qkv_ref, gate_ref, o_ref, c_ref, n_ref, m_ref, direction):
    reverse = direction == 1
    rows = qkv_ref.shape[1]
    nchunk = rows // CHUNK
    gw = GROUP_WIDTH
    g = gate_ref[0]
    ig = _expand_heads(g, GROUP_HEADS * direction, rows)
    lf = _expand_heads(g, 2 * GROUP_HEADS + GROUP_HEADS * direction, rows)
    f_cum = _dot01(_chunk_tri(rows, reverse), lf)
    q = qkv_ref[0, :, 0:gw]
    k = qkv_ref[0, :, gw:2 * gw]
    v = qkv_ref[0, :, 2 * gw:3 * gw].astype(BF16)

    same_head = _iota((gw, gw), 0) // HEAD_DIM == _iota((gw, gw), 1) // HEAD_DIM
    ones_bd = same_head.astype(BF16)
    hid = _iota((CHUNK, gw), 1) // HEAD_DIM
    t_pos = _iota((CHUNK, gw), 0)
    s_pos = _iota((CHUNK, gw), 1) % CHUNK
    causal = (s_pos >= t_pos) if reverse else (s_pos <= t_pos)
    diag = (s_pos == t_pos).astype(F32)
    ones_cc = jnp.ones((CHUNK, CHUNK), BF16)
    end = 0 if reverse else CHUNK - 1

    c_state = c_ref[...]
    n_state = n_ref[...]
    m_state = m_ref[...]
    order = range(nchunk - 1, -1, -1) if reverse else range(nchunk)
    for c in order:
        sl = slice(c * CHUNK, (c + 1) * CHUNK)
        fc, ic = f_cum[sl], ig[sl]
        qc, kc, vc = q[sl], k[sl], v[sl]
        qb = qc.astype(BF16)
        key_term = _dot01(ones_cc, (fc - ic) * diag)
        logw = jnp.where(causal, fc - key_term, NEG)
        log_inter = fc + m_state
        row_max = jnp.full((CHUNK, gw), NEG, F32)
        for h in range(GROUP_HEADS):
            mh = jnp.max(jnp.where(hid == h, logw, NEG), axis=-1, keepdims=True)
            row_max = jnp.where(hid == h, mh, row_max)
        m_t = jnp.maximum(log_inter, row_max)
        w = jnp.exp(logw - m_t)
        w_inter = jnp.exp(log_inter - m_t)
        k_bd = jnp.where(same_head, _tile4(kc.astype(BF16)), jnp.zeros((), BF16))
        v_bd = jnp.where(same_head, _tile4(vc), jnp.zeros((), BF16))
        qk = _dot_nt(qb, k_bd) * w
        num = _dot(qk.astype(BF16), v_bd) + w_inter * _dot(qb, c_state.astype(BF16))
        den = _dot01_r(qk, ones_bd) + w_inter * _dot((qc * n_state).astype(BF16), ones_bd)
        o_ref[0, sl, :] = num / jnp.maximum(jnp.abs(den), jnp.exp(-m_t))
        m_new = m_t[end:end + 1]
        decay = w_inter[end:end + 1]
        k_end = kc * jnp.exp(fc[end:end + 1] - fc + ic - m_new)
        c_state = c_state * decay + jnp.where(same_head, _dot_tn(k_end.astype(BF16), vc), 0.0)
        n_state = n_state * decay + jnp.sum(k_end, axis=0, keepdims=True)
        m_state = m_new
    c_ref[...] = c_state
    n_ref[...] = n_state
    m_ref[...] = m_state


def _mlstm_kernel(qkv_f_ref, gate_f_ref, qkv_b_ref, gate_b_ref, of_ref, ob_ref,
                  cf_ref, nf_ref, mf_ref, cb_ref, nb_ref, mb_ref):
    @pl.when(pl.program_id(1) == 0)
    def _():
        for r in (cf_ref, nf_ref, mf_ref, cb_ref, nb_ref, mb_ref):
            r[...] = jnp.zeros_like(r)

    _mlstm_direction(qkv_f_ref, gate_f_ref, of_ref, cf_ref, nf_ref, mf_ref, 0)
    _mlstm_direction(qkv_b_ref, gate_b_ref, ob_ref, cb_ref, nb_ref, mb_ref, 1)


def _mlstm_scan(zd, lc, rows):
    b, t, _ = zd.shape
    nblk = t // rows
    ncb = lc // rows
    gw = GROUP_WIDTH
    fwd = lambda col: (lambda bi, i: (bi, i, col))
    rev = lambda col: (lambda bi, i: (bi, _rev_block(i, ncb, nblk), col))
    gate_col = 4 * gw // LANES
    state = [pltpu.VMEM((gw, gw), F32), pltpu.VMEM((1, gw), F32), pltpu.VMEM((1, gw), F32)]
    return pl.pallas_call(
        _mlstm_kernel,
        out_shape=(jax.ShapeDtypeStruct((b, t, gw), F32),
                   jax.ShapeDtypeStruct((b, t, gw), F32)),
        grid=(b, nblk),
        in_specs=[pl.BlockSpec((1, rows, 3 * gw), fwd(0)),
                  pl.BlockSpec((1, rows, LANES), fwd(gate_col)),
                  pl.BlockSpec((1, rows, 3 * gw), rev(0)),
                  pl.BlockSpec((1, rows, LANES), rev(gate_col))],
        out_specs=(pl.BlockSpec((1, rows, gw), fwd(0)),
                   pl.BlockSpec((1, rows, gw), rev(0))),
        scratch_shapes=state + state,
        compiler_params=_cparams(("arbitrary", "arbitrary")),
        name="mlstm_scan",
    )(zd, zd, zd, zd)


def _head_rms(o, gain, ones_bd):
    ss = _dot01_r(o * o, ones_bd)
    return o * lax.rsqrt(ss * (1.0 / HEAD_DIM) + EPS) * gain


def _out_kernel(x_ref, mod_ref, oa_ref, ob_ref, gf_ref, gb_ref, r_ref, mf_ref, mb_ref, og_ref,
                gains_ref, g2_ref, wo_ref, w1_ref, w2_ref, y_ref):
    d = x_ref.shape[2]
    x = x_ref[0]
    mod = mod_ref[0, 0]
    ga1 = mod[:, 2 * d:3 * d]
    sh2 = mod[:, 3 * d:4 * d]
    sc2 = mod[:, 4 * d:5 * d]
    ga2 = mod[:, 5 * d:6 * d]
    ones_bd = _group_ones(GROUP_WIDTH, HEAD_DIM)
    gains = gains_ref[...]

    r = r_ref[0]
    o_c = _head_rms(gf_ref[0] + gb_ref[0], gains[0:1], ones_bd) * (r * _sigmoid(r))
    o_d = _sigmoid(og_ref[0]) * _head_rms(mf_ref[0] + mb_ref[0], gains[1:2], ones_bd)
    o_cat = jnp.concatenate([oa_ref[0], ob_ref[0], o_c.astype(BF16), o_d.astype(BF16)], axis=1)
    x = x + ga1 * _dot(o_cat, wo_ref[...])

    ms = jnp.mean(x * x, axis=-1, keepdims=True)
    h = (x * lax.rsqrt(ms + EPS)) * g2_ref[...]
    hb = (h * (1.0 + sc2) + sh2).astype(BF16)
    hid = jnp.maximum(_dot(hb, w1_ref[...]), 0.0)
    hid = (hid * hid).astype(BF16)
    y_ref[0] = x + ga2 * _dot(hid, w2_ref[...])


def _out_mlp(xt, modsel, oa, ob, gf, gb, zc, mf, mb, zd, gains, g2, wo, w1, w2, tm, ncb, skip_blocks):
    b, t, d = xt.shape
    nblk = t // tm - skip_blocks
    gw = GROUP_WIDTH
    row = lambda col: (lambda bi, j: (bi, j + skip_blocks, col))
    const = lambda bi, j: (0, 0)
    resident = lambda shape: pl.BlockSpec(shape, const, pipeline_mode=pl.Buffered(1))
    return pl.pallas_call(
        _out_kernel,
        out_shape=jax.ShapeDtypeStruct((b, nblk * tm, d), F32),
        grid=(b, nblk),
        in_specs=[pl.BlockSpec((1, tm, d), row(0)),
                  pl.BlockSpec((1, 1, 1, N_MOD * d),
                               lambda bi, j: (bi, jnp.where(j + skip_blocks >= ncb, 1, 0), 0, 0)),
                  pl.BlockSpec((1, tm, gw), row(0)),
                  pl.BlockSpec((1, tm, gw), row(0)),
                  pl.BlockSpec((1, tm, gw), row(0)),
                  pl.BlockSpec((1, tm, gw), row(0)),
                  pl.BlockSpec((1, tm, gw), row(2)),
                  pl.BlockSpec((1, tm, gw), row(0)),
                  pl.BlockSpec((1, tm, gw), row(0)),
                  pl.BlockSpec((1, tm, gw), row(3)),
                  pl.BlockSpec((2, gw), const),
                  pl.BlockSpec((1, d), const),
                  resident((d, d)),
                  resident((d, 4 * d)),
                  resident((4 * d, d))],
        out_specs=pl.BlockSpec((1, tm, d), lambda bi, j: (bi, j, 0)),
        compiler_params=_cparams(("arbitrary", "arbitrary")),
        name="out_mlp",
    )(xt, modsel, oa, ob, gf, gb, zc, mf, mb, zd, gains, g2, wo, w1, w2)


def _projection_columns():
    gw, kw = GROUP_WIDTH, KV_HEADS * HEAD_DIM
    splits = (gw, kw, kw, gw, kw, kw, GLA_KW, GLA_KW, gw, gw, 2 * GLA_GATE_RANK,
              gw, gw, gw, gw, 2 * GROUP_HEADS, 2 * GROUP_HEADS)
    offs = np.concatenate([[0], np.cumsum(splits)])
    in_width = int(offs[-1])
    half_split = np.concatenate([np.arange(0, HEAD_DIM, 2), np.arange(1, HEAD_DIM, 2)])
    plain = np.arange(HEAD_DIM)
    cols = []
    for base in (0, 3):
        q0, k0, v0 = offs[base], offs[base + 1], offs[base + 2]
        cols += [q0 + h * HEAD_DIM + half_split for h in range(GROUP_HEADS)]
        cols += [k0 + g * HEAD_DIM + half_split for g in range(KV_HEADS) for _ in range(2)]
        cols += [v0 + g * HEAD_DIM + plain for g in range(KV_HEADS) for _ in range(2)]
    cols.append(np.arange(offs[6], offs[11]))
    cols.append(np.full(W_C - (offs[11] - offs[6]), in_width))
    cols.append(np.arange(offs[11], offs[17]))
    cols.append(np.full(W_D - (offs[17] - offs[11]), in_width))
    cols = np.concatenate(cols).astype(np.int32)
    assert cols.shape[0] == W_ALL
    return cols, in_width, half_split


def _rope_tables(seq, lc):
    rows = seq // GRID_W
    row = jnp.repeat(jnp.arange(rows, dtype=F32), GRID_W)
    col = jnp.tile(jnp.arange(GRID_W, dtype=F32), rows)
    n_freq = HEAD_DIM // 4
    inv = ROPE_THETA ** (-jnp.arange(n_freq, dtype=F32) / n_freq)
    ang = jnp.concatenate([row[:, None] * inv, col[:, None] * inv], axis=-1)
    cos, sin = jnp.cos(ang), jnp.sin(ang)
    cos_t = jnp.tile(cos, (1, 4))
    sin_t = jnp.tile(jnp.concatenate([-sin, sin], axis=-1), (1, 2))
    cos_t = jnp.concatenate([jnp.ones((lc, LANES), F32), cos_t], axis=0)
    sin_t = jnp.concatenate([jnp.zeros((lc, LANES), F32), sin_t], axis=0)
    return cos_t, sin_t


def kernel(x, c, ctx, c_ctx, w_mod, b_mod, g_norm1, g_norm2, w_in, g_q_a, g_k_a, g_q_b, g_k_b, sink_b,
           w_gla_gate, b_gla_gate, g_gla_out, b_mlstm_i, b_mlstm_f, g_mlstm_out, w_out, w_mlp1, w_mlp2):
    b, seq, d = x.shape
    lc = ctx.shape[1]
    depth = w_mod.shape[0]
    tm = 256
    assert d == 4 * GROUP_WIDTH and lc % tm == 0 and seq % tm == 0 and seq % GRID_W == 0 and b + 1 <= MOD_ROWS

    cols, in_width, half_split = _projection_columns()
    cos_t, sin_t = _rope_tables(seq, lc)

    cc = jnp.zeros((MOD_ROWS, d), F32).at[0:b].set(c).at[b].set(c_ctx)
    mod_all = _modulation(cc, w_mod, b_mod)

    xt = jnp.concatenate([ctx, x], axis=1)
    for l in range(depth):
        modsel = jnp.stack([jnp.broadcast_to(mod_all[l, b], (b, N_MOD * d)), mod_all[l, 0:b]], axis=1)
        modsel = modsel.reshape(b, 2, 1, N_MOD * d)

        w_ext = jnp.concatenate([w_in[l], jnp.zeros((d, 1), F32)], axis=1)
        w_big = jnp.take(w_ext, cols, axis=1).astype(BF16)
        tile4 = lambda g: jnp.tile(g[half_split], 4)
        gains_qk = jnp.stack([tile4(g_q_a[l]), tile4(g_k_a[l]), tile4(g_q_b[l]), tile4(g_k_b[l])])
        wg = jnp.zeros((LANES, 2 * GLA_KW), F32)
        wg = wg.at[0:GLA_GATE_RANK, 0:GLA_KW].set(w_gla_gate[l, 0])
        wg = wg.at[GLA_GATE_RANK:2 * GLA_GATE_RANK, GLA_KW:].set(w_gla_gate[l, 1]).astype(BF16)
        bg = b_gla_gate[l].reshape(1, 2 * GLA_KW)
        bm = jnp.zeros((1, LANES), F32)
        bm = bm.at[0, 0:2 * GROUP_HEADS].set(b_mlstm_i[l].reshape(-1))
        bm = bm.at[0, 2 * GROUP_HEADS:4 * GROUP_HEADS].set(b_mlstm_f[l].reshape(-1))

        qkva, qkvb, zc, zd = _projection(xt, modsel, g_norm1[l].reshape(1, d), w_big, gains_qk,
                                         cos_t, sin_t, wg, bg, bm, tm, lc // tm)
        oa = _attn_global(qkva, lc, tq=256, tk=512)
        ob = _attn_window(qkvb, sink_b[l], lc)
        gf, gb = _gla_scan(zc, lc, tm)
        mf, mb = _mlstm_scan(zd, lc, tm)

        gains_out = jnp.stack([jnp.tile(g_gla_out[l], 4), jnp.tile(g_mlstm_out[l], 4)])
        last = l == depth - 1
        xt = _out_mlp(xt, modsel, oa, ob, gf, gb, zc, mf, mb, zd, gains_out, g_norm2[l].reshape(1, d),
                      w_out[l].astype(BF16), w_mlp1[l].astype(BF16), w_mlp2[l].astype(BF16), tm,
                      ncb=lc // tm, skip_blocks=lc // tm if last else 0)
    return xt
```

```python
import functools

import numpy as np
import jax
import jax.numpy as jnp
from jax import lax
from jax.experimental import pallas as pl
from jax.experimental.pallas import tpu as pltpu

F32 = jnp.float32
BF16 = jnp.bfloat16

HEAD_DIM = 64
GROUP_HEADS = 4
GROUP_WIDTH = GROUP_HEADS * HEAD_DIM
KV_HEADS = 2
GRID_W = 64
Q_BLOCK = 128
ROPE_THETA = 10000.0
GLA_DK = 32
GLA_KW = GROUP_HEADS * GLA_DK
GLA_GATE_RANK = 16
GLA_GATE_TAU = 16.0
N_MOD = 6
EPS = 1e-6
LOG2E = 1.4426950408889634
NEG = -1e30
CHUNK = 64
LANES = 128
MOD_ROWS = 16
VMEM_LIMIT = 56 * 1024 * 1024

ATTN_HEAD_ORDER = (0, 2, 1, 3)
W_Q = 2 * GROUP_WIDTH
W_KV = 2 * KV_HEADS * HEAD_DIM
W_C = 896
W_D = 1152
W_ALL = W_Q + 2 * W_KV + W_C + W_D


def _cparams(sem):
    return pltpu.CompilerParams(dimension_semantics=sem, vmem_limit_bytes=VMEM_LIMIT)


def _dot(a, b):
    return jnp.dot(a, b, preferred_element_type=F32)


def _dot_nt(a, b):
    return lax.dot_general(a, b, (((1,), (1,)), ((), ())), preferred_element_type=F32)


def _dot_tn(a, b):
    return lax.dot_general(a, b, (((0,), (0,)), ((), ())), preferred_element_type=F32)


def _split3(x):
    hi = x.astype(BF16)
    r1 = x - hi.astype(F32)
    mid = r1.astype(BF16)
    lo = (r1 - mid.astype(F32)).astype(BF16)
    return hi, mid, lo


def _dot01(a01, x):
    hi, mid, lo = _split3(x)
    return _dot(a01, hi) + _dot(a01, mid) + _dot(a01, lo)


def _dot01_r(x, b01):
    hi, mid, lo = _split3(x)
    return _dot(hi, b01) + _dot(mid, b01) + _dot(lo, b01)


def _dot01_r2(x, b01):
    hi = x.astype(BF16)
    lo = (x - hi.astype(F32)).astype(BF16)
    return _dot(hi, b01) + _dot(lo, b01)


def _log_sigmoid(x):
    return jnp.minimum(x, 0.0) - jnp.log(1.0 + jnp.exp(-jnp.abs(x)))


def _sigmoid(x):
    return 1.0 / (1.0 + jnp.exp(-x))


def _iota(shape, dim):
    return lax.broadcasted_iota(jnp.int32, shape, dim)


def _group_ones(n, group):
    return (_iota((n, n), 0) // group == _iota((n, n), 1) // group).astype(BF16)


def _mod_kernel(c_ref, w_ref, b_ref, o_ref):
    c = c_ref[...]
    s = (c * _sigmoid(c)).astype(BF16)
    o_ref[0] = _dot(s, w_ref[0].astype(BF16)) + b_ref[0]


def _modulation(cc, w_mod, b_mod):
    depth, d, n = w_mod.shape
    tn = 1536
    return pl.pallas_call(
        _mod_kernel,
        out_shape=jax.ShapeDtypeStruct((depth, MOD_ROWS, n), F32),
        grid=(depth, n // tn),
        in_specs=[pl.BlockSpec((MOD_ROWS, d), lambda l, j: (0, 0)),
                  pl.BlockSpec((1, d, tn), lambda l, j: (l, 0, j)),
                  pl.BlockSpec((1, 1, tn), lambda l, j: (l, 0, j))],
        out_specs=pl.BlockSpec((1, MOD_ROWS, tn), lambda l, j: (l, 0, j)),
        compiler_params=_cparams(("arbitrary", "arbitrary")),
        name="modulation",
    )(cc, w_mod, b_mod.reshape(depth, 1, n))


def _qk_norm_rope(z, gain, cos, sin, ones_bd, scale):
    ss = _dot01_r2(z * z, ones_bd)
    y = z * lax.rsqrt(ss * (1.0 / HEAD_DIM) + EPS) * gain
    first_half = (_iota((z.shape[0], LANES), 1) % HEAD_DIM) < (HEAD_DIM // 2)
    outs = []
    for cb in range(z.shape[1] // LANES):
        yc = y[:, cb * LANES:(cb + 1) * LANES]
        partner = jnp.where(first_half, pltpu.roll(yc, LANES - HEAD_DIM // 2, 1),
                            pltpu.roll(yc, HEAD_DIM // 2, 1))
        outs.append((yc * cos + partner * sin) * scale)
    return jnp.concatenate(outs, axis=1)


def _proj_kernel(x_ref, mod_ref, g1_ref, w_ref, gains_ref, cos_ref, sin_ref, wg_ref, bg_ref, bm_ref,
                 q_ref, k_ref, v_ref, zc_ref, zd_ref):
    d = x_ref.shape[2]
    tm = x_ref.shape[1]
    x = x_ref[0]
    mod = mod_ref[0, 0]
    sh1 = mod[:, 0:d]
    sc1 = mod[:, d:2 * d]
    ms = jnp.mean(x * x, axis=-1, keepdims=True)
    h = (x * lax.rsqrt(ms + EPS)) * g1_ref[...]
    hb = (h * (1.0 + sc1) + sh1).astype(BF16)

    ones_bd = _group_ones(GROUP_WIDTH, HEAD_DIM)
    cos = cos_ref[...]
    sin = sin_ref[...]
    gains = gains_ref[...]
    gw = GROUP_WIDTH

    z = _dot(hb, w_ref[:, 0:W_Q + 2 * W_KV])
    q_scale = HEAD_DIM ** -0.5 * LOG2E
    for n in range(2):
        q = _qk_norm_rope(z[:, n * gw:(n + 1) * gw], gains[n:n + 1], cos, sin, ones_bd, q_scale)
        q_ref[0, :, n * gw:(n + 1) * gw] = q.astype(BF16)
    k_ref[0] = _qk_norm_rope(z[:, W_Q:W_Q + W_KV], gains[2:3], cos, sin, ones_bd, 1.0).astype(BF16)
    v_ref[0] = z[:, W_Q + W_KV:W_Q + 2 * W_KV].astype(BF16)

    base = W_Q + 2 * W_KV
    z = _dot(hb, w_ref[:, base:base + W_C])
    zc_ref[0, :, 0:GLA_KW] = z[:, 0:GLA_KW] * (GLA_DK ** -0.5)
    zc_ref[0, :, GLA_KW:768] = z[:, GLA_KW:768]
    pre = _dot(z[:, 768:896].astype(BF16), wg_ref[...]) + bg_ref[...]
    zc_ref[0, :, 768:1024] = _log_sigmoid(pre) * (1.0 / GLA_GATE_TAU)

    z = _dot(hb, w_ref[:, base + W_C:W_ALL])
    zd_ref[0, :, 0:gw] = z[:, 0:gw]
    zd_ref[0, :, gw:2 * gw] = z[:, gw:2 * gw] * (HEAD_DIM ** -0.5)
    zd_ref[0, :, 2 * gw:4 * gw] = z[:, 2 * gw:4 * gw]
    gates = z[:, 4 * gw:4 * gw + LANES] + bm_ref[...]
    is_input_gate = _iota((tm, LANES), 1) < 2 * GROUP_HEADS
    zd_ref[0, :, 4 * gw:4 * gw + LANES] = jnp.where(is_input_gate, gates, _log_sigmoid(gates))


def _projection(xt, modsel, g1, w_big, gains, cos_t, sin_t, wg, bg, bm, tm, ncb):
    b, t, d = xt.shape
    nblk = t // tm
    const = lambda bi, j: (0, 0)
    return pl.pallas_call(
        _proj_kernel,
        out_shape=(jax.ShapeDtypeStruct((b, t, W_Q), BF16),
                   jax.ShapeDtypeStruct((b, t, W_KV), BF16),
                   jax.ShapeDtypeStruct((b, t, W_KV), BF16),
                   jax.ShapeDtypeStruct((b, t, 1024), F32),
                   jax.ShapeDtypeStruct((b, t, W_D), F32)),
        grid=(b, nblk),
        in_specs=[pl.BlockSpec((1, tm, d), lambda bi, j: (bi, j, 0)),
                  pl.BlockSpec((1, 1, 1, N_MOD * d), lambda bi, j: (bi, jnp.where(j >= ncb, 1, 0), 0, 0)),
                  pl.BlockSpec((1, d), const),
                  pl.BlockSpec((d, W_ALL), const),
                  pl.BlockSpec((3, GROUP_WIDTH), const),
                  pl.BlockSpec((tm, LANES), lambda bi, j: (j, 0)),
                  pl.BlockSpec((tm, LANES), lambda bi, j: (j, 0)),
                  pl.BlockSpec((LANES, 2 * GLA_KW), const),
                  pl.BlockSpec((1, 2 * GLA_KW), const),
                  pl.BlockSpec((1, LANES), const)],
        out_specs=(pl.BlockSpec((1, tm, W_Q), lambda bi, j: (bi, j, 0)),
                   pl.BlockSpec((1, tm, W_KV), lambda bi, j: (bi, j, 0)),
                   pl.BlockSpec((1, tm, W_KV), lambda bi, j: (bi, j, 0)),
                   pl.BlockSpec((1, tm, 1024), lambda bi, j: (bi, j, 0)),
                   pl.BlockSpec((1, tm, W_D), lambda bi, j: (bi, j, 0))),
        compiler_params=_cparams(("arbitrary", "arbitrary")),
        name="projection",
    )(xt, modsel, g1, w_big, gains, cos_t, sin_t, wg, bg, bm)


def _stack_heads(q128):
    low = _iota(q128.shape, 1) < HEAD_DIM
    zero = jnp.zeros_like(q128)
    return jnp.concatenate([jnp.where(low, q128, zero), jnp.where(low, zero, q128)], axis=0)


def _with_ones(v):
    return jnp.concatenate([v, jnp.ones_like(v)], axis=1)


def _unstack_heads(acc, tq, extra_sum=None):
    low = _iota((tq, LANES), 1) < HEAD_DIM
    o = jnp.where(low, acc[:tq, :LANES], acc[tq:, :LANES])
    l = jnp.where(low, acc[:tq, LANES:], acc[tq:, LANES:])
    if extra_sum is not None:
        l = l + jnp.where(low, extra_sum[:tq], extra_sum[tq:])
    return o / l


def _attn_global_kernel(q_ref, k_ref, v_ref, o_ref, *, lc, tk, nk):
    tq = q_ref.shape[1]
    n_streams = GROUP_WIDTH // LANES
    qs = [_stack_heads(q_ref[0, :, n * LANES:(n + 1) * LANES]) for n in range(n_streams)]

    k = k_ref[0, 0:lc, :]
    v = _with_ones(v_ref[0, 0:lc, :])
    carry = []
    for q in qs:
        s = _dot_nt(q, k)
        m = jnp.max(s, axis=-1, keepdims=True)
        carry += [m, _dot(jnp.exp2(s - m).astype(BF16), v)]
    carry = tuple(carry)

    def latent_keys(carry):
        carry = list(carry)
        for c in range(nk):
            rows = slice(lc + c * tk, lc + (c + 1) * tk)
            k = k_ref[0, rows, :]
            v = _with_ones(v_ref[0, rows, :])
            for n, q in enumerate(qs):
                m, acc = carry[2 * n], carry[2 * n + 1]
                s = _dot_nt(q, k)
                m_new = jnp.maximum(m, jnp.max(s, axis=-1, keepdims=True))
                p = jnp.exp2(s - m_new).astype(BF16)
                carry[2 * n], carry[2 * n + 1] = m_new, jnp.exp2(m - m_new) * acc + _dot(p, v)
        return tuple(carry)

    def finish(carry):
        return jnp.concatenate([_unstack_heads(carry[2 * n + 1], tq) for n in range(n_streams)], axis=1)

    is_latent = pl.program_id(1) * tq >= lc
    o = lax.cond(is_latent, lambda: finish(latent_keys(carry)), lambda: finish(carry))
    o_ref[0] = o.astype(BF16)


def _attn_global(q, k, v, lc, tq, tk):
    b, t, _ = q.shape
    nk = (t - lc) // tk
    return pl.pallas_call(
        functools.partial(_attn_global_kernel, lc=lc, tk=tk, nk=nk),
        out_shape=jax.ShapeDtypeStruct((b, t, GROUP_WIDTH), BF16),
        grid=(b, t // tq),
        in_specs=[pl.BlockSpec((1, tq, GROUP_WIDTH), lambda bi, j: (bi, j, 0)),
                  pl.BlockSpec((1, t, LANES), lambda bi, j: (bi, 0, 0)),
                  pl.BlockSpec((1, t, LANES), lambda bi, j: (bi, 0, 0))],
        out_specs=pl.BlockSpec((1, tq, GROUP_WIDTH), lambda bi, j: (bi, j, 0)),
        compiler_params=_cparams(("arbitrary", "arbitrary")),
        name="attn_global",
    )(q, k, v)


def _attn_window_kernel(sink_ref, q_ref, kp_ref, kc_ref, kn_ref, vp_ref, vc_ref, vn_ref, kx_ref, vx_ref, o_ref,
                        *, nctx, nblk):
    tq = q_ref.shape[1]
    halo = Q_BLOCK
    j = pl.program_id(1)
    is_lat = j >= nctx
    off = 4 * tq
    lo_prev = jnp.where(j > nctx, -halo, off)
    lo_cur = jnp.where(is_lat, -halo, off)
    lo_next = jnp.where(jnp.logical_and(is_lat, j + 1 <= nblk - 1), -halo, off)
    iq = _iota((2 * tq, tq + 2 * halo), 0) % tq
    col = _iota((2 * tq, tq + 2 * halo), 1)
    diff = col - halo - iq
    lo = jnp.where(col < halo, lo_prev, jnp.where(col < halo + tq, lo_cur, lo_next))
    ok = jnp.logical_and(diff >= lo, diff <= halo)
    top = _iota((2 * tq, 1), 0) < tq
    kw = jnp.concatenate([kp_ref[0], kc_ref[0], kn_ref[0]], axis=0)
    vw = _with_ones(jnp.concatenate([vp_ref[0], vc_ref[0], vn_ref[0]], axis=0))
    kx = kx_ref[0]
    vx = _with_ones(vx_ref[0])
    outs = []
    for n in range(GROUP_WIDTH // LANES):
        qs = _stack_heads(q_ref[0, :, n * LANES:(n + 1) * LANES])
        s_w = jnp.where(ok, _dot_nt(qs, kw), NEG)
        s_x = _dot_nt(qs, kx)
        sink = jnp.where(top, sink_ref[n], sink_ref[n + KV_HEADS]) * LOG2E
        m = jnp.maximum(jnp.maximum(jnp.max(s_w, axis=-1, keepdims=True),
                                    jnp.max(s_x, axis=-1, keepdims=True)), sink)
        acc = _dot(jnp.exp2(s_w - m).astype(BF16), vw) + _dot(jnp.exp2(s_x - m).astype(BF16), vx)
        outs.append(_unstack_heads(acc, tq, extra_sum=jnp.exp2(sink - m)))
    o_ref[0] = jnp.concatenate(outs, axis=1).astype(BF16)


def _attn_window(q, k, v, sink, lc, tq):
    b, t, _ = q.shape
    nblk = t // tq
    nctx = lc // tq
    per = tq // Q_BLOCK
    cur = pl.BlockSpec((1, tq, LANES), lambda bi, j: (bi, j, 1))
    prev = pl.BlockSpec((1, Q_BLOCK, LANES), lambda bi, j: (bi, jnp.maximum(j * per - 1, 0), 1))
    nxt = pl.BlockSpec((1, Q_BLOCK, LANES), lambda bi, j: (bi, jnp.minimum((j + 1) * per, nblk * per - 1), 1))
    ctx = pl.BlockSpec((1, lc, LANES), lambda bi, j: (bi, 0, 1))
    return pl.pallas_call(
        functools.partial(_attn_window_kernel, nctx=nctx, nblk=nblk),
        out_shape=jax.ShapeDtypeStruct((b, t, GROUP_WIDTH), BF16),
        grid=(b, nblk),
        in_specs=[pl.BlockSpec(memory_space=pltpu.SMEM),
                  pl.BlockSpec((1, tq, GROUP_WIDTH), lambda bi, j: (bi, j, 1)),
                  prev, cur, nxt,
                  prev, cur, nxt,
                  ctx, ctx],
        out_specs=pl.BlockSpec((1, tq, GROUP_WIDTH), lambda bi, j: (bi, j, 0)),
        compiler_params=_cparams(("arbitrary", "arbitrary")),
        name="attn_window",
    )(sink, q, k, k, k, v, v, v, k, v)


def _chunk_tri(rows, reverse):
    t = _iota((rows, rows), 0)
    u = _iota((rows, rows), 1)
    same = t // CHUNK == u // CHUNK
    return jnp.logical_and(same, (u >= t) if reverse else (u <= t)).astype(BF16)


def _tile4(x):
    return jnp.concatenate([x, x, x, x], axis=0)


def _rev_block(i, ncb, nblk):
    return jnp.where(i < ncb, ncb - 1 - i, nblk - 1 - (i - ncb))


def _gla_direction(qkv_ref, la_ref, o_ref, s_ref, reverse):
    rows = qkv_ref.shape[1]
    nchunk = rows // CHUNK
    b = _dot01(_chunk_tri(rows, reverse), la_ref[0])
    q = qkv_ref[0, :, 0:GLA_KW]
    k = qkv_ref[0, :, GLA_KW:2 * GLA_KW]
    v = qkv_ref[0, :, 2 * GLA_KW:2 * GLA_KW + GROUP_WIDTH].astype(BF16)
    same_kd = _iota((GROUP_WIDTH, GLA_KW), 0) // CHUNK == _iota((GROUP_WIDTH, GLA_KW), 1) // GLA_DK
    same_kv = _iota((GROUP_WIDTH, GROUP_WIDTH), 0) // CHUNK == _iota((GROUP_WIDTH, GROUP_WIDTH), 1) // HEAD_DIM
    t_pos = _iota((CHUNK, GROUP_WIDTH), 0)
    s_pos = _iota((CHUNK, GROUP_WIDTH), 1) % CHUNK
    causal = (s_pos >= t_pos) if reverse else (s_pos <= t_pos)
    end = 0 if reverse else CHUNK - 1
    state = s_ref[...]
    order = range(nchunk - 1, -1, -1) if reverse else range(nchunk)
    for c in order:
        sl = slice(c * CHUNK, (c + 1) * CHUNK)
        bc = b[sl]
        b_end = bc[end:end + 1]
        b_mid = bc[CHUNK // 2:CHUNK // 2 + 1]
        qc, kc, vc = q[sl], k[sl], v[sl]
        q_inter = (qc * jnp.exp(bc)).astype(BF16)
        q_intra = (qc * jnp.exp(bc - b_mid)).astype(BF16)
        k_intra = (kc * jnp.exp(b_mid - bc)).astype(BF16)
        k_state = (kc * jnp.exp(b_end - bc)).astype(BF16)
        k_bd = jnp.where(same_kd, _tile4(k_intra), jnp.zeros((), BF16))
        v_bd = jnp.where(same_kv, _tile4(vc), jnp.zeros((), BF16))
        scores = jnp.where(causal, _dot_nt(q_intra, k_bd), 0.0).astype(BF16)
        o_ref[0, sl, :] = _dot(scores, v_bd) + _dot_nt(q_inter, state.astype(BF16))
        state = state * jnp.exp(b_end) + jnp.where(same_kd, _dot_tn(vc, k_state), 0.0)
    s_ref[...] = state


def _gla_kernel(qkv_f_ref, la_f_ref, qkv_b_ref, la_b_ref, of_ref, ob_ref, sf_ref, sb_ref):
    @pl.when(pl.program_id(1) == 0)
    def _():
        sf_ref[...] = jnp.zeros_like(sf_ref)
        sb_ref[...] = jnp.zeros_like(sb_ref)

    _gla_direction(qkv_f_ref, la_f_ref, of_ref, sf_ref, False)
    _gla_direction(qkv_b_ref, la_b_ref, ob_ref, sb_ref, True)


def _gla_scan(zc, lc, rows):
    b, t, _ = zc.shape
    nblk = t // rows
    ncb = lc // rows
    fwd = lambda col: (lambda bi, i: (bi, i, col))
    rev = lambda col: (lambda bi, i: (bi, _rev_block(i, ncb, nblk), col))
    qkv_w = 2 * GLA_KW + GROUP_WIDTH
    return pl.pallas_call(
        _gla_kernel,
        out_shape=(jax.ShapeDtypeStruct((b, t, GROUP_WIDTH), F32),
                   jax.ShapeDtypeStruct((b, t, GROUP_WIDTH), F32)),
        grid=(b, nblk),
        in_specs=[pl.BlockSpec((1, rows, qkv_w), fwd(0)),
                  pl.BlockSpec((1, rows, GLA_KW), fwd(6)),
                  pl.BlockSpec((1, rows, qkv_w), rev(0)),
                  pl.BlockSpec((1, rows, GLA_KW), rev(7))],
        out_specs=(pl.BlockSpec((1, rows, GROUP_WIDTH), fwd(0)),
                   pl.BlockSpec((1, rows, GROUP_WIDTH), rev(0))),
        scratch_shapes=[pltpu.VMEM((GROUP_WIDTH, GLA_KW), F32),
                        pltpu.VMEM((GROUP_WIDTH, GLA_KW), F32)],
        compiler_params=_cparams(("arbitrary", "arbitrary")),
        name="gla_scan",
    )(zc, zc, zc, zc)


def _expand_heads(g, base, rows):
    hid = _iota((rows, GROUP_WIDTH), 1) // HEAD_DIM
    cols = [jnp.broadcast_to(g[:, base + h:base + h + 1], (rows, GROUP_WIDTH)) for h in range(GROUP_HEADS)]
    return jnp.where(hid == 0, cols[0], jnp.where(hid == 1, cols[1], jnp.where(hid == 2, cols[2], cols[3])))


def _mlstm_direction(qkv_ref, gate_ref, o_ref, c_ref, n_ref, m_ref, direction):
    reverse = direction == 1
    rows = qkv_ref.shape[1]
    nchunk = rows // CHUNK
    gw = GROUP_WIDTH
    g = gate_ref[0]
    ig = _expand_heads(g, GROUP_HEADS * direction, rows)
    lf = _expand_heads(g, 2 * GROUP_HEADS + GROUP_HEADS * direction, rows)
    f_cum = _dot01(_chunk_tri(rows, reverse), lf)
    q = qkv_ref[0, :, 0:gw]
    k = qkv_ref[0, :, gw:2 * gw]
    v = qkv_ref[0, :, 2 * gw:3 * gw].astype(BF16)

    same_head = _iota((gw, gw), 0) // HEAD_DIM == _iota((gw, gw), 1) // HEAD_DIM
    ones_bd = same_head.astype(BF16)
    hid = _iota((CHUNK, gw), 1) // HEAD_DIM
    t_pos = _iota((CHUNK, gw), 0)
    s_pos = _iota((CHUNK, gw), 1) % CHUNK
    causal = (s_pos >= t_pos) if reverse else (s_pos <= t_pos)
    diag = (s_pos == t_pos).astype(F32)
    ones_cc = jnp.ones((CHUNK, CHUNK), BF16)
    end = 0 if reverse else CHUNK - 1

    c_state = c_ref[...]
    n_state = n_ref[...]
    m_state = m_ref[...]
    order = range(nchunk - 1, -1, -1) if reverse else range(nchunk)
    for c in order:
        sl = slice(c * CHUNK, (c + 1) * CHUNK)
        fc, ic = f_cum[sl], ig[sl]
        qc, kc, vc = q[sl], k[sl], v[sl]
        qb = qc.astype(BF16)
        key_term = _dot01(ones_cc, (fc - ic) * diag)
        logw = jnp.where(causal, fc - key_term, NEG)
        log_inter = fc + m_state
        row_max = jnp.full((CHUNK, gw), NEG, F32)
        for h in range(GROUP_HEADS):
            mh = jnp.max(jnp.where(hid == h, logw, NEG), axis=-1, keepdims=True)
            row_max = jnp.where(hid == h, mh, row_max)
        m_t = jnp.maximum(log_inter, row_max)
        w = jnp.exp(logw - m_t)
        w_inter = jnp.exp(log_inter - m_t)
        k_bd = jnp.where(same_head, _tile4(kc.astype(BF16)), jnp.zeros((), BF16))
        v_bd = jnp.where(same_head, _tile4(vc), jnp.zeros((), BF16))
        qk = _dot_nt(qb, k_bd) * w
        num = _dot(qk.astype(BF16), v_bd) + w_inter * _dot(qb, c_state.astype(BF16))
        den = _dot01_r(qk, ones_bd) + w_inter * _dot((qc * n_state).astype(BF16), ones_bd)
        o_ref[0, sl, :] = num / jnp.maximum(jnp.abs(den), jnp.exp(-m_t))
        m_new = m_t[end:end + 1]
        decay = w_inter[end:end + 1]
        k_end = kc * jnp.exp(fc[end:end + 1] - fc + ic - m_new)
        c_state = c_state * decay + jnp.where(same_head, _dot_tn(k_end.astype(BF16), vc), 0.0)
        n_state = n_state * decay + jnp.sum(k_end, axis=0, keepdims=True)
        m_state = m_new
    c_ref[...] = c_state
    n_ref[...] = n_state
    m_ref[...] = m_state


def _mlstm_kernel(qkv_f_ref, gate_f_ref, qkv_b_ref, gate_b_ref, of_ref, ob_ref,
                  cf_ref, nf_ref, mf_ref, cb_ref, nb_ref, mb_ref):
    @pl.when(pl.program_id(1) == 0)
    def _():
        for r in (cf_ref, nf_ref, mf_ref, cb_ref, nb_ref, mb_ref):
            r[...] = jnp.zeros_like(r)

    _mlstm_direction(qkv_f_ref, gate_f_ref, of_ref, cf_ref, nf_ref, mf_ref, 0)
    _mlstm_direction(qkv_b_ref, gate_b_ref, ob_ref, cb_ref, nb_ref, mb_ref, 1)


def _mlstm_scan(zd, lc, rows):
    b, t, _ = zd.shape
    nblk = t // rows
    ncb = lc // rows
    gw = GROUP_WIDTH
    fwd = lambda col: (lambda bi, i: (bi, i, col))
    rev = lambda col: (lambda bi, i: (bi, _rev_block(i, ncb, nblk), col))
    gate_col = 4 * gw // LANES
    state = [pltpu.VMEM((gw, gw), F32), pltpu.VMEM((1, gw), F32), pltpu.VMEM((1, gw), F32)]
    return pl.pallas_call(
        _mlstm_kernel,
        out_shape=(jax.ShapeDtypeStruct((b, t, gw), F32),
                   jax.ShapeDtypeStruct((b, t, gw), F32)),
        grid=(b, nblk),
        in_specs=[pl.BlockSpec((1, rows, 3 * gw), fwd(0)),
                  pl.BlockSpec((1, rows, LANES), fwd(gate_col)),
                  pl.BlockSpec((1, rows, 3 * gw), rev(0)),
                  pl.BlockSpec((1, rows, LANES), rev(gate_col))],
        out_specs=(pl.BlockSpec((1, rows, gw), fwd(0)),
                   pl.BlockSpec((1, rows, gw), rev(0))),
        scratch_shapes=state + state,
        compiler_params=_cparams(("arbitrary", "arbitrary")),
        name="mlstm_scan",
    )(zd, zd, zd, zd)


def _head_rms(o, gain, ones_bd):
    ss = _dot01_r(o * o, ones_bd)
    return o * lax.rsqrt(ss * (1.0 / HEAD_DIM) + EPS) * gain


def _out_kernel(x_ref, mod_ref, oa_ref, ob_ref, gf_ref, gb_ref, r_ref, mf_ref, mb_ref, og_ref,
                gains_ref, g2_ref, wo_ref, w1_ref, w2_ref, y_ref):
    d = x_ref.shape[2]
    x = x_ref[0]
    mod = mod_ref[0, 0]
    ga1 = mod[:, 2 * d:3 * d]
    sh2 = mod[:, 3 * d:4 * d]
    sc2 = mod[:, 4 * d:5 * d]
    ga2 = mod[:, 5 * d:6 * d]
    ones_bd = _group_ones(GROUP_WIDTH, HEAD_DIM)
    gains = gains_ref[...]

    r = r_ref[0]
    o_c = _head_rms(gf_ref[0] + gb_ref[0], gains[0:1], ones_bd) * (r * _sigmoid(r))
    o_d = _sigmoid(og_ref[0]) * _head_rms(mf_ref[0] + mb_ref[0], gains[1:2], ones_bd)
    o_cat = jnp.concatenate([oa_ref[0], ob_ref[0], o_c.astype(BF16), o_d.astype(BF16)], axis=1)
    x = x + ga1 * _dot(o_cat, wo_ref[...])

    ms = jnp.mean(x * x, axis=-1, keepdims=True)
    h = (x * lax.rsqrt(ms + EPS)) * g2_ref[...]
    hb = (h * (1.0 + sc2) + sh2).astype(BF16)
    hid = jnp.maximum(_dot(hb, w1_ref[...]), 0.0)
    hid = (hid * hid).astype(BF16)
    y_ref[0] = x + ga2 * _dot(hid, w2_ref[...])


def _out_mlp(xt, modsel, oa, ob, gf, gb, zc, mf, mb, zd, gains, g2, wo, w1, w2, tm, ncb, skip_blocks):
    b, t, d = xt.shape
    nblk = t // tm - skip_blocks
    gw = GROUP_WIDTH
    row = lambda col: (lambda bi, j: (bi, j + skip_blocks, col))
    const = lambda bi, j: (0, 0)
    resident = lambda shape: pl.BlockSpec(shape, const, pipeline_mode=pl.Buffered(1))
    return pl.pallas_call(
        _out_kernel,
        out_shape=jax.ShapeDtypeStruct((b, nblk * tm, d), F32),
        grid=(b, nblk),
        in_specs=[pl.BlockSpec((1, tm, d), row(0)),
                  pl.BlockSpec((1, 1, 1, N_MOD * d),
                               lambda bi, j: (bi, jnp.where(j + skip_blocks >= ncb, 1, 0), 0, 0)),
                  pl.BlockSpec((1, tm, gw), row(0)),
                  pl.BlockSpec((1, tm, gw), row(0)),
                  pl.BlockSpec((1, tm, gw), row(0)),
                  pl.BlockSpec((1, tm, gw), row(0)),
                  pl.BlockSpec((1, tm, gw), row(2)),
                  pl.BlockSpec((1, tm, gw), row(0)),
                  pl.BlockSpec((1, tm, gw), row(0)),
                  pl.BlockSpec((1, tm, gw), row(3)),
                  pl.BlockSpec((2, gw), const),
                  pl.BlockSpec((1, d), const),
                  resident((d, d)),
                  resident((d, 4 * d)),
                  resident((4 * d, d))],
        out_specs=pl.BlockSpec((1, tm, d), lambda bi, j: (bi, j, 0)),
        compiler_params=_cparams(("arbitrary", "arbitrary")),
        name="out_mlp",
    )(xt, modsel, oa, ob, gf, gb, zc, mf, mb, zd, gains, g2, wo, w1, w2)


def _projection_columns():
    gw, kw = GROUP_WIDTH, KV_HEADS * HEAD_DIM
    splits = (gw, kw, kw, gw, kw, kw, GLA_KW, GLA_KW, gw, gw, 2 * GLA_GATE_RANK,
              gw, gw, gw, gw, 2 * GROUP_HEADS, 2 * GROUP_HEADS)
    offs = np.concatenate([[0], np.cumsum(splits)])
    in_width = int(offs[-1])
    half_split = np.concatenate([np.arange(0, HEAD_DIM, 2), np.arange(1, HEAD_DIM, 2)])
    plain = np.arange(HEAD_DIM)
    cols = []
    mixers = (0, 3)
    for base in mixers:
        cols += [offs[base] + h * HEAD_DIM + half_split for h in ATTN_HEAD_ORDER]
    for base in mixers:
        cols += [offs[base + 1] + g * HEAD_DIM + half_split for g in range(KV_HEADS)]
    for base in mixers:
        cols += [offs[base + 2] + g * HEAD_DIM + plain for g in range(KV_HEADS)]
    cols.append(np.arange(offs[6], offs[11]))
    cols.append(np.full(W_C - (offs[11] - offs[6]), in_width))
    cols.append(np.arange(offs[11], offs[17]))
    cols.append(np.full(W_D - (offs[17] - offs[11]), in_width))
    cols = np.concatenate(cols).astype(np.int32)
    assert cols.shape[0] == W_ALL
    return cols, in_width, half_split


def _rope_tables(seq, lc):
    rows = seq // GRID_W
    row = jnp.repeat(jnp.arange(rows, dtype=F32), GRID_W)
    col = jnp.tile(jnp.arange(GRID_W, dtype=F32), rows)
    n_freq = HEAD_DIM // 4
    inv = ROPE_THETA ** (-jnp.arange(n_freq, dtype=F32) / n_freq)
    ang = jnp.concatenate([row[:, None] * inv, col[:, None] * inv], axis=-1)
    cos, sin = jnp.cos(ang), jnp.sin(ang)
    cos_t = jnp.tile(cos, (1, 4))
    sin_t = jnp.tile(jnp.concatenate([-sin, sin], axis=-1), (1, 2))
    cos_t = jnp.concatenate([jnp.ones((lc, LANES), F32), cos_t], axis=0)
    sin_t = jnp.concatenate([jnp.zeros((lc, LANES), F32), sin_t], axis=0)
    return cos_t, sin_t


def kernel(x, c, ctx, c_ctx, w_mod, b_mod, g_norm1, g_norm2, w_in, g_q_a, g_k_a, g_q_b, g_k_b, sink_b,
           w_gla_gate, b_gla_gate, g_gla_out, b_mlstm_i, b_mlstm_f, g_mlstm_out, w_out, w_mlp1, w_mlp2):
    b, seq, d = x.shape
    lc = ctx.shape[1]
    depth = w_mod.shape[0]
    tm = 256
    assert d == 4 * GROUP_WIDTH and lc % tm == 0 and seq % tm == 0 and seq % GRID_W == 0 and b + 1 <= MOD_ROWS

    cols, in_width, half_split = _projection_columns()
    cos_t, sin_t = _rope_tables(seq, lc)
    attn_rows = np.concatenate([h * HEAD_DIM + np.arange(HEAD_DIM) for h in ATTN_HEAD_ORDER])
    out_rows = np.concatenate([attn_rows, GROUP_WIDTH + attn_rows, np.arange(2 * GROUP_WIDTH, d)])

    cc = jnp.zeros((MOD_ROWS, d), F32).at[0:b].set(c).at[b].set(c_ctx)
    mod_all = _modulation(cc, w_mod, b_mod)

    xt = jnp.concatenate([ctx, x], axis=1)
    for l in range(depth):
        modsel = jnp.stack([jnp.broadcast_to(mod_all[l, b], (b, N_MOD * d)), mod_all[l, 0:b]], axis=1)
        modsel = modsel.reshape(b, 2, 1, N_MOD * d)

        w_ext = jnp.concatenate([w_in[l], jnp.zeros((d, 1), F32)], axis=1)
        w_big = jnp.take(w_ext, cols, axis=1).astype(BF16)
        tiled = lambda g, n: jnp.tile(g[half_split], n)
        gains_qk = jnp.stack([tiled(g_q_a[l], GROUP_HEADS), tiled(g_q_b[l], GROUP_HEADS),
                              jnp.concatenate([tiled(g_k_a[l], KV_HEADS), tiled(g_k_b[l], KV_HEADS)])])
        wg = jnp.zeros((LANES, 2 * GLA_KW), F32)
        wg = wg.at[0:GLA_GATE_RANK, 0:GLA_KW].set(w_gla_gate[l, 0])
        wg = wg.at[GLA_GATE_RANK:2 * GLA_GATE_RANK, GLA_KW:].set(w_gla_gate[l, 1]).astype(BF16)
        bg = b_gla_gate[l].reshape(1, 2 * GLA_KW)
        bm = jnp.zeros((1, LANES), F32)
        bm = bm.at[0, 0:2 * GROUP_HEADS].set(b_mlstm_i[l].reshape(-1))
        bm = bm.at[0, 2 * GROUP_HEADS:4 * GROUP_HEADS].set(b_mlstm_f[l].reshape(-1))

        q, k, v, zc, zd = _projection(xt, modsel, g_norm1[l].reshape(1, d), w_big, gains_qk,
                                      cos_t, sin_t, wg, bg, bm, tm, lc // tm)
        oa = _attn_global(q, k, v, lc, tq=256, tk=512)
        ob = _attn_window(q, k, v, sink_b[l], lc, tq=256)
        gf, gb = _gla_scan(zc, lc, tm)
        mf, mb = _mlstm_scan(zd, lc, tm)

        gains_out = jnp.stack([jnp.tile(g_gla_out[l], 4), jnp.tile(g_mlstm_out[l], 4)])
        last = l == depth - 1
        xt = _out_mlp(xt, modsel, oa, ob, gf, gb, zc, mf, mb, zd, gains_out, g_norm2[l].reshape(1, d),
                      w_out[l][out_rows].astype(BF16), w_mlp1[l].astype(BF16), w_mlp2[l].astype(BF16), tm,
                      ncb=lc // tm, skip_blocks=lc // tm if last else 0)
    return xt
```

```python
import functools

import numpy as np
import jax
import jax.numpy as jnp
from jax import lax
from jax.experimental import pallas as pl
from jax.experimental.pallas import tpu as pltpu

F32 = jnp.float32
BF16 = jnp.bfloat16

HEAD_DIM = 64
GROUP_HEADS = 4
GROUP_WIDTH = GROUP_HEADS * HEAD_DIM
KV_HEADS = 2
GRID_W = 64
Q_BLOCK = 128
ROPE_THETA = 10000.0
GLA_DK = 32
GLA_KW = GROUP_HEADS * GLA_DK
GLA_GATE_RANK = 16
GLA_GATE_TAU = 16.0
N_MOD = 6
EPS = 1e-6
LOG2E = 1.4426950408889634
NEG = -1e30
CHUNK = 64
LANES = 128
MOD_ROWS = 16
VMEM_LIMIT = 56 * 1024 * 1024

ATTN_HEAD_ORDER = (0, 2, 1, 3)
W_Q = 2 * GROUP_WIDTH
W_KV = 2 * KV_HEADS * HEAD_DIM
W_C = 896
W_D = 1152
W_ALL = W_Q + 2 * W_KV + W_C + W_D


def _cparams(sem):
    return pltpu.CompilerParams(dimension_semantics=sem, vmem_limit_bytes=VMEM_LIMIT)


def _dot(a, b):
    return jnp.dot(a, b, preferred_element_type=F32)


def _dot_nt(a, b):
    return lax.dot_general(a, b, (((1,), (1,)), ((), ())), preferred_element_type=F32)


def _dot_tn(a, b):
    return lax.dot_general(a, b, (((0,), (0,)), ((), ())), preferred_element_type=F32)


def _split3(x):
    hi = x.astype(BF16)
    r1 = x - hi.astype(F32)
    mid = r1.astype(BF16)
    lo = (r1 - mid.astype(F32)).astype(BF16)
    return hi, mid, lo


def _dot01(a01, x):
    hi, mid, lo = _split3(x)
    return _dot(a01, hi) + _dot(a01, mid) + _dot(a01, lo)


def _dot01_r(x, b01):
    hi, mid, lo = _split3(x)
    return _dot(hi, b01) + _dot(mid, b01) + _dot(lo, b01)


def _dot01_r2(x, b01):
    hi = x.astype(BF16)
    lo = (x - hi.astype(F32)).astype(BF16)
    return _dot(hi, b01) + _dot(lo, b01)


def _log_sigmoid(x):
    return jnp.minimum(x, 0.0) - jnp.log(1.0 + jnp.exp(-jnp.abs(x)))


def _sigmoid(x):
    return 1.0 / (1.0 + jnp.exp(-x))


def _iota(shape, dim):
    return lax.broadcasted_iota(jnp.int32, shape, dim)


def _group_ones(n, group):
    return (_iota((n, n), 0) // group == _iota((n, n), 1) // group).astype(BF16)


def _mod_kernel(c_ref, w_ref, b_ref, o_ref):
    c = c_ref[...]
    s = (c * _sigmoid(c)).astype(BF16)
    o_ref[0] = _dot(s, w_ref[0].astype(BF16)) + b_ref[0]


def _modulation(cc, w_mod, b_mod):
    depth, d, n = w_mod.shape
    tn = 1536
    return pl.pallas_call(
        _mod_kernel,
        out_shape=jax.ShapeDtypeStruct((depth, MOD_ROWS, n), F32),
        grid=(depth, n // tn),
        in_specs=[pl.BlockSpec((MOD_ROWS, d), lambda l, j: (0, 0)),
                  pl.BlockSpec((1, d, tn), lambda l, j: (l, 0, j)),
                  pl.BlockSpec((1, 1, tn), lambda l, j: (l, 0, j))],
        out_specs=pl.BlockSpec((1, MOD_ROWS, tn), lambda l, j: (l, 0, j)),
        compiler_params=_cparams(("arbitrary", "arbitrary")),
        name="modulation",
    )(cc, w_mod, b_mod.reshape(depth, 1, n))


def _qk_norm_rope(z, gain, cos, sin, ones_bd, scale):
    ss = _dot01_r2(z * z, ones_bd)
    y = z * lax.rsqrt(ss * (1.0 / HEAD_DIM) + EPS) * gain
    first_half = (_iota((z.shape[0], LANES), 1) % HEAD_DIM) < (HEAD_DIM // 2)
    outs = []
    for cb in range(z.shape[1] // LANES):
        yc = y[:, cb * LANES:(cb + 1) * LANES]
        partner = jnp.where(first_half, pltpu.roll(yc, LANES - HEAD_DIM // 2, 1),
                            pltpu.roll(yc, HEAD_DIM // 2, 1))
        outs.append((yc * cos + partner * sin) * scale)
    return jnp.concatenate(outs, axis=1)


def _residual_specs(xs, tm, ncb, skip_blocks=0):
    d = xs[0].shape[2]
    if len(xs) == 1:
        return [pl.BlockSpec((1, tm, d), lambda bi, j: (bi, j + skip_blocks, 0))]
    return [pl.BlockSpec((1, tm, d), lambda bi, j: (bi, jnp.minimum(j + skip_blocks, ncb - 1), 0)),
            pl.BlockSpec((1, tm, d), lambda bi, j: (bi, jnp.maximum(j + skip_blocks - ncb, 0), 0))]


def _residual_rows(x_refs, ncb, skip_blocks=0):
    if len(x_refs) == 1:
        return x_refs[0][0]
    return jnp.where(pl.program_id(1) + skip_blocks < ncb, x_refs[0][0], x_refs[1][0])


def _proj_kernel(*refs, n_x, ncb):
    x_refs = refs[:n_x]
    (mod_ref, g1_ref, w_ref, gains_ref, cos_ref, sin_ref, wg_ref, bg_ref, bm_ref,
     q_ref, k_ref, v_ref, zc_ref, zd_ref) = refs[n_x:]
    d = x_refs[0].shape[2]
    tm = x_refs[0].shape[1]
    x = _residual_rows(x_refs, ncb)
    mod = mod_ref[0, 0]
    sh1 = mod[:, 0:d]
    sc1 = mod[:, d:2 * d]
    ms = jnp.mean(x * x, axis=-1, keepdims=True)
    h = (x * lax.rsqrt(ms + EPS)) * g1_ref[...]
    hb = (h * (1.0 + sc1) + sh1).astype(BF16)

    ones_bd = _group_ones(GROUP_WIDTH, HEAD_DIM)
    cos = cos_ref[...]
    sin = sin_ref[...]
    gains = gains_ref[...]
    gw = GROUP_WIDTH

    z = _dot(hb, w_ref[:, 0:W_Q + 2 * W_KV])
    q_scale = HEAD_DIM ** -0.5 * LOG2E
    for n in range(2):
        q = _qk_norm_rope(z[:, n * gw:(n + 1) * gw], gains[n:n + 1], cos, sin, ones_bd, q_scale)
        q_ref[0, :, n * gw:(n + 1) * gw] = q.astype(BF16)
    k_ref[0] = _qk_norm_rope(z[:, W_Q:W_Q + W_KV], gains[2:3], cos, sin, ones_bd, 1.0).astype(BF16)
    v_ref[0] = z[:, W_Q + W_KV:W_Q + 2 * W_KV].astype(BF16)

    base = W_Q + 2 * W_KV
    z = _dot(hb, w_ref[:, base:base + W_C])
    zc_ref[0, :, 0:GLA_KW] = z[:, 0:GLA_KW] * (GLA_DK ** -0.5)
    zc_ref[0, :, GLA_KW:768] = z[:, GLA_KW:768]
    pre = _dot(z[:, 768:896].astype(BF16), wg_ref[...]) + bg_ref[...]
    zc_ref[0, :, 768:1024] = _log_sigmoid(pre) * (1.0 / GLA_GATE_TAU)

    z = _dot(hb, w_ref[:, base + W_C:W_ALL])
    zd_ref[0, :, 0:gw] = z[:, 0:gw]
    zd_ref[0, :, gw:2 * gw] = z[:, gw:2 * gw] * (HEAD_DIM ** -0.5)
    zd_ref[0, :, 2 * gw:4 * gw] = z[:, 2 * gw:4 * gw]
    gates = z[:, 4 * gw:4 * gw + LANES] + bm_ref[...]
    is_input_gate = _iota((tm, LANES), 1) < 2 * GROUP_HEADS
    zd_ref[0, :, 4 * gw:4 * gw + LANES] = jnp.where(is_input_gate, gates, _log_sigmoid(gates))


def _projection(xs, modsel, g1, w_big, gains, cos_t, sin_t, wg, bg, bm, tm, ncb):
    b, _, d = xs[0].shape
    t = sum(a.shape[1] for a in xs)
    nblk = t // tm
    const = lambda bi, j: (0, 0)
    return pl.pallas_call(
        functools.partial(_proj_kernel, n_x=len(xs), ncb=ncb),
        out_shape=(jax.ShapeDtypeStruct((b, t, W_Q), BF16),
                   jax.ShapeDtypeStruct((b, t, W_KV), BF16),
                   jax.ShapeDtypeStruct((b, t, W_KV), BF16),
                   jax.ShapeDtypeStruct((b, t, 1024), F32),
                   jax.ShapeDtypeStruct((b, t, W_D), F32)),
        grid=(b, nblk),
        in_specs=_residual_specs(xs, tm, ncb) + [
                  pl.BlockSpec((1, 1, 1, N_MOD * d), lambda bi, j: (bi, jnp.where(j >= ncb, 1, 0), 0, 0)),
                  pl.BlockSpec((1, d), const),
                  pl.BlockSpec((d, W_ALL), const),
                  pl.BlockSpec((3, GROUP_WIDTH), const),
                  pl.BlockSpec((tm, LANES), lambda bi, j: (j, 0)),
                  pl.BlockSpec((tm, LANES), lambda bi, j: (j, 0)),
                  pl.BlockSpec((LANES, 2 * GLA_KW), const),
                  pl.BlockSpec((1, 2 * GLA_KW), const),
                  pl.BlockSpec((1, LANES), const)],
        out_specs=(pl.BlockSpec((1, tm, W_Q), lambda bi, j: (bi, j, 0)),
                   pl.BlockSpec((1, tm, W_KV), lambda bi, j: (bi, j, 0)),
                   pl.BlockSpec((1, tm, W_KV), lambda bi, j: (bi, j, 0)),
                   pl.BlockSpec((1, tm, 1024), lambda bi, j: (bi, j, 0)),
                   pl.BlockSpec((1, tm, W_D), lambda bi, j: (bi, j, 0))),
        compiler_params=_cparams(("arbitrary", "arbitrary")),
        name="projection",
    )(*xs, modsel, g1, w_big, gains, cos_t, sin_t, wg, bg, bm)


def _stack_heads(q128):
    low = _iota(q128.shape, 1) < HEAD_DIM
    zero = jnp.zeros_like(q128)
    return jnp.concatenate([jnp.where(low, q128, zero), jnp.where(low, zero, q128)], axis=0)


def _with_ones(v):
    return jnp.concatenate([v, jnp.ones_like(v)], axis=1)


def _unstack_heads(acc, tq, extra_sum=None):
    low = _iota((tq, LANES), 1) < HEAD_DIM
    o = jnp.where(low, acc[:tq, :LANES], acc[tq:, :LANES])
    l = jnp.where(low, acc[:tq, LANES:], acc[tq:, LANES:])
    if extra_sum is not None:
        l = l + jnp.where(low, extra_sum[:tq], extra_sum[tq:])
    return o / l


def _attn_global_kernel(q_ref, k_ref, v_ref, o_ref, *, lc, tk, nk):
    tq = q_ref.shape[1]
    n_streams = GROUP_WIDTH // LANES
    qs = [_stack_heads(q_ref[0, :, n * LANES:(n + 1) * LANES]) for n in range(n_streams)]

    k = k_ref[0, 0:lc, :]
    v = _with_ones(v_ref[0, 0:lc, :])
    carry = []
    for q in qs:
        s = _dot_nt(q, k)
        m = jnp.max(s, axis=-1, keepdims=True)
        carry += [m, _dot(jnp.exp2(s - m).astype(BF16), v)]
    carry = tuple(carry)

    def latent_keys(carry):
        carry = list(carry)
        for c in range(nk):
            rows = slice(lc + c * tk, lc + (c + 1) * tk)
            k = k_ref[0, rows, :]
            v = _with_ones(v_ref[0, rows, :])
            for n, q in enumerate(qs):
                m, acc = carry[2 * n], carry[2 * n + 1]
                s = _dot_nt(q, k)
                m_new = jnp.maximum(m, jnp.max(s, axis=-1, keepdims=True))
                p = jnp.exp2(s - m_new).astype(BF16)
                carry[2 * n], carry[2 * n + 1] = m_new, jnp.exp2(m - m_new) * acc + _dot(p, v)
        return tuple(carry)

    def finish(carry):
        return jnp.concatenate([_unstack_heads(carry[2 * n + 1], tq) for n in range(n_streams)], axis=1)

    is_latent = pl.program_id(1) * tq >= lc
    o = lax.cond(is_latent, lambda: finish(latent_keys(carry)), lambda: finish(carry))
    o_ref[0] = o.astype(BF16)


def _attn_global(q, k, v, lc, tq, tk):
    b, t, _ = q.shape
    nk = (t - lc) // tk
    return pl.pallas_call(
        functools.partial(_attn_global_kernel, lc=lc, tk=tk, nk=nk),
        out_shape=jax.ShapeDtypeStruct((b, t, GROUP_WIDTH), BF16),
        grid=(b, t // tq),
        in_specs=[pl.BlockSpec((1, tq, GROUP_WIDTH), lambda bi, j: (bi, j, 0)),
                  pl.BlockSpec((1, t, LANES), lambda bi, j: (bi, 0, 0)),
                  pl.BlockSpec((1, t, LANES), lambda bi, j: (bi, 0, 0))],
        out_specs=pl.BlockSpec((1, tq, GROUP_WIDTH), lambda bi, j: (bi, j, 0)),
        compiler_params=_cparams(("arbitrary", "arbitrary")),
        name="attn_global",
    )(q, k, v)


def _attn_window_kernel(sink_ref, q_ref, kp_ref, kc_ref, kn_ref, vp_ref, vc_ref, vn_ref, kx_ref, vx_ref, o_ref,
                        *, nctx, nblk):
    tq = q_ref.shape[1]
    halo = Q_BLOCK
    j = pl.program_id(1)
    is_lat = j >= nctx
    off = 4 * tq
    lo_prev = jnp.where(j > nctx, -halo, off)
    lo_cur = jnp.where(is_lat, -halo, off)
    lo_next = jnp.where(jnp.logical_and(is_lat, j + 1 <= nblk - 1), -halo, off)
    iq = _iota((2 * tq, tq + 2 * halo), 0) % tq
    col = _iota((2 * tq, tq + 2 * halo), 1)
    diff = col - halo - iq
    lo = jnp.where(col < halo, lo_prev, jnp.where(col < halo + tq, lo_cur, lo_next))
    ok = jnp.logical_and(diff >= lo, diff <= halo)
    top = _iota((2 * tq, 1), 0) < tq
    kw = jnp.concatenate([kp_ref[0], kc_ref[0], kn_ref[0]], axis=0)
    vw = _with_ones(jnp.concatenate([vp_ref[0], vc_ref[0], vn_ref[0]], axis=0))
    kx = kx_ref[0]
    vx = _with_ones(vx_ref[0])
    streams = range(GROUP_WIDTH // LANES)
    qs = [_stack_heads(q_ref[0, :, n * LANES:(n + 1) * LANES]) for n in streams]
    s_w = [jnp.where(ok, _dot_nt(q, kw), NEG) for q in qs]
    s_x = [_dot_nt(q, kx) for q in qs]
    outs = []
    for n in streams:
        sink = jnp.where(top, sink_ref[n], sink_ref[n + KV_HEADS]) * LOG2E
        m = jnp.maximum(jnp.maximum(jnp.max(s_w[n], axis=-1, keepdims=True),
                                    jnp.max(s_x[n], axis=-1, keepdims=True)), sink)
        acc = _dot(jnp.exp2(s_w[n] - m).astype(BF16), vw) + _dot(jnp.exp2(s_x[n] - m).astype(BF16), vx)
        outs.append(_unstack_heads(acc, tq, extra_sum=jnp.exp2(sink - m)))
    o_ref[0] = jnp.concatenate(outs, axis=1).astype(BF16)


def _attn_window(q, k, v, sink, lc, tq):
    b, t, _ = q.shape
    nblk = t // tq
    nctx = lc // tq
    per = tq // Q_BLOCK
    cur = pl.BlockSpec((1, tq, LANES), lambda bi, j: (bi, j, 1))
    prev = pl.BlockSpec((1, Q_BLOCK, LANES), lambda bi, j: (bi, jnp.maximum(j * per - 1, 0), 1))
    nxt = pl.BlockSpec((1, Q_BLOCK, LANES), lambda bi, j: (bi, jnp.minimum((j + 1) * per, nblk * per - 1), 1))
    ctx = pl.BlockSpec((1, lc, LANES), lambda bi, j: (bi, 0, 1))
    return pl.pallas_call(
        functools.partial(_attn_window_kernel, nctx=nctx, nblk=nblk),
        out_shape=jax.ShapeDtypeStruct((b, t, GROUP_WIDTH), BF16),
        grid=(b, nblk),
        in_specs=[pl.BlockSpec(memory_space=pltpu.SMEM),
                  pl.BlockSpec((1, tq, GROUP_WIDTH), lambda bi, j: (bi, j, 1)),
                  prev, cur, nxt,
                  prev, cur, nxt,
                  ctx, ctx],
        out_specs=pl.BlockSpec((1, tq, GROUP_WIDTH), lambda bi, j: (bi, j, 0)),
        compiler_params=_cparams(("arbitrary", "arbitrary")),
        name="attn_window",
    )(sink, q, k, k, k, v, v, v, k, v)


def _chunk_tri(rows, reverse):
    t = _iota((rows, rows), 0)
    u = _iota((rows, rows), 1)
    same = t // CHUNK == u // CHUNK
    return jnp.logical_and(same, (u >= t) if reverse else (u <= t)).astype(BF16)


def _tile4(x):
    return jnp.concatenate([x, x, x, x], axis=0)


def _rev_block(i, ncb, nblk):
    return jnp.where(i < ncb, ncb - 1 - i, nblk - 1 - (i - ncb))


class _Item:
    pass


def _scan_items(directions, nchunk):
    items = []
    for pos in range(nchunk):
        for d in directions:
            it = _Item()
            it.d = d
            it.c = nchunk - 1 - pos if d.reverse else pos
            it.sl = slice(it.c * CHUNK, (it.c + 1) * CHUNK)
            items.append(it)
    return items


def _gla_kernel(qkv_f_ref, la_f_ref, qkv_b_ref, la_b_ref, of_ref, ob_ref, sf_ref, sb_ref):
    @pl.when(pl.program_id(1) == 0)
    def _():
        sf_ref[...] = jnp.zeros_like(sf_ref)
        sb_ref[...] = jnp.zeros_like(sb_ref)

    rows = qkv_f_ref.shape[1]
    same_kd = _iota((GROUP_WIDTH, GLA_KW), 0) // CHUNK == _iota((GROUP_WIDTH, GLA_KW), 1) // GLA_DK
    same_kv = _iota((GROUP_WIDTH, GROUP_WIDTH), 0) // CHUNK == _iota((GROUP_WIDTH, GROUP_WIDTH), 1) // HEAD_DIM
    t_pos = _iota((CHUNK, GROUP_WIDTH), 0)
    s_pos = _iota((CHUNK, GROUP_WIDTH), 1) % CHUNK

    directions = []
    for qkv_ref, la_ref, o_ref, s_ref, reverse in ((qkv_f_ref, la_f_ref, of_ref, sf_ref, False),
                                                   (qkv_b_ref, la_b_ref, ob_ref, sb_ref, True)):
        d = _Item()
        d.reverse, d.o_ref, d.s_ref = reverse, o_ref, s_ref
        d.b = _dot01(_chunk_tri(rows, reverse), la_ref[0])
        d.q = qkv_ref[0, :, 0:GLA_KW]
        d.k = qkv_ref[0, :, GLA_KW:2 * GLA_KW]
        d.v = qkv_ref[0, :, 2 * GLA_KW:2 * GLA_KW + GROUP_WIDTH].astype(BF16)
        d.causal = (s_pos >= t_pos) if reverse else (s_pos <= t_pos)
        d.end = 0 if reverse else CHUNK - 1
        d.state = s_ref[...]
        directions.append(d)
    items = _scan_items(directions, rows // CHUNK)

    for it in items:
        d = it.d
        bc = d.b[it.sl]
        b_end = bc[d.end:d.end + 1]
        b_mid = bc[CHUNK // 2:CHUNK // 2 + 1]
        qc, kc = d.q[it.sl], d.k[it.sl]
        it.vc = d.v[it.sl]
        it.q_inter = (qc * jnp.exp(bc)).astype(BF16)
        it.q_intra = (qc * jnp.exp(bc - b_mid)).astype(BF16)
        k_intra = (kc * jnp.exp(b_mid - bc)).astype(BF16)
        it.k_state = (kc * jnp.exp(b_end - bc)).astype(BF16)
        it.decay = jnp.exp(b_end)
        it.k_bd = jnp.where(same_kd, _tile4(k_intra), jnp.zeros((), BF16))
        it.v_bd = jnp.where(same_kv, _tile4(it.vc), jnp.zeros((), BF16))
    for it in items:
        it.scores = jnp.where(it.d.causal, _dot_nt(it.q_intra, it.k_bd), 0.0).astype(BF16)
    for it in items:
        it.o_intra = _dot(it.scores, it.v_bd)
        it.update = jnp.where(same_kd, _dot_tn(it.vc, it.k_state), 0.0)
    for it in items:
        it.state_in = it.d.state
        it.d.state = it.d.state * it.decay + it.update
    for it in items:
        it.d.o_ref[0, it.sl, :] = it.o_intra + _dot_nt(it.q_inter, it.state_in.astype(BF16))
    for d in directions:
        d.s_ref[...] = d.state


def _gla_scan(zc, lc, rows):
    b, t, _ = zc.shape
    nblk = t // rows
    ncb = lc // rows
    fwd = lambda col: (lambda bi, i: (bi, i, col))
    rev = lambda col: (lambda bi, i: (bi, _rev_block(i, ncb, nblk), col))
    qkv_w = 2 * GLA_KW + GROUP_WIDTH
    return pl.pallas_call(
        _gla_kernel,
        out_shape=(jax.ShapeDtypeStruct((b, t, GROUP_WIDTH), F32),
                   jax.ShapeDtypeStruct((b, t, GROUP_WIDTH), F32)),
        grid=(b, nblk),
        in_specs=[pl.BlockSpec((1, rows, qkv_w), fwd(0)),
                  pl.BlockSpec((1, rows, GLA_KW), fwd(6)),
                  pl.BlockSpec((1, rows, qkv_w), rev(0)),
                  pl.BlockSpec((1, rows, GLA_KW), rev(7))],
        out_specs=(pl.BlockSpec((1, rows, GROUP_WIDTH), fwd(0)),
                   pl.BlockSpec((1, rows, GROUP_WIDTH), rev(0))),
        scratch_shapes=[pltpu.VMEM((GROUP_WIDTH, GLA_KW), F32),
                        pltpu.VMEM((GROUP_WIDTH, GLA_KW), F32)],
        compiler_params=_cparams(("arbitrary", "arbitrary")),
        name="gla_scan",
    )(zc, zc, zc, zc)


def _expand_heads(g, base, rows):
    hid = _iota((rows, GROUP_WIDTH), 1) // HEAD_DIM
    cols = [jnp.broadcast_to(g[:, base + h:base + h + 1], (rows, GROUP_WIDTH)) for h in range(GROUP_HEADS)]
    return jnp.where(hid == 0, cols[0], jnp.where(hid == 1, cols[1], jnp.where(hid == 2, cols[2], cols[3])))


def _mlstm_kernel(qkv_f_ref, gate_f_ref, qkv_b_ref, gate_b_ref, of_ref, ob_ref,
                  cf_ref, nf_ref, mf_ref, cb_ref, nb_ref, mb_ref):
    @pl.when(pl.program_id(1) == 0)
    def _():
        for r in (cf_ref, nf_ref, mf_ref, cb_ref, nb_ref, mb_ref):
            r[...] = jnp.zeros_like(r)

    rows = qkv_f_ref.shape[1]
    gw = GROUP_WIDTH
    same_head = _iota((gw, gw), 0) // HEAD_DIM == _iota((gw, gw), 1) // HEAD_DIM
    ones_bd = same_head.astype(BF16)
    hid = _iota((CHUNK, gw), 1) // HEAD_DIM
    t_pos = _iota((CHUNK, gw), 0)
    s_pos = _iota((CHUNK, gw), 1) % CHUNK
    diag = (s_pos == t_pos).astype(F32)
    ones_cc = jnp.ones((CHUNK, CHUNK), BF16)

    directions = []
    for index, (qkv_ref, gate_ref, o_ref, c_ref, n_ref, m_ref) in enumerate(
            ((qkv_f_ref, gate_f_ref, of_ref, cf_ref, nf_ref, mf_ref),
             (qkv_b_ref, gate_b_ref, ob_ref, cb_ref, nb_ref, mb_ref))):
        d = _Item()
        d.reverse = index == 1
        d.o_ref, d.c_ref, d.n_ref, d.m_ref = o_ref, c_ref, n_ref, m_ref
        g = gate_ref[0]
        i_col = GROUP_HEADS * index
        f_col = 2 * GROUP_HEADS + GROUP_HEADS * index
        f_all = _dot01(_chunk_tri(rows, d.reverse), g)
        d.ig = _expand_heads(g, i_col, rows)
        d.f_cum = _expand_heads(f_all, f_col, rows)
        d.q = qkv_ref[0, :, 0:gw]
        d.k = qkv_ref[0, :, gw:2 * gw]
        d.v = qkv_ref[0, :, 2 * gw:3 * gw].astype(BF16)
        d.causal = (s_pos >= t_pos) if d.reverse else (s_pos <= t_pos)
        d.end = 0 if d.reverse else CHUNK - 1
        d.c_state = c_ref[...]
        d.n_state = n_ref[...]
        d.m_state = m_ref[...]
        directions.append(d)
    items = _scan_items(directions, rows // CHUNK)

    for it in items:
        d = it.d
        it.fc, it.ic = d.f_cum[it.sl], d.ig[it.sl]
        it.qc, it.kc, it.vc = d.q[it.sl], d.k[it.sl], d.v[it.sl]
        it.qb = it.qc.astype(BF16)
        it.key_term = _dot01(ones_cc, (it.fc - it.ic) * diag)
    for it in items:
        it.logw = jnp.where(it.d.causal, it.fc - it.key_term, NEG)
        row_max = jnp.full((CHUNK, gw), NEG, F32)
        for h in range(GROUP_HEADS):
            mh = jnp.max(jnp.where(hid == h, it.logw, NEG), axis=-1, keepdims=True)
            row_max = jnp.where(hid == h, mh, row_max)
        it.row_max = row_max
        it.k_bd = jnp.where(same_head, _tile4(it.kc.astype(BF16)), jnp.zeros((), BF16))
        it.v_bd = jnp.where(same_head, _tile4(it.vc), jnp.zeros((), BF16))
    for it in items:
        d = it.d
        it.log_inter = it.fc + d.m_state
        it.m_t = jnp.maximum(it.log_inter, it.row_max)
        d.m_state = it.m_new = it.m_t[d.end:d.end + 1]
    for it in items:
        end = it.d.end
        it.w = jnp.exp(it.logw - it.m_t)
        it.w_inter = jnp.exp(it.log_inter - it.m_t)
        it.decay = it.w_inter[end:end + 1]
        it.k_end = it.kc * jnp.exp(it.fc[end:end + 1] - it.fc + it.ic - it.m_new)
    for it in items:
        it.qk = _dot_nt(it.qb, it.k_bd) * it.w
    for it in items:
        it.num = _dot(it.qk.astype(BF16), it.v_bd)
        it.den = _dot01_r(it.qk, ones_bd)
        it.update = jnp.where(same_head, _dot_tn(it.k_end.astype(BF16), it.vc), 0.0)
    for it in items:
        d = it.d
        it.c_in, it.n_in = d.c_state, d.n_state
        d.c_state = d.c_state * it.decay + it.update
        d.n_state = d.n_state * it.decay + jnp.sum(it.k_end, axis=0, keepdims=True)
    for it in items:
        num = it.num + it.w_inter * _dot(it.qb, it.c_in.astype(BF16))
        den = it.den + it.w_inter * _dot((it.qc * it.n_in).astype(BF16), ones_bd)
        it.d.o_ref[0, it.sl, :] = num / jnp.maximum(jnp.abs(den), jnp.exp(-it.m_t))
    for d in directions:
        d.c_ref[...] = d.c_state
        d.n_ref[...] = d.n_state
        d.m_ref[...] = d.m_state


def _mlstm_scan(zd, lc, rows):
    b, t, _ = zd.shape
    nblk = t // rows
    ncb = lc // rows
    gw = GROUP_WIDTH
    fwd = lambda col: (lambda bi, i: (bi, i, col))
    rev = lambda col: (lambda bi, i: (bi, _rev_block(i, ncb, nblk), col))
    gate_col = 4 * gw // LANES
    state = [pltpu.VMEM((gw, gw), F32), pltpu.VMEM((1, gw), F32), pltpu.VMEM((1, gw), F32)]
    return pl.pallas_call(
        _mlstm_kernel,
        out_shape=(jax.ShapeDtypeStruct((b, t, gw), F32),
                   jax.ShapeDtypeStruct((b, t, gw), F32)),
        grid=(b, nblk),
        in_specs=[pl.BlockSpec((1, rows, 3 * gw), fwd(0)),
                  pl.BlockSpec((1, rows, LANES), fwd(gate_col)),
                  pl.BlockSpec((1, rows, 3 * gw), rev(0)),
                  pl.BlockSpec((1, rows, LANES), rev(gate_col))],
        out_specs=(pl.BlockSpec((1, rows, gw), fwd(0)),
                   pl.BlockSpec((1, rows, gw), rev(0))),
        scratch_shapes=state + state,
        compiler_params=_cparams(("arbitrary", "arbitrary")),
        name="mlstm_scan",
    )(zd, zd, zd, zd)


def _head_rms(o, gain, ones_bd):
    ss = _dot01_r(o * o, ones_bd)
    return o * lax.rsqrt(ss * (1.0 / HEAD_DIM) + EPS) * gain


def _out_kernel(*refs, n_x, ncb, skip_blocks):
    x_refs = refs[:n_x]
    (mod_ref, oa_ref, ob_ref, gf_ref, gb_ref, r_ref, mf_ref, mb_ref, og_ref,
     gains_ref, g2_ref, wo_ref, w1_ref, w2_ref, y_ref) = refs[n_x:]
    d = x_refs[0].shape[2]
    x = _residual_rows(x_refs, ncb, skip_blocks)
    mod = mod_ref[0, 0]
    ga1 = mod[:, 2 * d:3 * d]
    sh2 = mod[:, 3 * d:4 * d]
    sc2 = mod[:, 4 * d:5 * d]
    ga2 = mod[:, 5 * d:6 * d]
    ones_bd = _group_ones(GROUP_WIDTH, HEAD_DIM)
    gains = gains_ref[...]

    r = r_ref[0]
    o_c = _head_rms(gf_ref[0] + gb_ref[0], gains[0:1], ones_bd) * (r * _sigmoid(r))
    o_d = _sigmoid(og_ref[0]) * _head_rms(mf_ref[0] + mb_ref[0], gains[1:2], ones_bd)
    o_cat = jnp.concatenate([oa_ref[0], ob_ref[0], o_c.astype(BF16), o_d.astype(BF16)], axis=1)
    x = x + ga1 * _dot(o_cat, wo_ref[...])

    ms = jnp.mean(x * x, axis=-1, keepdims=True)
    h = (x * lax.rsqrt(ms + EPS)) * g2_ref[...]
    hb = (h * (1.0 + sc2) + sh2).astype(BF16)
    hid = jnp.maximum(_dot(hb, w1_ref[...]), 0.0)
    hid = (hid * hid).astype(BF16)
    y_ref[0] = x + ga2 * _dot(hid, w2_ref[...])


def _out_mlp(xs, modsel, oa, ob, gf, gb, zc, mf, mb, zd, gains, g2, wo, w1, w2, tm, ncb, skip_blocks):
    b, _, d = xs[0].shape
    t = sum(a.shape[1] for a in xs)
    nblk = t // tm - skip_blocks
    gw = GROUP_WIDTH
    row = lambda col: (lambda bi, j: (bi, j + skip_blocks, col))
    const = lambda bi, j: (0, 0)
    resident = lambda shape: pl.BlockSpec(shape, const, pipeline_mode=pl.Buffered(1))
    return pl.pallas_call(
        functools.partial(_out_kernel, n_x=len(xs), ncb=ncb, skip_blocks=skip_blocks),
        out_shape=jax.ShapeDtypeStruct((b, nblk * tm, d), F32),
        grid=(b, nblk),
        in_specs=_residual_specs(xs, tm, ncb, skip_blocks) + [
                  pl.BlockSpec((1, 1, 1, N_MOD * d),
                               lambda bi, j: (bi, jnp.where(j + skip_blocks >= ncb, 1, 0), 0, 0)),
                  pl.BlockSpec((1, tm, gw), row(0)),
                  pl.BlockSpec((1, tm, gw), row(0)),
                  pl.BlockSpec((1, tm, gw), row(0)),
                  pl.BlockSpec((1, tm, gw), row(0)),
                  pl.BlockSpec((1, tm, gw), row(2)),
                  pl.BlockSpec((1, tm, gw), row(0)),
                  pl.BlockSpec((1, tm, gw), row(0)),
                  pl.BlockSpec((1, tm, gw), row(3)),
                  pl.BlockSpec((2, gw), const),
                  pl.BlockSpec((1, d), const),
                  resident((d, d)),
                  resident((d, 4 * d)),
                  resident((4 * d, d))],
        out_specs=pl.BlockSpec((1, tm, d), lambda bi, j: (bi, j, 0)),
        compiler_params=_cparams(("arbitrary", "arbitrary")),
        name="out_mlp",
    )(*xs, modsel, oa, ob, gf, gb, zc, mf, mb, zd, gains, g2, wo, w1, w2)


def _projection_columns():
    gw, kw = GROUP_WIDTH, KV_HEADS * HEAD_DIM
    splits = (gw, kw, kw, gw, kw, kw, GLA_KW, GLA_KW, gw, gw, 2 * GLA_GATE_RANK,
              gw, gw, gw, gw, 2 * GROUP_HEADS, 2 * GROUP_HEADS)
    offs = np.concatenate([[0], np.cumsum(splits)])
    in_width = int(offs[-1])
    half_split = np.concatenate([np.arange(0, HEAD_DIM, 2), np.arange(1, HEAD_DIM, 2)])
    plain = np.arange(HEAD_DIM)
    cols = []
    mixers = (0, 3)
    for base in mixers:
        cols += [offs[base] + h * HEAD_DIM + half_split for h in ATTN_HEAD_ORDER]
    for base in mixers:
        cols += [offs[base + 1] + g * HEAD_DIM + half_split for g in range(KV_HEADS)]
    for base in mixers:
        cols += [offs[base + 2] + g * HEAD_DIM + plain for g in range(KV_HEADS)]
    cols.append(np.arange(offs[6], offs[11]))
    cols.append(np.full(W_C - (offs[11] - offs[6]), in_width))
    cols.append(np.arange(offs[11], offs[17]))
    cols.append(np.full(W_D - (offs[17] - offs[11]), in_width))
    cols = np.concatenate(cols).astype(np.int32)
    assert cols.shape[0] == W_ALL
    return cols, in_width, half_split


def _rope_tables(seq, lc):
    rows = seq // GRID_W
    row = jnp.repeat(jnp.arange(rows, dtype=F32), GRID_W)
    col = jnp.tile(jnp.arange(GRID_W, dtype=F32), rows)
    n_freq = HEAD_DIM // 4
    inv = ROPE_THETA ** (-jnp.arange(n_freq, dtype=F32) / n_freq)
    ang = jnp.concatenate([row[:, None] * inv, col[:, None] * inv], axis=-1)
    cos, sin = jnp.cos(ang), jnp.sin(ang)
    cos_t = jnp.tile(cos, (1, 4))
    sin_t = jnp.tile(jnp.concatenate([-sin, sin], axis=-1), (1, 2))
    cos_t = jnp.concatenate([jnp.ones((lc, LANES), F32), cos_t], axis=0)
    sin_t = jnp.concatenate([jnp.zeros((lc, LANES), F32), sin_t], axis=0)
    return cos_t, sin_t


def kernel(x, c, ctx, c_ctx, w_mod, b_mod, g_norm1, g_norm2, w_in, g_q_a, g_k_a, g_q_b, g_k_b, sink_b,
           w_gla_gate, b_gla_gate, g_gla_out, b_mlstm_i, b_mlstm_f, g_mlstm_out, w_out, w_mlp1, w_mlp2):
    b, seq, d = x.shape
    lc = ctx.shape[1]
    depth = w_mod.shape[0]
    tm = 256
    assert d == 4 * GROUP_WIDTH and lc % tm == 0 and seq % tm == 0 and seq % GRID_W == 0 and b + 1 <= MOD_ROWS

    cols, in_width, half_split = _projection_columns()
    cos_t, sin_t = _rope_tables(seq, lc)
    attn_rows = np.concatenate([h * HEAD_DIM + np.arange(HEAD_DIM) for h in ATTN_HEAD_ORDER])
    out_rows = np.concatenate([attn_rows, GROUP_WIDTH + attn_rows, np.arange(2 * GROUP_WIDTH, d)])

    cc = jnp.zeros((MOD_ROWS, d), F32).at[0:b].set(c).at[b].set(c_ctx)
    mod_all = _modulation(cc, w_mod, b_mod)

    xs = (ctx, x)
    for l in range(depth):
        modsel = jnp.stack([jnp.broadcast_to(mod_all[l, b], (b, N_MOD * d)), mod_all[l, 0:b]], axis=1)
        modsel = modsel.reshape(b, 2, 1, N_MOD * d)

        w_ext = jnp.concatenate([w_in[l], jnp.zeros((d, 1), F32)], axis=1)
        w_big = jnp.take(w_ext, cols, axis=1).astype(BF16)
        tiled = lambda g, n: jnp.tile(g[half_split], n)
        gains_qk = jnp.stack([tiled(g_q_a[l], GROUP_HEADS), tiled(g_q_b[l], GROUP_HEADS),
                              jnp.concatenate([tiled(g_k_a[l], KV_HEADS), tiled(g_k_b[l], KV_HEADS)])])
        wg = jnp.zeros((LANES, 2 * GLA_KW), F32)
        wg = wg.at[0:GLA_GATE_RANK, 0:GLA_KW].set(w_gla_gate[l, 0])
        wg = wg.at[GLA_GATE_RANK:2 * GLA_GATE_RANK, GLA_KW:].set(w_gla_gate[l, 1]).astype(BF16)
        bg = b_gla_gate[l].reshape(1, 2 * GLA_KW)
        bm = jnp.zeros((1, LANES), F32)
        bm = bm.at[0, 0:2 * GROUP_HEADS].set(b_mlstm_i[l].reshape(-1))
        bm = bm.at[0, 2 * GROUP_HEADS:4 * GROUP_HEADS].set(b_mlstm_f[l].reshape(-1))

        q, k, v, zc, zd = _projection(xs, modsel, g_norm1[l].reshape(1, d), w_big, gains_qk,
                                      cos_t, sin_t, wg, bg, bm, tm, lc // tm)
        oa = _attn_global(q, k, v, lc, tq=256, tk=512)
        ob = _attn_window(q, k, v, sink_b[l], lc, tq=256)
        gf, gb = _gla_scan(zc, lc, tm)
        mf, mb = _mlstm_scan(zd, lc, tm)

        gains_out = jnp.stack([jnp.tile(g_gla_out[l], 4), jnp.tile(g_mlstm_out[l], 4)])
        last = l == depth - 1
        xs = (_out_mlp(xs, modsel, oa, ob, gf, gb, zc, mf, mb, zd, gains_out, g_norm2[l].reshape(1, d),
                       w_out[l][out_rows].astype(BF16), w_mlp1[l].astype(BF16), w_mlp2[l].astype(BF16), tm,
                       ncb=lc // tm, skip_blocks=lc // tm if last else 0),)
    return xs[0]
```

```python
import functools

import numpy as np
import jax
import jax.numpy as jnp
from jax import lax
from jax.experimental import pallas as pl
from jax.experimental.pallas import tpu as pltpu

F32 = jnp.float32
BF16 = jnp.bfloat16

HEAD_DIM = 64
GROUP_HEADS = 4
GROUP_WIDTH = GROUP_HEADS * HEAD_DIM
KV_HEADS = 2
GRID_W = 64
Q_BLOCK = 128
ROPE_THETA = 10000.0
GLA_DK = 32
GLA_KW = GROUP_HEADS * GLA_DK
GLA_GATE_RANK = 16
GLA_GATE_TAU = 16.0
N_MOD = 6
EPS = 1e-6
LOG2E = 1.4426950408889634
NEG = -1e30
CHUNK = 64
LANES = 128
MOD_ROWS = 16
VMEM_LIMIT = 56 * 1024 * 1024

ATTN_HEAD_ORDER = (0, 2, 1, 3)
W_Q = 2 * GROUP_WIDTH
W_KV = 2 * KV_HEADS * HEAD_DIM
W_C = 896
W_D = 1152
W_ALL = W_Q + 2 * W_KV + W_C + W_D


def _cparams(sem):
    return pltpu.CompilerParams(dimension_semantics=sem, vmem_limit_bytes=VMEM_LIMIT)


def _dot(a, b):
    return jnp.dot(a, b, preferred_element_type=F32)


def _dot_nt(a, b):
    return lax.dot_general(a, b, (((1,), (1,)), ((), ())), preferred_element_type=F32)


def _dot_tn(a, b):
    return lax.dot_general(a, b, (((0,), (0,)), ((), ())), preferred_element_type=F32)


def _split3(x):
    hi = x.astype(BF16)
    r1 = x - hi.astype(F32)
    mid = r1.astype(BF16)
    lo = (r1 - mid.astype(F32)).astype(BF16)
    return hi, mid, lo


def _dot01(a01, x):
    hi, mid, lo = _split3(x)
    return _dot(a01, hi) + _dot(a01, mid) + _dot(a01, lo)


def _dot01_r(x, b01):
    hi, mid, lo = _split3(x)
    return _dot(hi, b01) + _dot(mid, b01) + _dot(lo, b01)


def _dot01_r2(x, b01):
    hi = x.astype(BF16)
    lo = (x - hi.astype(F32)).astype(BF16)
    return _dot(hi, b01) + _dot(lo, b01)


def _log_sigmoid(x):
    return jnp.minimum(x, 0.0) - jnp.log(1.0 + jnp.exp(-jnp.abs(x)))


def _sigmoid(x):
    return 1.0 / (1.0 + jnp.exp(-x))


def _iota(shape, dim):
    return lax.broadcasted_iota(jnp.int32, shape, dim)


def _group_ones(n, group):
    return (_iota((n, n), 0) // group == _iota((n, n), 1) // group).astype(BF16)


def _mod_kernel(c_ref, w_ref, b_ref, o_ref):
    c = c_ref[...]
    s = (c * _sigmoid(c)).astype(BF16)
    o_ref[0] = _dot(s, w_ref[0].astype(BF16)) + b_ref[0]


def _modulation(cc, w_mod, b_mod):
    depth, d, n = w_mod.shape
    tn = 1536
    return pl.pallas_call(
        _mod_kernel,
        out_shape=jax.ShapeDtypeStruct((depth, MOD_ROWS, n), F32),
        grid=(depth, n // tn),
        in_specs=[pl.BlockSpec((MOD_ROWS, d), lambda l, j: (0, 0)),
                  pl.BlockSpec((1, d, tn), lambda l, j: (l, 0, j)),
                  pl.BlockSpec((1, 1, tn), lambda l, j: (l, 0, j))],
        out_specs=pl.BlockSpec((1, MOD_ROWS, tn), lambda l, j: (l, 0, j)),
        compiler_params=_cparams(("arbitrary", "arbitrary")),
        name="modulation",
    )(cc, w_mod, b_mod.reshape(depth, 1, n))


def _qk_norm_rope(z, gain, cos, sin, ones_bd, scale):
    ss = _dot01_r2(z * z, ones_bd)
    y = z * lax.rsqrt(ss * (1.0 / HEAD_DIM) + EPS) * gain
    first_half = (_iota((z.shape[0], LANES), 1) % HEAD_DIM) < (HEAD_DIM // 2)
    outs = []
    for cb in range(z.shape[1] // LANES):
        yc = y[:, cb * LANES:(cb + 1) * LANES]
        partner = jnp.where(first_half, pltpu.roll(yc, LANES - HEAD_DIM // 2, 1),
                            pltpu.roll(yc, HEAD_DIM // 2, 1))
        outs.append((yc * cos + partner * sin) * scale)
    return jnp.concatenate(outs, axis=1)


def _residual_specs(xs, tm, ncb, skip_blocks=0):
    d = xs[0].shape[2]
    if len(xs) == 1:
        return [pl.BlockSpec((1, tm, d), lambda bi, j: (bi, j + skip_blocks, 0))]
    return [pl.BlockSpec((1, tm, d), lambda bi, j: (bi, jnp.minimum(j + skip_blocks, ncb - 1), 0)),
            pl.BlockSpec((1, tm, d), lambda bi, j: (bi, jnp.maximum(j + skip_blocks - ncb, 0), 0))]


def _residual_rows(x_refs, ncb, skip_blocks=0):
    if len(x_refs) == 1:
        return x_refs[0][0]
    return jnp.where(pl.program_id(1) + skip_blocks < ncb, x_refs[0][0], x_refs[1][0])


def _proj_kernel(*refs, n_x, ncb):
    x_refs = refs[:n_x]
    (mod_ref, g1_ref, w_ref, gains_ref, cos_ref, sin_ref, wg_ref, bg_ref, bm_ref,
     q_ref, k_ref, v_ref, vt_ref, zc_ref, zd_ref) = refs[n_x:]
    d = x_refs[0].shape[2]
    tm = x_refs[0].shape[1]
    x = _residual_rows(x_refs, ncb)
    mod = mod_ref[0, 0]
    sh1 = mod[:, 0:d]
    sc1 = mod[:, d:2 * d]
    ms = jnp.mean(x * x, axis=-1, keepdims=True)
    h = (x * lax.rsqrt(ms + EPS)) * g1_ref[...]
    hb = (h * (1.0 + sc1) + sh1).astype(BF16)

    ones_bd = _group_ones(GROUP_WIDTH, HEAD_DIM)
    cos = cos_ref[...]
    sin = sin_ref[...]
    gains = gains_ref[...]
    gw = GROUP_WIDTH

    z = _dot(hb, w_ref[:, 0:W_Q + 2 * W_KV])
    q_scale = HEAD_DIM ** -0.5 * LOG2E
    for n in range(2):
        q = _qk_norm_rope(z[:, n * gw:(n + 1) * gw], gains[n:n + 1], cos, sin, ones_bd, q_scale)
        q_ref[0, :, n * gw:(n + 1) * gw] = q.astype(BF16)
    k_ref[0] = _qk_norm_rope(z[:, W_Q:W_Q + W_KV], gains[2:3], cos, sin, ones_bd, 1.0).astype(BF16)
    v = z[:, W_Q + W_KV:W_Q + 2 * W_KV]
    v_ref[0] = v.astype(BF16)
    vt_ref[0] = v.T.astype(BF16)

    base = W_Q + 2 * W_KV
    z = _dot(hb, w_ref[:, base:base + W_C])
    zc_ref[0, :, 0:GLA_KW] = z[:, 0:GLA_KW] * (GLA_DK ** -0.5)
    zc_ref[0, :, GLA_KW:768] = z[:, GLA_KW:768]
    pre = _dot(z[:, 768:896].astype(BF16), wg_ref[...]) + bg_ref[...]
    zc_ref[0, :, 768:1024] = _log_sigmoid(pre) * (1.0 / GLA_GATE_TAU)

    z = _dot(hb, w_ref[:, base + W_C:W_ALL])
    zd_ref[0, :, 0:gw] = z[:, 0:gw]
    zd_ref[0, :, gw:2 * gw] = z[:, gw:2 * gw] * (HEAD_DIM ** -0.5)
    zd_ref[0, :, 2 * gw:4 * gw] = z[:, 2 * gw:4 * gw]
    gates = z[:, 4 * gw:4 * gw + LANES] + bm_ref[...]
    is_input_gate = _iota((tm, LANES), 1) < 2 * GROUP_HEADS
    zd_ref[0, :, 4 * gw:4 * gw + LANES] = jnp.where(is_input_gate, gates, _log_sigmoid(gates))


def _projection(xs, modsel, g1, w_big, gains, cos_t, sin_t, wg, bg, bm, tm, ncb):
    b, _, d = xs[0].shape
    t = sum(a.shape[1] for a in xs)
    nblk = t // tm
    const = lambda bi, j: (0, 0)
    return pl.pallas_call(
        functools.partial(_proj_kernel, n_x=len(xs), ncb=ncb),
        out_shape=(jax.ShapeDtypeStruct((b, t, W_Q), BF16),
                   jax.ShapeDtypeStruct((b, t, W_KV), BF16),
                   jax.ShapeDtypeStruct((b, t, W_KV), BF16),
                   jax.ShapeDtypeStruct((b, W_KV, t), BF16),
                   jax.ShapeDtypeStruct((b, t, 1024), F32),
                   jax.ShapeDtypeStruct((b, t, W_D), F32)),
        grid=(b, nblk),
        in_specs=_residual_specs(xs, tm, ncb) + [
                  pl.BlockSpec((1, 1, 1, N_MOD * d), lambda bi, j: (bi, jnp.where(j >= ncb, 1, 0), 0, 0)),
                  pl.BlockSpec((1, d), const),
                  pl.BlockSpec((d, W_ALL), const),
                  pl.BlockSpec((3, GROUP_WIDTH), const),
                  pl.BlockSpec((tm, LANES), lambda bi, j: (j, 0)),
                  pl.BlockSpec((tm, LANES), lambda bi, j: (j, 0)),
                  pl.BlockSpec((LANES, 2 * GLA_KW), const),
                  pl.BlockSpec((1, 2 * GLA_KW), const),
                  pl.BlockSpec((1, LANES), const)],
        out_specs=(pl.BlockSpec((1, tm, W_Q), lambda bi, j: (bi, j, 0)),
                   pl.BlockSpec((1, tm, W_KV), lambda bi, j: (bi, j, 0)),
                   pl.BlockSpec((1, tm, W_KV), lambda bi, j: (bi, j, 0)),
                   pl.BlockSpec((1, W_KV, tm), lambda bi, j: (bi, 0, j)),
                   pl.BlockSpec((1, tm, 1024), lambda bi, j: (bi, j, 0)),
                   pl.BlockSpec((1, tm, W_D), lambda bi, j: (bi, j, 0))),
        compiler_params=_cparams(("arbitrary", "arbitrary")),
        name="projection",
    )(*xs, modsel, g1, w_big, gains, cos_t, sin_t, wg, bg, bm)


def _stack_heads(q128):
    low = _iota(q128.shape, 1) < HEAD_DIM
    zero = jnp.zeros_like(q128)
    return jnp.concatenate([jnp.where(low, q128, zero), jnp.where(low, zero, q128)], axis=0)


def _with_ones(v):
    return jnp.concatenate([v, jnp.ones_like(v)], axis=1)


def _unstack_heads(acc, tq, extra_sum=None):
    low = _iota((tq, LANES), 1) < HEAD_DIM
    o = jnp.where(low, acc[:tq, :LANES], acc[tq:, :LANES])
    l = jnp.where(low, acc[:tq, LANES:], acc[tq:, LANES:])
    if extra_sum is not None:
        l = l + jnp.where(low, extra_sum[:tq], extra_sum[tq:])
    return o / l


def _attn_global_kernel(q_ref, k_ref, v_ref, o_ref, *, lc, tk, nk):
    tq = q_ref.shape[1]
    n_streams = GROUP_WIDTH // LANES
    qs = [_stack_heads(q_ref[0, :, n * LANES:(n + 1) * LANES]) for n in range(n_streams)]

    k = k_ref[0, 0:lc, :]
    v = _with_ones(v_ref[0, 0:lc, :])
    carry = []
    for q in qs:
        s = _dot_nt(q, k)
        m = jnp.max(s, axis=-1, keepdims=True)
        carry += [m, _dot(jnp.exp2(s - m).astype(BF16), v)]
    carry = tuple(carry)

    def latent_keys(carry):
        carry = list(carry)
        for c in range(nk):
            rows = slice(lc + c * tk, lc + (c + 1) * tk)
            k = k_ref[0, rows, :]
            v = _with_ones(v_ref[0, rows, :])
            for n, q in enumerate(qs):
                m, acc = carry[2 * n], carry[2 * n + 1]
                s = _dot_nt(q, k)
                m_new = jnp.maximum(m, jnp.max(s, axis=-1, keepdims=True))
                p = jnp.exp2(s - m_new).astype(BF16)
                carry[2 * n], carry[2 * n + 1] = m_new, jnp.exp2(m - m_new) * acc + _dot(p, v)
        return tuple(carry)

    def finish(carry):
        return jnp.concatenate([_unstack_heads(carry[2 * n + 1], tq) for n in range(n_streams)], axis=1)

    is_latent = pl.program_id(1) * tq >= lc
    o = lax.cond(is_latent, lambda: finish(latent_keys(carry)), lambda: finish(carry))
    o_ref[0] = o.astype(BF16)


def _attn_global(q, k, v, lc, tq, tk):
    b, t, _ = q.shape
    nk = (t - lc) // tk
    return pl.pallas_call(
        functools.partial(_attn_global_kernel, lc=lc, tk=tk, nk=nk),
        out_shape=jax.ShapeDtypeStruct((b, t, GROUP_WIDTH), BF16),
        grid=(b, t // tq),
        in_specs=[pl.BlockSpec((1, tq, GROUP_WIDTH), lambda bi, j: (bi, j, 0)),
                  pl.BlockSpec((1, t, LANES), lambda bi, j: (bi, 0, 0)),
                  pl.BlockSpec((1, t, LANES), lambda bi, j: (bi, 0, 0))],
        out_specs=pl.BlockSpec((1, tq, GROUP_WIDTH), lambda bi, j: (bi, j, 0)),
        compiler_params=_cparams(("arbitrary", "arbitrary")),
        name="attn_global",
    )(q, k, v)


ONES_ROWS = 16


def _attn_global_t_kernel(q_ref, k_ref, vt_ref, o_ref, *, lc, tk, nk):
    tq = q_ref.shape[1]
    n_tiles = GROUP_HEADS
    low = _iota((tq, LANES), 1) < HEAD_DIM
    parts = []
    for n in range(GROUP_WIDTH // LANES):
        q128 = q_ref[0, :, n * LANES:(n + 1) * LANES]
        zero = jnp.zeros_like(q128)
        parts += [jnp.where(low, q128, zero), jnp.where(low, zero, q128)]
    ones = jnp.ones((ONES_ROWS, tk), BF16)

    def scores(rows):
        k = k_ref[0, rows, :]
        return [_dot_nt(k, qp) for qp in parts]

    def chunk(s_t, rows, m, accs):
        m_out, acc_out = [], []
        for c in range(n_tiles):
            col_max = jnp.max(s_t[c], axis=0, keepdims=True)
            m_new = col_max if m is None else jnp.maximum(m[c], col_max)
            p_t = jnp.exp2(s_t[c] - m_new).astype(BF16)
            kv = c % KV_HEADS
            vx = jnp.concatenate([vt_ref[0, kv * HEAD_DIM:(kv + 1) * HEAD_DIM, rows], ones], axis=0)
            upd = _dot(vx, p_t)
            m_out.append(m_new)
            acc_out.append(upd if m is None else jnp.exp2(m[c] - m_new) * accs[c] + upd)
        return m_out, acc_out

    def key_chunks(first, count, m, accs):
        rows = [slice((first + c) * tk, (first + c + 1) * tk) for c in range(count)]
        s_next = scores(rows[0])
        for c in range(count):
            s_t = s_next
            if c + 1 < count:
                s_next = scores(rows[c + 1])
            m, accs = chunk(s_t, rows[c], m, accs)
        return m, accs

    n_ctx = lc // tk
    m, accs = key_chunks(0, n_ctx, None, None)

    def latent_keys(m, accs):
        return key_chunks(n_ctx, nk, m, accs)[1]

    def finish(accs):
        o_t = [a[0:HEAD_DIM] / a[HEAD_DIM:HEAD_DIM + 1] for a in accs]
        halves = [jnp.concatenate(o_t[2 * n:2 * n + 2], axis=0).T for n in range(n_tiles // 2)]
        return jnp.concatenate(halves, axis=1)

    is_latent = pl.program_id(1) * tq >= lc
    o = lax.cond(is_latent, lambda: finish(latent_keys(m, accs)), lambda: finish(accs))
    o_ref[0] = o.astype(BF16)


def _attn_global_t(q, k, vt, lc, tq, tk):
    b, t, _ = q.shape
    nk = (t - lc) // tk
    assert lc % tk == 0
    return pl.pallas_call(
        functools.partial(_attn_global_t_kernel, lc=lc, tk=tk, nk=nk),
        out_shape=jax.ShapeDtypeStruct((b, t, GROUP_WIDTH), BF16),
        grid=(b, t // tq),
        in_specs=[pl.BlockSpec((1, tq, GROUP_WIDTH), lambda bi, j: (bi, j, 0)),
                  pl.BlockSpec((1, t, LANES), lambda bi, j: (bi, 0, 0)),
                  pl.BlockSpec((1, LANES, t), lambda bi, j: (bi, 0, 0))],
        out_specs=pl.BlockSpec((1, tq, GROUP_WIDTH), lambda bi, j: (bi, j, 0)),
        compiler_params=_cparams(("arbitrary", "arbitrary")),
        name="attn_global",
    )(q, k, vt)


def _attn_window_kernel(sink_ref, q_ref, kp_ref, kc_ref, kn_ref, vp_ref, vc_ref, vn_ref, kx_ref, vx_ref, o_ref,
                        *, nctx, nblk):
    tq = q_ref.shape[1]
    halo = Q_BLOCK
    j = pl.program_id(1)
    is_lat = j >= nctx
    off = 4 * tq
    lo_prev = jnp.where(j > nctx, -halo, off)
    lo_cur = jnp.where(is_lat, -halo, off)
    lo_next = jnp.where(jnp.logical_and(is_lat, j + 1 <= nblk - 1), -halo, off)
    iq = _iota((2 * tq, tq + 2 * halo), 0) % tq
    col = _iota((2 * tq, tq + 2 * halo), 1)
    diff = col - halo - iq
    lo = jnp.where(col < halo, lo_prev, jnp.where(col < halo + tq, lo_cur, lo_next))
    ok = jnp.logical_and(diff >= lo, diff <= halo)
    top = _iota((2 * tq, 1), 0) < tq
    kw = jnp.concatenate([kp_ref[0], kc_ref[0], kn_ref[0]], axis=0)
    vw = _with_ones(jnp.concatenate([vp_ref[0], vc_ref[0], vn_ref[0]], axis=0))
    kx = kx_ref[0]
    vx = _with_ones(vx_ref[0])
    streams = range(GROUP_WIDTH // LANES)
    qs = [_stack_heads(q_ref[0, :, n * LANES:(n + 1) * LANES]) for n in streams]
    s_w = [jnp.where(ok, _dot_nt(q, kw), NEG) for q in qs]
    s_x = [_dot_nt(q, kx) for q in qs]
    outs = []
    for n in streams:
        sink = jnp.where(top, sink_ref[n], sink_ref[n + KV_HEADS]) * LOG2E
        m = jnp.maximum(jnp.maximum(jnp.max(s_w[n], axis=-1, keepdims=True),
                                    jnp.max(s_x[n], axis=-1, keepdims=True)), sink)
        acc = _dot(jnp.exp2(s_w[n] - m).astype(BF16), vw) + _dot(jnp.exp2(s_x[n] - m).astype(BF16), vx)
        outs.append(_unstack_heads(acc, tq, extra_sum=jnp.exp2(sink - m)))
    o_ref[0] = jnp.concatenate(outs, axis=1).astype(BF16)


def _attn_window(q, k, v, sink, lc, tq):
    b, t, _ = q.shape
    nblk = t // tq
    nctx = lc // tq
    per = tq // Q_BLOCK
    cur = pl.BlockSpec((1, tq, LANES), lambda bi, j: (bi, j, 1))
    prev = pl.BlockSpec((1, Q_BLOCK, LANES), lambda bi, j: (bi, jnp.maximum(j * per - 1, 0), 1))
    nxt = pl.BlockSpec((1, Q_BLOCK, LANES), lambda bi, j: (bi, jnp.minimum((j + 1) * per, nblk * per - 1), 1))
    ctx = pl.BlockSpec((1, lc, LANES), lambda bi, j: (bi, 0, 1))
    return pl.pallas_call(
        functools.partial(_attn_window_kernel, nctx=nctx, nblk=nblk),
        out_shape=jax.ShapeDtypeStruct((b, t, GROUP_WIDTH), BF16),
        grid=(b, nblk),
        in_specs=[pl.BlockSpec(memory_space=pltpu.SMEM),
                  pl.BlockSpec((1, tq, GROUP_WIDTH), lambda bi, j: (bi, j, 1)),
                  prev, cur, nxt,
                  prev, cur, nxt,
                  ctx, ctx],
        out_specs=pl.BlockSpec((1, tq, GROUP_WIDTH), lambda bi, j: (bi, j, 0)),
        compiler_params=_cparams(("arbitrary", "arbitrary")),
        name="attn_window",
    )(sink, q, k, k, k, v, v, v, k, v)


def _chunk_tri(rows, reverse):
    t = _iota((rows, rows), 0)
    u = _iota((rows, rows), 1)
    same = t // CHUNK == u // CHUNK
    return jnp.logical_and(same, (u >= t) if reverse else (u <= t)).astype(BF16)


def _tile4(x):
    return jnp.concatenate([x, x, x, x], axis=0)


def _rev_block(i, ncb, nblk):
    return jnp.where(i < ncb, ncb - 1 - i, nblk - 1 - (i - ncb))


class _Item:
    pass


def _scan_items(directions, nchunk):
    items = []
    for pos in range(nchunk):
        for d in directions:
            it = _Item()
            it.d = d
            it.c = nchunk - 1 - pos if d.reverse else pos
            it.sl = slice(it.c * CHUNK, (it.c + 1) * CHUNK)
            items.append(it)
    return items


def _gla_kernel(qkv_f_ref, la_f_ref, qkv_b_ref, la_b_ref, of_ref, ob_ref, sf_ref, sb_ref):
    @pl.when(pl.program_id(1) == 0)
    def _():
        sf_ref[...] = jnp.zeros_like(sf_ref)
        sb_ref[...] = jnp.zeros_like(sb_ref)

    rows = qkv_f_ref.shape[1]
    same_kd = _iota((GROUP_WIDTH, GLA_KW), 0) // CHUNK == _iota((GROUP_WIDTH, GLA_KW), 1) // GLA_DK
    same_kv = _iota((GROUP_WIDTH, GROUP_WIDTH), 0) // CHUNK == _iota((GROUP_WIDTH, GROUP_WIDTH), 1) // HEAD_DIM
    t_pos = _iota((CHUNK, GROUP_WIDTH), 0)
    s_pos = _iota((CHUNK, GROUP_WIDTH), 1) % CHUNK

    directions = []
    for qkv_ref, la_ref, o_ref, s_ref, reverse in ((qkv_f_ref, la_f_ref, of_ref, sf_ref, False),
                                                   (qkv_b_ref, la_b_ref, ob_ref, sb_ref, True)):
        d = _Item()
        d.reverse, d.o_ref, d.s_ref = reverse, o_ref, s_ref
        d.b = _dot01(_chunk_tri(rows, reverse), la_ref[0])
        d.q = qkv_ref[0, :, 0:GLA_KW]
        d.k = qkv_ref[0, :, GLA_KW:2 * GLA_KW]
        d.v = qkv_ref[0, :, 2 * GLA_KW:2 * GLA_KW + GROUP_WIDTH].astype(BF16)
        d.causal = (s_pos >= t_pos) if reverse else (s_pos <= t_pos)
        d.end = 0 if reverse else CHUNK - 1
        d.state = s_ref[...]
        directions.append(d)
    items = _scan_items(directions, rows // CHUNK)

    for it in items:
        d = it.d
        bc = d.b[it.sl]
        b_end = bc[d.end:d.end + 1]
        b_mid = bc[CHUNK // 2:CHUNK // 2 + 1]
        qc, kc = d.q[it.sl], d.k[it.sl]
        it.vc = d.v[it.sl]
        it.q_inter = (qc * jnp.exp(bc)).astype(BF16)
        it.q_intra = (qc * jnp.exp(bc - b_mid)).astype(BF16)
        k_intra = (kc * jnp.exp(b_mid - bc)).astype(BF16)
        it.k_state = (kc * jnp.exp(b_end - bc)).astype(BF16)
        it.decay = jnp.exp(b_end)
        it.k_bd = jnp.where(same_kd, _tile4(k_intra), jnp.zeros((), BF16))
        it.v_bd = jnp.where(same_kv, _tile4(it.vc), jnp.zeros((), BF16))
    for it in items:
        it.scores = jnp.where(it.d.causal, _dot_nt(it.q_intra, it.k_bd), 0.0).astype(BF16)
    for it in items:
        it.o_intra = _dot(it.scores, it.v_bd)
        it.update = jnp.where(same_kd, _dot_tn(it.vc, it.k_state), 0.0)
    for it in items:
        it.state_in = it.d.state
        it.d.state = it.d.state * it.decay + it.update
    for it in items:
        it.d.o_ref[0, it.sl, :] = it.o_intra + _dot_nt(it.q_inter, it.state_in.astype(BF16))
    for d in directions:
        d.s_ref[...] = d.state


def _gla_scan(zc, lc, rows):
    b, t, _ = zc.shape
    nblk = t // rows
    ncb = lc // rows
    fwd = lambda col: (lambda bi, i: (bi, i, col))
    rev = lambda col: (lambda bi, i: (bi, _rev_block(i, ncb, nblk), col))
    qkv_w = 2 * GLA_KW + GROUP_WIDTH
    return pl.pallas_call(
        _gla_kernel,
        out_shape=(jax.ShapeDtypeStruct((b, t, GROUP_WIDTH), F32),
                   jax.ShapeDtypeStruct((b, t, GROUP_WIDTH), F32)),
        grid=(b, nblk),
        in_specs=[pl.BlockSpec((1, rows, qkv_w), fwd(0)),
                  pl.BlockSpec((1, rows, GLA_KW), fwd(6)),
                  pl.BlockSpec((1, rows, qkv_w), rev(0)),
                  pl.BlockSpec((1, rows, GLA_KW), rev(7))],
        out_specs=(pl.BlockSpec((1, rows, GROUP_WIDTH), fwd(0)),
                   pl.BlockSpec((1, rows, GROUP_WIDTH), rev(0))),
        scratch_shapes=[pltpu.VMEM((GROUP_WIDTH, GLA_KW), F32),
                        pltpu.VMEM((GROUP_WIDTH, GLA_KW), F32)],
        compiler_params=_cparams(("arbitrary", "arbitrary")),
        name="gla_scan",
    )(zc, zc, zc, zc)


def _expand_heads(g, base, rows):
    hid = _iota((rows, GROUP_WIDTH), 1) // HEAD_DIM
    cols = [jnp.broadcast_to(g[:, base + h:base + h + 1], (rows, GROUP_WIDTH)) for h in range(GROUP_HEADS)]
    return jnp.where(hid == 0, cols[0], jnp.where(hid == 1, cols[1], jnp.where(hid == 2, cols[2], cols[3])))


def _mlstm_kernel(qkv_f_ref, gate_f_ref, qkv_b_ref, gate_b_ref, of_ref, ob_ref,
                  cf_ref, nf_ref, mf_ref, cb_ref, nb_ref, mb_ref):
    @pl.when(pl.program_id(1) == 0)
    def _():
        for r in (cf_ref, nf_ref, mf_ref, cb_ref, nb_ref, mb_ref):
            r[...] = jnp.zeros_like(r)

    rows = qkv_f_ref.shape[1]
    gw = GROUP_WIDTH
    same_head = _iota((gw, gw), 0) // HEAD_DIM == _iota((gw, gw), 1) // HEAD_DIM
    ones_bd = same_head.astype(BF16)
    hid = _iota((CHUNK, gw), 1) // HEAD_DIM
    t_pos = _iota((CHUNK, gw), 0)
    s_pos = _iota((CHUNK, gw), 1) % CHUNK
    diag = (s_pos == t_pos).astype(F32)
    ones_cc = jnp.ones((CHUNK, CHUNK), BF16)

    directions = []
    for index, (qkv_ref, gate_ref, o_ref, c_ref, n_ref, m_ref) in enumerate(
            ((qkv_f_ref, gate_f_ref, of_ref, cf_ref, nf_ref, mf_ref),
             (qkv_b_ref, gate_b_ref, ob_ref, cb_ref, nb_ref, mb_ref))):
        d = _Item()
        d.reverse = index == 1
        d.o_ref, d.c_ref, d.n_ref, d.m_ref = o_ref, c_ref, n_ref, m_ref
        g = gate_ref[0]
        i_col = GROUP_HEADS * index
        f_col = 2 * GROUP_HEADS + GROUP_HEADS * index
        f_all = _dot01(_chunk_tri(rows, d.reverse), g)
        d.ig = _expand_heads(g, i_col, rows)
        d.f_cum = _expand_heads(f_all, f_col, rows)
        d.q = qkv_ref[0, :, 0:gw]
        d.k = qkv_ref[0, :, gw:2 * gw]
        d.v = qkv_ref[0, :, 2 * gw:3 * gw].astype(BF16)
        d.causal = (s_pos >= t_pos) if d.reverse else (s_pos <= t_pos)
        d.end = 0 if d.reverse else CHUNK - 1
        d.c_state = c_ref[...]
        d.n_state = n_ref[...]
        d.m_state = m_ref[...]
        directions.append(d)
    items = _scan_items(directions, rows // CHUNK)

    for it in items:
        d = it.d
        it.fc, it.ic = d.f_cum[it.sl], d.ig[it.sl]
        it.qc, it.kc, it.vc = d.q[it.sl], d.k[it.sl], d.v[it.sl]
        it.qb = it.qc.astype(BF16)
        it.key_term = _dot01(ones_cc, (it.fc - it.ic) * diag)
    for it in items:
        it.logw = jnp.where(it.d.causal, it.fc - it.key_term, NEG)
        head_max = [jnp.max(it.logw[:, h * HEAD_DIM:(h + 1) * HEAD_DIM], axis=-1, keepdims=True)
                    for h in range(GROUP_HEADS)]
        it.row_max = jnp.where(hid < 2, jnp.where(hid == 0, head_max[0], head_max[1]),
                               jnp.where(hid == 2, head_max[2], head_max[3]))
        it.k_bd = jnp.where(same_head, _tile4(it.kc.astype(BF16)), jnp.zeros((), BF16))
        it.v_bd = jnp.where(same_head, _tile4(it.vc), jnp.zeros((), BF16))
    for it in items:
        d = it.d
        it.log_inter = it.fc + d.m_state
        it.m_t = jnp.maximum(it.log_inter, it.row_max)
        d.m_state = it.m_new = it.m_t[d.end:d.end + 1]
    for it in items:
        end = it.d.end
        it.w = jnp.exp(it.logw - it.m_t)
        it.w_inter = jnp.exp(it.log_inter - it.m_t)
        it.decay = it.w_inter[end:end + 1]
        it.k_end = it.kc * jnp.exp(it.fc[end:end + 1] - it.fc + it.ic - it.m_new)
    for it in items:
        it.qk = _dot_nt(it.qb, it.k_bd) * it.w
    for it in items:
        it.num = _dot(it.qk.astype(BF16), it.v_bd)
        it.den = _dot01_r(it.qk, ones_bd)
        it.update = jnp.where(same_head, _dot_tn(it.k_end.astype(BF16), it.vc), 0.0)
    for it in items:
        d = it.d
        it.c_in, it.n_in = d.c_state, d.n_state
        d.c_state = d.c_state * it.decay + it.update
        d.n_state = d.n_state * it.decay + jnp.sum(it.k_end, axis=0, keepdims=True)
    for it in items:
        num = it.num + it.w_inter * _dot(it.qb, it.c_in.astype(BF16))
        den = it.den + it.w_inter * _dot((it.qc * it.n_in).astype(BF16), ones_bd)
        it.d.o_ref[0, it.sl, :] = num / jnp.maximum(jnp.abs(den), jnp.exp(-it.m_t))
    for d in directions:
        d.c_ref[...] = d.c_state
        d.n_ref[...] = d.n_state
        d.m_ref[...] = d.m_state


def _mlstm_scan(zd, lc, rows):
    b, t, _ = zd.shape
    nblk = t // rows
    ncb = lc // rows
    gw = GROUP_WIDTH
    fwd = lambda col: (lambda bi, i: (bi, i, col))
    rev = lambda col: (lambda bi, i: (bi, _rev_block(i, ncb, nblk), col))
    gate_col = 4 * gw // LANES
    state = [pltpu.VMEM((gw, gw), F32), pltpu.VMEM((1, gw), F32), pltpu.VMEM((1, gw), F32)]
    return pl.pallas_call(
        _mlstm_kernel,
        out_shape=(jax.ShapeDtypeStruct((b, t, gw), F32),
                   jax.ShapeDtypeStruct((b, t, gw), F32)),
        grid=(b, nblk),
        in_specs=[pl.BlockSpec((1, rows, 3 * gw), fwd(0)),
                  pl.BlockSpec((1, rows, LANES), fwd(gate_col)),
                  pl.BlockSpec((1, rows, 3 * gw), rev(0)),
                  pl.BlockSpec((1, rows, LANES), rev(gate_col))],
        out_specs=(pl.BlockSpec((1, rows, gw), fwd(0)),
                   pl.BlockSpec((1, rows, gw), rev(0))),
        scratch_shapes=state + state,
        compiler_params=_cparams(("arbitrary", "arbitrary")),
        name="mlstm_scan",
    )(zd, zd, zd, zd)


def _head_rms(o, gain, ones_bd):
    ss = _dot01_r(o * o, ones_bd)
    return o * lax.rsqrt(ss * (1.0 / HEAD_DIM) + EPS) * gain


def _out_kernel(*refs, n_x, ncb, skip_blocks):
    x_refs = refs[:n_x]
    (mod_ref, oa_ref, ob_ref, gf_ref, gb_ref, r_ref, mf_ref, mb_ref, og_ref,
     gains_ref, g2_ref, wo_ref, w1_ref, w2_ref, y_ref) = refs[n_x:]
    d = x_refs[0].shape[2]
    x = _residual_rows(x_refs, ncb, skip_blocks)
    mod = mod_ref[0, 0]
    ga1 = mod[:, 2 * d:3 * d]
    sh2 = mod[:, 3 * d:4 * d]
    sc2 = mod[:, 4 * d:5 * d]
    ga2 = mod[:, 5 * d:6 * d]
    ones_bd = _group_ones(GROUP_WIDTH, HEAD_DIM)
    gains = gains_ref[...]

    r = r_ref[0]
    o_c = _head_rms(gf_ref[0] + gb_ref[0], gains[0:1], ones_bd) * (r * _sigmoid(r))
    o_d = _sigmoid(og_ref[0]) * _head_rms(mf_ref[0] + mb_ref[0], gains[1:2], ones_bd)
    o_cat = jnp.concatenate([oa_ref[0], ob_ref[0], o_c.astype(BF16), o_d.astype(BF16)], axis=1)
    x = x + ga1 * _dot(o_cat, wo_ref[...])

    ms = jnp.mean(x * x, axis=-1, keepdims=True)
    h = (x * lax.rsqrt(ms + EPS)) * g2_ref[...]
    hb = (h * (1.0 + sc2) + sh2).astype(BF16)
    hid = jnp.maximum(_dot(hb, w1_ref[...]), 0.0)
    hid = (hid * hid).astype(BF16)
    y_ref[0] = x + ga2 * _dot(hid, w2_ref[...])


def _out_mlp(xs, modsel, oa, ob, gf, gb, zc, mf, mb, zd, gains, g2, wo, w1, w2, tm, ncb, skip_blocks):
    b, _, d = xs[0].shape
    t = sum(a.shape[1] for a in xs)
    nblk = t // tm - skip_blocks
    gw = GROUP_WIDTH
    row = lambda col: (lambda bi, j: (bi, j + skip_blocks, col))
    const = lambda bi, j: (0, 0)
    resident = lambda shape: pl.BlockSpec(shape, const, pipeline_mode=pl.Buffered(1))
    return pl.pallas_call(
        functools.partial(_out_kernel, n_x=len(xs), ncb=ncb, skip_blocks=skip_blocks),
        out_shape=jax.ShapeDtypeStruct((b, nblk * tm, d), F32),
        grid=(b, nblk),
        in_specs=_residual_specs(xs, tm, ncb, skip_blocks) + [
                  pl.BlockSpec((1, 1, 1, N_MOD * d),
                               lambda bi, j: (bi, jnp.where(j + skip_blocks >= ncb, 1, 0), 0, 0)),
                  pl.BlockSpec((1, tm, gw), row(0)),
                  pl.BlockSpec((1, tm, gw), row(0)),
                  pl.BlockSpec((1, tm, gw), row(0)),
                  pl.BlockSpec((1, tm, gw), row(0)),
                  pl.BlockSpec((1, tm, gw), row(2)),
                  pl.BlockSpec((1, tm, gw), row(0)),
                  pl.BlockSpec((1, tm, gw), row(0)),
                  pl.BlockSpec((1, tm, gw), row(3)),
                  pl.BlockSpec((2, gw), const),
                  pl.BlockSpec((1, d), const),
                  resident((d, d)),
                  resident((d, 4 * d)),
                  resident((4 * d, d))],
        out_specs=pl.BlockSpec((1, tm, d), lambda bi, j: (bi, j, 0)),
        compiler_params=_cparams(("arbitrary", "arbitrary")),
        name="out_mlp",
    )(*xs, modsel, oa, ob, gf, gb, zc, mf, mb, zd, gains, g2, wo, w1, w2)


def _projection_columns():
    gw, kw = GROUP_WIDTH, KV_HEADS * HEAD_DIM
    splits = (gw, kw, kw, gw, kw, kw, GLA_KW, GLA_KW, gw, gw, 2 * GLA_GATE_RANK,
              gw, gw, gw, gw, 2 * GROUP_HEADS, 2 * GROUP_HEADS)
    offs = np.concatenate([[0], np.cumsum(splits)])
    in_width = int(offs[-1])
    half_split = np.concatenate([np.arange(0, HEAD_DIM, 2), np.arange(1, HEAD_DIM, 2)])
    plain = np.arange(HEAD_DIM)
    cols = []
    mixers = (0, 3)
    for base in mixers:
        cols += [offs[base] + h * HEAD_DIM + half_split for h in ATTN_HEAD_ORDER]
    for base in mixers:
        cols += [offs[base + 1] + g * HEAD_DIM + half_split for g in range(KV_HEADS)]
    for base in mixers:
        cols += [offs[base + 2] + g * HEAD_DIM + plain for g in range(KV_HEADS)]
    cols.append(np.arange(offs[6], offs[11]))
    cols.append(np.full(W_C - (offs[11] - offs[6]), in_width))
    cols.append(np.arange(offs[11], offs[17]))
    cols.append(np.full(W_D - (offs[17] - offs[11]), in_width))
    cols = np.concatenate(cols).astype(np.int32)
    assert cols.shape[0] == W_ALL
    return cols, in_width, half_split


def _rope_tables(seq, lc):
    rows = seq // GRID_W
    row = jnp.repeat(jnp.arange(rows, dtype=F32), GRID_W)
    col = jnp.tile(jnp.arange(GRID_W, dtype=F32), rows)
    n_freq = HEAD_DIM // 4
    inv = ROPE_THETA ** (-jnp.arange(n_freq, dtype=F32) / n_freq)
    ang = jnp.concatenate([row[:, None] * inv, col[:, None] * inv], axis=-1)
    cos, sin = jnp.cos(ang), jnp.sin(ang)
    cos_t = jnp.tile(cos, (1, 4))
    sin_t = jnp.tile(jnp.concatenate([-sin, sin], axis=-1), (1, 2))
    cos_t = jnp.concatenate([jnp.ones((lc, LANES), F32), cos_t], axis=0)
    sin_t = jnp.concatenate([jnp.zeros((lc, LANES), F32), sin_t], axis=0)
    return cos_t, sin_t


def kernel(x, c, ctx, c_ctx, w_mod, b_mod, g_norm1, g_norm2, w_in, g_q_a, g_k_a, g_q_b, g_k_b, sink_b,
           w_gla_gate, b_gla_gate, g_gla_out, b_mlstm_i, b_mlstm_f, g_mlstm_out, w_out, w_mlp1, w_mlp2):
    b, seq, d = x.shape
    lc = ctx.shape[1]
    depth = w_mod.shape[0]
    tm = 256
    assert d == 4 * GROUP_WIDTH and lc % tm == 0 and seq % tm == 0 and seq % GRID_W == 0 and b + 1 <= MOD_ROWS

    cols, in_width, half_split = _projection_columns()
    cos_t, sin_t = _rope_tables(seq, lc)
    attn_rows = np.concatenate([h * HEAD_DIM + np.arange(HEAD_DIM) for h in ATTN_HEAD_ORDER])
    out_rows = np.concatenate([attn_rows, GROUP_WIDTH + attn_rows, np.arange(2 * GROUP_WIDTH, d)])

    cc = jnp.zeros((MOD_ROWS, d), F32).at[0:b].set(c).at[b].set(c_ctx)
    mod_all = _modulation(cc, w_mod, b_mod)

    xs = (ctx, x)
    for l in range(depth):
        modsel = jnp.stack([jnp.broadcast_to(mod_all[l, b], (b, N_MOD * d)), mod_all[l, 0:b]], axis=1)
        modsel = modsel.reshape(b, 2, 1, N_MOD * d)

        w_ext = jnp.concatenate([w_in[l], jnp.zeros((d, 1), F32)], axis=1)
        w_big = jnp.take(w_ext, cols, axis=1).astype(BF16)
        tiled = lambda g, n: jnp.tile(g[half_split], n)
        gains_qk = jnp.stack([tiled(g_q_a[l], GROUP_HEADS), tiled(g_q_b[l], GROUP_HEADS),
                              jnp.concatenate([tiled(g_k_a[l], KV_HEADS), tiled(g_k_b[l], KV_HEADS)])])
        wg = jnp.zeros((LANES, 2 * GLA_KW), F32)
        wg = wg.at[0:GLA_GATE_RANK, 0:GLA_KW].set(w_gla_gate[l, 0])
        wg = wg.at[GLA_GATE_RANK:2 * GLA_GATE_RANK, GLA_KW:].set(w_gla_gate[l, 1]).astype(BF16)
        bg = b_gla_gate[l].reshape(1, 2 * GLA_KW)
        bm = jnp.zeros((1, LANES), F32)
        bm = bm.at[0, 0:2 * GROUP_HEADS].set(b_mlstm_i[l].reshape(-1))
        bm = bm.at[0, 2 * GROUP_HEADS:4 * GROUP_HEADS].set(b_mlstm_f[l].reshape(-1))

        q, k, v, vt, zc, zd = _projection(xs, modsel, g_norm1[l].reshape(1, d), w_big, gains_qk,
                                      cos_t, sin_t, wg, bg, bm, tm, lc // tm)
        oa = _attn_global_t(q, k, vt, lc, tq=256, tk=256)
        ob = _attn_window(q, k, v, sink_b[l], lc, tq=256)
        gf, gb = _gla_scan(zc, lc, tm)
        mf, mb = _mlstm_scan(zd, lc, tm)

        gains_out = jnp.stack([jnp.tile(g_gla_out[l], 4), jnp.tile(g_mlstm_out[l], 4)])
        last = l == depth - 1
        xs = (_out_mlp(xs, modsel, oa, ob, gf, gb, zc, mf, mb, zd, gains_out, g_norm2[l].reshape(1, d),
                       w_out[l][out_rows].astype(BF16), w_mlp1[l].astype(BF16), w_mlp2[l].astype(BF16), tm,
                       ncb=lc // tm, skip_blocks=lc // tm if last else 0),)
    return xs[0]
```

```python
import functools

import numpy as np
import jax
import jax.numpy as jnp
from jax import lax
from jax.experimental import pallas as pl
from jax.experimental.pallas import tpu as pltpu

F32 = jnp.float32
BF16 = jnp.bfloat16

HEAD_DIM = 64
GROUP_HEADS = 4
GROUP_WIDTH = GROUP_HEADS * HEAD_DIM
KV_HEADS = 2
GRID_W = 64
Q_BLOCK = 128
ROPE_THETA = 10000.0
GLA_DK = 32
GLA_KW = GROUP_HEADS * GLA_DK
GLA_GATE_RANK = 16
GLA_GATE_TAU = 16.0
N_MOD = 6
EPS = 1e-6
LOG2E = 1.4426950408889634
NEG = -1e30
CHUNK = 64
LANES = 128
MOD_ROWS = 16
VMEM_LIMIT = 56 * 1024 * 1024

ATTN_HEAD_ORDER = (0, 2, 1, 3)
W_Q = 2 * GROUP_WIDTH
W_KV = 2 * KV_HEADS * HEAD_DIM
W_C = 896
W_D = 1152
W_ALL = W_Q + 2 * W_KV + W_C + W_D


def _cparams(sem):
    return pltpu.CompilerParams(dimension_semantics=sem, vmem_limit_bytes=VMEM_LIMIT)


def _dot(a, b):
    return jnp.dot(a, b, preferred_element_type=F32)


def _dot_nt(a, b):
    return lax.dot_general(a, b, (((1,), (1,)), ((), ())), preferred_element_type=F32)


def _dot_tn(a, b):
    return lax.dot_general(a, b, (((0,), (0,)), ((), ())), preferred_element_type=F32)


def _split3(x):
    hi = x.astype(BF16)
    r1 = x - hi.astype(F32)
    mid = r1.astype(BF16)
    lo = (r1 - mid.astype(F32)).astype(BF16)
    return hi, mid, lo


def _dot01(a01, x):
    hi, mid, lo = _split3(x)
    return _dot(a01, hi) + _dot(a01, mid) + _dot(a01, lo)


def _dot01_r(x, b01):
    hi, mid, lo = _split3(x)
    return _dot(hi, b01) + _dot(mid, b01) + _dot(lo, b01)


def _dot01_r2(x, b01):
    hi = x.astype(BF16)
    lo = (x - hi.astype(F32)).astype(BF16)
    return _dot(hi, b01) + _dot(lo, b01)


def _log_sigmoid(x):
    return jnp.minimum(x, 0.0) - jnp.log(1.0 + jnp.exp(-jnp.abs(x)))


def _sigmoid(x):
    return 1.0 / (1.0 + jnp.exp(-x))


def _iota(shape, dim):
    return lax.broadcasted_iota(jnp.int32, shape, dim)


def _group_ones(n, group):
    return (_iota((n, n), 0) // group == _iota((n, n), 1) // group).astype(BF16)


def _mod_kernel(c_ref, w_ref, b_ref, o_ref):
    c = c_ref[...]
    s = (c * _sigmoid(c)).astype(BF16)
    o_ref[0] = _dot(s, w_ref[0].astype(BF16)) + b_ref[0]


def _modulation(cc, w_mod, b_mod):
    depth, d, n = w_mod.shape
    tn = 1536
    return pl.pallas_call(
        _mod_kernel,
        out_shape=jax.ShapeDtypeStruct((depth, MOD_ROWS, n), F32),
        grid=(depth, n // tn),
        in_specs=[pl.BlockSpec((MOD_ROWS, d), lambda l, j: (0, 0)),
                  pl.BlockSpec((1, d, tn), lambda l, j: (l, 0, j)),
                  pl.BlockSpec((1, 1, tn), lambda l, j: (l, 0, j))],
        out_specs=pl.BlockSpec((1, MOD_ROWS, tn), lambda l, j: (l, 0, j)),
        compiler_params=_cparams(("arbitrary", "arbitrary")),
        name="modulation",
    )(cc, w_mod, b_mod.reshape(depth, 1, n))


def _qk_norm_rope(z, gain, cos, sin, ones_bd, scale):
    ss = _dot01_r2(z * z, ones_bd)
    y = z * lax.rsqrt(ss * (1.0 / HEAD_DIM) + EPS) * gain
    first_half = (_iota((z.shape[0], LANES), 1) % HEAD_DIM) < (HEAD_DIM // 2)
    outs = []
    for cb in range(z.shape[1] // LANES):
        yc = y[:, cb * LANES:(cb + 1) * LANES]
        partner = jnp.where(first_half, pltpu.roll(yc, LANES - HEAD_DIM // 2, 1),
                            pltpu.roll(yc, HEAD_DIM // 2, 1))
        outs.append((yc * cos + partner * sin) * scale)
    return jnp.concatenate(outs, axis=1)


def _residual_specs(xs, tm, ncb, skip_blocks=0):
    d = xs[0].shape[2]
    if len(xs) == 1:
        return [pl.BlockSpec((1, tm, d), lambda bi, j: (bi, j + skip_blocks, 0))]
    return [pl.BlockSpec((1, tm, d), lambda bi, j: (bi, jnp.minimum(j + skip_blocks, ncb - 1), 0)),
            pl.BlockSpec((1, tm, d), lambda bi, j: (bi, jnp.maximum(j + skip_blocks - ncb, 0), 0))]


def _residual_rows(x_refs, ncb, skip_blocks=0):
    if len(x_refs) == 1:
        return x_refs[0][0]
    return jnp.where(pl.program_id(1) + skip_blocks < ncb, x_refs[0][0], x_refs[1][0])


def _proj_kernel(*refs, n_x, ncb):
    x_refs = refs[:n_x]
    (mod_ref, g1_ref, w_ref, gains_ref, cos_ref, sin_ref, wg_ref, bg_ref, bm_ref,
     q_ref, k_ref, vt_ref, zc_ref, zd_ref) = refs[n_x:]
    d = x_refs[0].shape[2]
    tm = x_refs[0].shape[1]
    x = _residual_rows(x_refs, ncb)
    mod = mod_ref[0, 0]
    sh1 = mod[:, 0:d]
    sc1 = mod[:, d:2 * d]
    ms = jnp.mean(x * x, axis=-1, keepdims=True)
    h = (x * lax.rsqrt(ms + EPS)) * g1_ref[...]
    hb = (h * (1.0 + sc1) + sh1).astype(BF16)

    ones_bd = _group_ones(GROUP_WIDTH, HEAD_DIM)
    cos = cos_ref[...]
    sin = sin_ref[...]
    gains = gains_ref[...]
    gw = GROUP_WIDTH

    z = _dot(hb, w_ref[:, 0:W_Q + 2 * W_KV])
    q_scale = HEAD_DIM ** -0.5 * LOG2E
    for n in range(2):
        q = _qk_norm_rope(z[:, n * gw:(n + 1) * gw], gains[n:n + 1], cos, sin, ones_bd, q_scale)
        q_ref[0, :, n * gw:(n + 1) * gw] = q.astype(BF16)
    k_ref[0] = _qk_norm_rope(z[:, W_Q:W_Q + W_KV], gains[2:3], cos, sin, ones_bd, 1.0).astype(BF16)
    vt_ref[0] = z[:, W_Q + W_KV:W_Q + 2 * W_KV].T.astype(BF16)

    base = W_Q + 2 * W_KV
    z = _dot(hb, w_ref[:, base:base + W_C])
    zc_ref[0, :, 0:GLA_KW] = z[:, 0:GLA_KW] * (GLA_DK ** -0.5)
    zc_ref[0, :, GLA_KW:768] = z[:, GLA_KW:768]
    pre = _dot(z[:, 768:896].astype(BF16), wg_ref[...]) + bg_ref[...]
    zc_ref[0, :, 768:1024] = _log_sigmoid(pre) * (1.0 / GLA_GATE_TAU)

    z = _dot(hb, w_ref[:, base + W_C:W_ALL])
    zd_ref[0, :, 0:gw] = z[:, 0:gw]
    zd_ref[0, :, gw:2 * gw] = z[:, gw:2 * gw] * (HEAD_DIM ** -0.5)
    zd_ref[0, :, 2 * gw:4 * gw] = z[:, 2 * gw:4 * gw]
    gates = z[:, 4 * gw:4 * gw + LANES] + bm_ref[...]
    is_input_gate = _iota((tm, LANES), 1) < 2 * GROUP_HEADS
    zd_ref[0, :, 4 * gw:4 * gw + LANES] = jnp.where(is_input_gate, gates, _log_sigmoid(gates))


def _projection(xs, modsel, g1, w_big, gains, cos_t, sin_t, wg, bg, bm, tm, ncb):
    b, _, d = xs[0].shape
    t = sum(a.shape[1] for a in xs)
    nblk = t // tm
    const = lambda bi, j: (0, 0)
    return pl.pallas_call(
        functools.partial(_proj_kernel, n_x=len(xs), ncb=ncb),
        out_shape=(jax.ShapeDtypeStruct((b, t, W_Q), BF16),
                   jax.ShapeDtypeStruct((b, t, W_KV), BF16),
                   jax.ShapeDtypeStruct((b, W_KV, t), BF16),
                   jax.ShapeDtypeStruct((b, t, 1024), F32),
                   jax.ShapeDtypeStruct((b, t, W_D), F32)),
        grid=(b, nblk),
        in_specs=_residual_specs(xs, tm, ncb) + [
                  pl.BlockSpec((1, 1, 1, N_MOD * d), lambda bi, j: (bi, jnp.where(j >= ncb, 1, 0), 0, 0)),
                  pl.BlockSpec((1, d), const),
                  pl.BlockSpec((d, W_ALL), const),
                  pl.BlockSpec((3, GROUP_WIDTH), const),
                  pl.BlockSpec((tm, LANES), lambda bi, j: (j, 0)),
                  pl.BlockSpec((tm, LANES), lambda bi, j: (j, 0)),
                  pl.BlockSpec((LANES, 2 * GLA_KW), const),
                  pl.BlockSpec((1, 2 * GLA_KW), const),
                  pl.BlockSpec((1, LANES), const)],
        out_specs=(pl.BlockSpec((1, tm, W_Q), lambda bi, j: (bi, j, 0)),
                   pl.BlockSpec((1, tm, W_KV), lambda bi, j: (bi, j, 0)),
                   pl.BlockSpec((1, W_KV, tm), lambda bi, j: (bi, 0, j)),
                   pl.BlockSpec((1, tm, 1024), lambda bi, j: (bi, j, 0)),
                   pl.BlockSpec((1, tm, W_D), lambda bi, j: (bi, j, 0))),
        compiler_params=_cparams(("arbitrary", "arbitrary")),
        name="projection",
    )(*xs, modsel, g1, w_big, gains, cos_t, sin_t, wg, bg, bm)


ONES_ROWS = 16


def _masked_query_tiles(q_ref):
    low = _iota((q_ref.shape[1], LANES), 1) < HEAD_DIM
    tiles = []
    for n in range(GROUP_WIDTH // LANES):
        q128 = q_ref[0, :, n * LANES:(n + 1) * LANES]
        zero = jnp.zeros_like(q128)
        tiles += [jnp.where(low, q128, zero), jnp.where(low, zero, q128)]
    return tiles


def _heads_to_rows(o_t):
    halves = [jnp.concatenate(o_t[2 * n:2 * n + 2], axis=0).T for n in range(len(o_t) // 2)]
    return jnp.concatenate(halves, axis=1)


def _window_stages(sink_ref, q_ref, k_refs, v_refs, o_ref, nctx, nblk):
    tq = q_ref.shape[1]
    halo = Q_BLOCK
    n_win = tq + 2 * halo
    n_keys = n_win + k_refs[3].shape[1]
    j = pl.program_id(1)
    is_lat = j >= nctx
    off = 4 * tq
    lo_prev = jnp.where(j > nctx, -halo, off)
    lo_cur = jnp.where(is_lat, -halo, off)
    lo_next = jnp.where(jnp.logical_and(is_lat, j + 1 <= nblk - 1), -halo, off)
    row = _iota((n_keys, tq), 0)
    diff = row - halo - _iota((n_keys, tq), 1)
    lo = jnp.where(row < halo, lo_prev, jnp.where(row < halo + tq, lo_cur, lo_next))
    ok = jnp.logical_or(row >= n_win, jnp.logical_and(diff >= lo, diff <= halo))
    keys = jnp.concatenate([r[0] for r in k_refs], axis=0)
    v_t = jnp.concatenate([r[0] for r in v_refs], axis=1)
    ones = jnp.ones((ONES_ROWS, n_keys), BF16)
    tiles = _masked_query_tiles(q_ref)
    yield
    scores = [jnp.where(ok, _dot_nt(keys, qp), NEG) for qp in tiles]
    yield
    o_t = []
    for c, s_t in enumerate(scores):
        sink = sink_ref[ATTN_HEAD_ORDER[c]] * LOG2E
        m = jnp.maximum(jnp.max(s_t, axis=0, keepdims=True), sink)
        kv = c % KV_HEADS
        vx = jnp.concatenate([v_t[kv * HEAD_DIM:(kv + 1) * HEAD_DIM], ones], axis=0)
        acc = _dot(vx, jnp.exp2(s_t - m).astype(BF16))
        o_t.append(acc[0:HEAD_DIM] / (acc[HEAD_DIM:HEAD_DIM + 1] + jnp.exp2(sink - m)))
        yield
    o_ref[0] = _heads_to_rows(o_t).astype(BF16)


def _attention_kernel(sink_ref, qa_ref, qb_ref, ka_ref, vta_ref, kp_ref, kc_ref, kn_ref, kx_ref,
                      vp_ref, vc_ref, vn_ref, vx_ref, oa_ref, ob_ref, *, lc, tk, nk, nctx, nblk):
    tq = qa_ref.shape[1]
    n_tiles = GROUP_HEADS
    tiles = _masked_query_tiles(qa_ref)
    ones = jnp.ones((ONES_ROWS, tk), BF16)

    def window():
        return _window_stages(sink_ref, qb_ref, (kp_ref, kc_ref, kn_ref, kx_ref),
                              (vp_ref, vc_ref, vn_ref, vx_ref), ob_ref, nctx, nblk)

    def scores(rows):
        k = ka_ref[0, rows, :]
        return [_dot_nt(k, qp) for qp in tiles]

    def chunk(s_t, rows, m, accs):
        m_out, acc_out = [], []
        for c in range(n_tiles):
            col_max = jnp.max(s_t[c], axis=0, keepdims=True)
            m_new = col_max if m is None else jnp.maximum(m[c], col_max)
            p_t = jnp.exp2(s_t[c] - m_new).astype(BF16)
            kv = c % KV_HEADS
            vx = jnp.concatenate([vta_ref[0, kv * HEAD_DIM:(kv + 1) * HEAD_DIM, rows], ones], axis=0)
            upd = _dot(vx, p_t)
            m_out.append(m_new)
            acc_out.append(upd if m is None else jnp.exp2(m[c] - m_new) * accs[c] + upd)
        return m_out, acc_out

    def key_chunks(first, count, m, accs, side=None):
        rows = [slice((first + c) * tk, (first + c + 1) * tk) for c in range(count)]
        s_next = scores(rows[0])
        for c in range(count):
            s_t = s_next
            if c + 1 < count:
                s_next = scores(rows[c + 1])
            if side is not None:
                next(side, None)
            m, accs = chunk(s_t, rows[c], m, accs)
        return m, accs

    n_ctx = lc // tk
    m, accs = key_chunks(0, n_ctx, None, None)

    def finish(accs):
        return _heads_to_rows([a[0:HEAD_DIM] / a[HEAD_DIM:HEAD_DIM + 1] for a in accs])

    def latent_queries():
        side = window()
        out = finish(key_chunks(n_ctx, nk, m, accs, side)[1])
        for _ in side:
            pass
        return out

    def context_queries():
        for _ in window():
            pass
        return finish(accs)

    o = lax.cond(pl.program_id(1) * tq >= lc, latent_queries, context_queries)
    oa_ref[0] = o.astype(BF16)


def _attention(q, k, vt, sink, lc, tq, tk):
    b, t, _ = q.shape
    nblk = t // tq
    nctx = lc // tq
    nk = (t - lc) // tk
    assert lc % tk == 0 and (t - lc) % tk == 0
    per = tq // Q_BLOCK
    prev_blk = lambda j: jnp.maximum(j * per - 1, 0)
    next_blk = lambda j: jnp.minimum((j + 1) * per, nblk * per - 1)
    k_specs = [pl.BlockSpec((1, Q_BLOCK, LANES), lambda bi, j: (bi, prev_blk(j), 1)),
               pl.BlockSpec((1, tq, LANES), lambda bi, j: (bi, j, 1)),
               pl.BlockSpec((1, Q_BLOCK, LANES), lambda bi, j: (bi, next_blk(j), 1)),
               pl.BlockSpec((1, lc, LANES), lambda bi, j: (bi, 0, 1))]
    v_specs = [pl.BlockSpec((1, LANES, Q_BLOCK), lambda bi, j: (bi, 1, prev_blk(j))),
               pl.BlockSpec((1, LANES, tq), lambda bi, j: (bi, 1, j)),
               pl.BlockSpec((1, LANES, Q_BLOCK), lambda bi, j: (bi, 1, next_blk(j))),
               pl.BlockSpec((1, LANES, lc), lambda bi, j: (bi, 1, 0))]
    out = jax.ShapeDtypeStruct((b, t, GROUP_WIDTH), BF16)
    q_spec = lambda col: pl.BlockSpec((1, tq, GROUP_WIDTH), lambda bi, j: (bi, j, col))
    return pl.pallas_call(
        functools.partial(_attention_kernel, lc=lc, tk=tk, nk=nk, nctx=nctx, nblk=nblk),
        out_shape=(out, out),
        grid=(b, nblk),
        in_specs=[pl.BlockSpec(memory_space=pltpu.SMEM), q_spec(0), q_spec(1),
                  pl.BlockSpec((1, t, LANES), lambda bi, j: (bi, 0, 0)),
                  pl.BlockSpec((1, LANES, t), lambda bi, j: (bi, 0, 0))] + k_specs + v_specs,
        out_specs=(q_spec(0), q_spec(0)),
        compiler_params=_cparams(("arbitrary", "arbitrary")),
        name="attention",
    )(sink, q, q, k, vt, k, k, k, k, vt, vt, vt, vt)


def _chunk_tri(rows, reverse):
    t = _iota((rows, rows), 0)
    u = _iota((rows, rows), 1)
    same = t // CHUNK == u // CHUNK
    return jnp.logical_and(same, (u >= t) if reverse else (u <= t)).astype(BF16)


def _tile4(x):
    return jnp.concatenate([x, x, x, x], axis=0)


def _rev_block(i, ncb, nblk):
    return jnp.where(i < ncb, ncb - 1 - i, nblk - 1 - (i - ncb))


class _Item:
    pass


def _scan_items(directions, nchunk):
    items = []
    for pos in range(nchunk):
        for d in directions:
            it = _Item()
            it.d = d
            it.c = nchunk - 1 - pos if d.reverse else pos
            it.sl = slice(it.c * CHUNK, (it.c + 1) * CHUNK)
            items.append(it)
    return items


def _gla_stages(qkv_f_ref, la_f_ref, qkv_b_ref, la_b_ref, of_ref, ob_ref, sf_ref, sb_ref):
    @pl.when(pl.program_id(1) == 0)
    def _():
        sf_ref[...] = jnp.zeros_like(sf_ref)
        sb_ref[...] = jnp.zeros_like(sb_ref)

    rows = qkv_f_ref.shape[1]
    same_kd = _iota((GROUP_WIDTH, GLA_KW), 0) // CHUNK == _iota((GROUP_WIDTH, GLA_KW), 1) // GLA_DK
    same_kv = _iota((GROUP_WIDTH, GROUP_WIDTH), 0) // CHUNK == _iota((GROUP_WIDTH, GROUP_WIDTH), 1) // HEAD_DIM
    t_pos = _iota((CHUNK, GROUP_WIDTH), 0)
    s_pos = _iota((CHUNK, GROUP_WIDTH), 1) % CHUNK

    directions = []
    for qkv_ref, la_ref, o_ref, s_ref, reverse in ((qkv_f_ref, la_f_ref, of_ref, sf_ref, False),
                                                   (qkv_b_ref, la_b_ref, ob_ref, sb_ref, True)):
        d = _Item()
        d.reverse, d.o_ref, d.s_ref = reverse, o_ref, s_ref
        d.b = _dot01(_chunk_tri(rows, reverse), la_ref[0])
        d.q = qkv_ref[0, :, 0:GLA_KW]
        d.k = qkv_ref[0, :, GLA_KW:2 * GLA_KW]
        d.v = qkv_ref[0, :, 2 * GLA_KW:2 * GLA_KW + GROUP_WIDTH].astype(BF16)
        d.causal = (s_pos >= t_pos) if reverse else (s_pos <= t_pos)
        d.end = 0 if reverse else CHUNK - 1
        d.state = s_ref[...]
        directions.append(d)
    items = _scan_items(directions, rows // CHUNK)

    yield
    for it in items:
        d = it.d
        bc = d.b[it.sl]
        b_end = bc[d.end:d.end + 1]
        b_mid = bc[CHUNK // 2:CHUNK // 2 + 1]
        qc, kc = d.q[it.sl], d.k[it.sl]
        it.vc = d.v[it.sl]
        it.q_inter = (qc * jnp.exp(bc)).astype(BF16)
        it.q_intra = (qc * jnp.exp(bc - b_mid)).astype(BF16)
        k_intra = (kc * jnp.exp(b_mid - bc)).astype(BF16)
        it.k_state = (kc * jnp.exp(b_end - bc)).astype(BF16)
        it.decay = jnp.exp(b_end)
        it.k_bd = jnp.where(same_kd, _tile4(k_intra), jnp.zeros((), BF16))
        it.v_bd = jnp.where(same_kv, _tile4(it.vc), jnp.zeros((), BF16))
    yield
    for it in items:
        it.scores = jnp.where(it.d.causal, _dot_nt(it.q_intra, it.k_bd), 0.0).astype(BF16)
    yield
    for it in items:
        it.o_intra = _dot(it.scores, it.v_bd)
        it.update = jnp.where(same_kd, _dot_tn(it.vc, it.k_state), 0.0)
    yield
    for it in items:
        it.state_in = it.d.state
        it.d.state = it.d.state * it.decay + it.update
    yield
    for it in items:
        it.d.o_ref[0, it.sl, :] = it.o_intra + _dot_nt(it.q_inter, it.state_in.astype(BF16))
    yield
    for d in directions:
        d.s_ref[...] = d.state


def _expand_heads(g, base, rows):
    hid = _iota((rows, GROUP_WIDTH), 1) // HEAD_DIM
    cols = [jnp.broadcast_to(g[:, base + h:base + h + 1], (rows, GROUP_WIDTH)) for h in range(GROUP_HEADS)]
    return jnp.where(hid == 0, cols[0], jnp.where(hid == 1, cols[1], jnp.where(hid == 2, cols[2], cols[3])))


def _mlstm_stages(qkv_f_ref, gate_f_ref, qkv_b_ref, gate_b_ref, of_ref, ob_ref,
                  cf_ref, nf_ref, mf_ref, cb_ref, nb_ref, mb_ref):
    @pl.when(pl.program_id(1) == 0)
    def _():
        for r in (cf_ref, nf_ref, mf_ref, cb_ref, nb_ref, mb_ref):
            r[...] = jnp.zeros_like(r)

    rows = qkv_f_ref.shape[1]
    gw = GROUP_WIDTH
    same_head = _iota((gw, gw), 0) // HEAD_DIM == _iota((gw, gw), 1) // HEAD_DIM
    ones_bd = same_head.astype(BF16)
    hid = _iota((CHUNK, gw), 1) // HEAD_DIM
    t_pos = _iota((CHUNK, gw), 0)
    s_pos = _iota((CHUNK, gw), 1) % CHUNK
    diag = (s_pos == t_pos).astype(F32)
    ones_cc = jnp.ones((CHUNK, CHUNK), BF16)

    directions = []
    for index, (qkv_ref, gate_ref, o_ref, c_ref, n_ref, m_ref) in enumerate(
            ((qkv_f_ref, gate_f_ref, of_ref, cf_ref, nf_ref, mf_ref),
             (qkv_b_ref, gate_b_ref, ob_ref, cb_ref, nb_ref, mb_ref))):
        d = _Item()
        d.reverse = index == 1
        d.o_ref, d.c_ref, d.n_ref, d.m_ref = o_ref, c_ref, n_ref, m_ref
        g = gate_ref[0]
        i_col = GROUP_HEADS * index
        f_col = 2 * GROUP_HEADS + GROUP_HEADS * index
        f_all = _dot01(_chunk_tri(rows, d.reverse), g)
        d.ig = _expand_heads(g, i_col, rows)
        d.f_cum = _expand_heads(f_all, f_col, rows)
        d.q = qkv_ref[0, :, 0:gw]
        d.k = qkv_ref[0, :, gw:2 * gw]
        d.v = qkv_ref[0, :, 2 * gw:3 * gw].astype(BF16)
        d.causal = (s_pos >= t_pos) if d.reverse else (s_pos <= t_pos)
        d.end = 0 if d.reverse else CHUNK - 1
        d.c_state = c_ref[...]
        d.n_state = n_ref[...]
        d.m_state = m_ref[...]
        directions.append(d)
    items = _scan_items(directions, rows // CHUNK)

    yield
    for it in items:
        d = it.d
        it.fc, it.ic = d.f_cum[it.sl], d.ig[it.sl]
        it.qc, it.kc, it.vc = d.q[it.sl], d.k[it.sl], d.v[it.sl]
        it.qb = it.qc.astype(BF16)
        it.key_term = _dot01(ones_cc, (it.fc - it.ic) * diag)
    yield
    for it in items:
        it.logw = jnp.where(it.d.causal, it.fc - it.key_term, NEG)
        head_max = [jnp.max(it.logw[:, h * HEAD_DIM:(h + 1) * HEAD_DIM], axis=-1, keepdims=True)
                    for h in range(GROUP_HEADS)]
        it.row_max = jnp.where(hid < 2, jnp.where(hid == 0, head_max[0], head_max[1]),
                               jnp.where(hid == 2, head_max[2], head_max[3]))
        it.k_bd = jnp.where(same_head, _tile4(it.kc.astype(BF16)), jnp.zeros((), BF16))
        it.v_bd = jnp.where(same_head, _tile4(it.vc), jnp.zeros((), BF16))
    yield
    for it in items:
        d = it.d
        it.log_inter = it.fc + d.m_state
        it.m_t = jnp.maximum(it.log_inter, it.row_max)
        d.m_state = it.m_new = it.m_t[d.end:d.end + 1]
    yield
    for it in items:
        end = it.d.end
        it.w = jnp.exp(it.logw - it.m_t)
        it.w_inter = jnp.exp(it.log_inter - it.m_t)
        it.decay = it.w_inter[end:end + 1]
        it.k_end = it.kc * jnp.exp(it.fc[end:end + 1] - it.fc + it.ic - it.m_new)
    yield
    for it in items:
        it.qk = _dot_nt(it.qb, it.k_bd) * it.w
    yield
    for it in items:
        it.num = _dot(it.qk.astype(BF16), it.v_bd)
        it.den = _dot01_r(it.qk, ones_bd)
        it.update = jnp.where(same_head, _dot_tn(it.k_end.astype(BF16), it.vc), 0.0)
    yield
    for it in items:
        d = it.d
        it.c_in, it.n_in = d.c_state, d.n_state
        d.c_state = d.c_state * it.decay + it.update
        d.n_state = d.n_state * it.decay + jnp.sum(it.k_end, axis=0, keepdims=True)
    yield
    for it in items:
        num = it.num + it.w_inter * _dot(it.qb, it.c_in.astype(BF16))
        den = it.den + it.w_inter * _dot((it.qc * it.n_in).astype(BF16), ones_bd)
        it.d.o_ref[0, it.sl, :] = num / jnp.maximum(jnp.abs(den), jnp.exp(-it.m_t))
    yield
    for d in directions:
        d.c_ref[...] = d.c_state
        d.n_ref[...] = d.n_state
        d.m_ref[...] = d.m_state


def _scans_kernel(*refs):
    gla_in, mlstm_in = refs[0:4], refs[4:8]
    gla_out, mlstm_out = refs[8:10], refs[10:12]
    gla_state, mlstm_state = refs[12:14], refs[14:20]
    running = [_gla_stages(*gla_in, *gla_out, *gla_state), _mlstm_stages(*mlstm_in, *mlstm_out, *mlstm_state)]
    while running:
        running = [g for g in running if next(g, StopIteration) is not StopIteration]


def _scans(zc, zd, lc, rows):
    b, t, _ = zc.shape
    nblk = t // rows
    ncb = lc // rows
    gw = GROUP_WIDTH
    fwd = lambda col: (lambda bi, i: (bi, i, col))
    rev = lambda col: (lambda bi, i: (bi, _rev_block(i, ncb, nblk), col))
    gla_qkv_w = 2 * GLA_KW + gw
    gate_col = 4 * gw // LANES
    mlstm_state = [pltpu.VMEM((gw, gw), F32), pltpu.VMEM((1, gw), F32), pltpu.VMEM((1, gw), F32)]
    out = jax.ShapeDtypeStruct((b, t, gw), F32)
    return pl.pallas_call(
        _scans_kernel,
        out_shape=(out, out, out, out),
        grid=(b, nblk),
        in_specs=[pl.BlockSpec((1, rows, gla_qkv_w), fwd(0)),
                  pl.BlockSpec((1, rows, GLA_KW), fwd(6)),
                  pl.BlockSpec((1, rows, gla_qkv_w), rev(0)),
                  pl.BlockSpec((1, rows, GLA_KW), rev(7)),
                  pl.BlockSpec((1, rows, 3 * gw), fwd(0)),
                  pl.BlockSpec((1, rows, LANES), fwd(gate_col)),
                  pl.BlockSpec((1, rows, 3 * gw), rev(0)),
                  pl.BlockSpec((1, rows, LANES), rev(gate_col))],
        out_specs=(pl.BlockSpec((1, rows, gw), fwd(0)), pl.BlockSpec((1, rows, gw), rev(0)),
                   pl.BlockSpec((1, rows, gw), fwd(0)), pl.BlockSpec((1, rows, gw), rev(0))),
        scratch_shapes=[pltpu.VMEM((gw, GLA_KW), F32), pltpu.VMEM((gw, GLA_KW), F32)] + mlstm_state + mlstm_state,
        compiler_params=_cparams(("arbitrary", "arbitrary")),
        name="scans",
    )(zc, zc, zc, zc, zd, zd, zd, zd)


def _head_rms(o, gain, ones_bd):
    ss = _dot01_r2(o * o, ones_bd)
    return o * lax.rsqrt(ss * (1.0 / HEAD_DIM) + EPS) * gain


def _out_kernel(*refs, n_x, ncb, skip_blocks):
    x_refs = refs[:n_x]
    (mod_ref, oa_ref, ob_ref, gf_ref, gb_ref, r_ref, mf_ref, mb_ref, og_ref,
     gains_ref, g2_ref, wo_ref, w1_ref, w2_ref, y_ref) = refs[n_x:]
    d = x_refs[0].shape[2]
    x = _residual_rows(x_refs, ncb, skip_blocks)
    mod = mod_ref[0, 0]
    ga1 = mod[:, 2 * d:3 * d]
    sh2 = mod[:, 3 * d:4 * d]
    sc2 = mod[:, 4 * d:5 * d]
    ga2 = mod[:, 5 * d:6 * d]
    ones_bd = _group_ones(GROUP_WIDTH, HEAD_DIM)
    gains = gains_ref[...]

    r = r_ref[0]
    o_c = _head_rms(gf_ref[0] + gb_ref[0], gains[0:1], ones_bd) * (r * _sigmoid(r))
    o_d = _sigmoid(og_ref[0]) * _head_rms(mf_ref[0] + mb_ref[0], gains[1:2], ones_bd)
    o_cat = jnp.concatenate([oa_ref[0], ob_ref[0], o_c.astype(BF16), o_d.astype(BF16)], axis=1)
    x = x + ga1 * _dot(o_cat, wo_ref[...])

    ms = jnp.mean(x * x, axis=-1, keepdims=True)
    h = (x * lax.rsqrt(ms + EPS)) * g2_ref[...]
    hb = (h * (1.0 + sc2) + sh2).astype(BF16)
    hid = jnp.maximum(_dot(hb, w1_ref[...]), 0.0)
    hid = (hid * hid).astype(BF16)
    y_ref[0] = x + ga2 * _dot(hid, w2_ref[...])


def _out_mlp(xs, modsel, oa, ob, gf, gb, zc, mf, mb, zd, gains, g2, wo, w1, w2, tm, ncb, skip_blocks):
    b, _, d = xs[0].shape
    t = sum(a.shape[1] for a in xs)
    nblk = t // tm - skip_blocks
    gw = GROUP_WIDTH
    row = lambda col: (lambda bi, j: (bi, j + skip_blocks, col))
    const = lambda bi, j: (0, 0)
    resident = lambda shape: pl.BlockSpec(shape, const, pipeline_mode=pl.Buffered(1))
    return pl.pallas_call(
        functools.partial(_out_kernel, n_x=len(xs), ncb=ncb, skip_blocks=skip_blocks),
        out_shape=jax.ShapeDtypeStruct((b, nblk * tm, d), F32),
        grid=(b, nblk),
        in_specs=_residual_specs(xs, tm, ncb, skip_blocks) + [
                  pl.BlockSpec((1, 1, 1, N_MOD * d),
                               lambda bi, j: (bi, jnp.where(j + skip_blocks >= ncb, 1, 0), 0, 0)),
                  pl.BlockSpec((1, tm, gw), row(0)),
                  pl.BlockSpec((1, tm, gw), row(0)),
                  pl.BlockSpec((1, tm, gw), row(0)),
                  pl.BlockSpec((1, tm, gw), row(0)),
                  pl.BlockSpec((1, tm, gw), row(2)),
                  pl.BlockSpec((1, tm, gw), row(0)),
                  pl.BlockSpec((1, tm, gw), row(0)),
                  pl.BlockSpec((1, tm, gw), row(3)),
                  pl.BlockSpec((2, gw), const),
                  pl.BlockSpec((1, d), const),
                  resident((d, d)),
                  resident((d, 4 * d)),
                  resident((4 * d, d))],
        out_specs=pl.BlockSpec((1, tm, d), lambda bi, j: (bi, j, 0)),
        compiler_params=_cparams(("arbitrary", "arbitrary")),
        name="out_mlp",
    )(*xs, modsel, oa, ob, gf, gb, zc, mf, mb, zd, gains, g2, wo, w1, w2)


def _projection_columns():
    gw, kw = GROUP_WIDTH, KV_HEADS * HEAD_DIM
    splits = (gw, kw, kw, gw, kw, kw, GLA_KW, GLA_KW, gw, gw, 2 * GLA_GATE_RANK,
              gw, gw, gw, gw, 2 * GROUP_HEADS, 2 * GROUP_HEADS)
    offs = np.concatenate([[0], np.cumsum(splits)])
    in_width = int(offs[-1])
    half_split = np.concatenate([np.arange(0, HEAD_DIM, 2), np.arange(1, HEAD_DIM, 2)])
    plain = np.arange(HEAD_DIM)
    cols = []
    mixers = (0, 3)
    for base in mixers:
        cols += [offs[base] + h * HEAD_DIM + half_split for h in ATTN_HEAD_ORDER]
    for base in mixers:
        cols += [offs[base + 1] + g * HEAD_DIM + half_split for g in range(KV_HEADS)]
    for base in mixers:
        cols += [offs[base + 2] + g * HEAD_DIM + plain for g in range(KV_HEADS)]
    cols.append(np.arange(offs[6], offs[11]))
    cols.append(np.full(W_C - (offs[11] - offs[6]), in_width))
    cols.append(np.arange(offs[11], offs[17]))
    cols.append(np.full(W_D - (offs[17] - offs[11]), in_width))
    cols = np.concatenate(cols).astype(np.int32)
    assert cols.shape[0] == W_ALL
    return cols, in_width, half_split


def _rope_tables(seq, lc):
    rows = seq // GRID_W
    row = jnp.repeat(jnp.arange(rows, dtype=F32), GRID_W)
    col = jnp.tile(jnp.arange(GRID_W, dtype=F32), rows)
    n_freq = HEAD_DIM // 4
    inv = ROPE_THETA ** (-jnp.arange(n_freq, dtype=F32) / n_freq)
    ang = jnp.concatenate([row[:, None] * inv, col[:, None] * inv], axis=-1)
    cos, sin = jnp.cos(ang), jnp.sin(ang)
    cos_t = jnp.tile(cos, (1, 4))
    sin_t = jnp.tile(jnp.concatenate([-sin, sin], axis=-1), (1, 2))
    cos_t = jnp.concatenate([jnp.ones((lc, LANES), F32), cos_t], axis=0)
    sin_t = jnp.concatenate([jnp.zeros((lc, LANES), F32), sin_t], axis=0)
    return cos_t, sin_t


def kernel(x, c, ctx, c_ctx, w_mod, b_mod, g_norm1, g_norm2, w_in, g_q_a, g_k_a, g_q_b, g_k_b, sink_b,
           w_gla_gate, b_gla_gate, g_gla_out, b_mlstm_i, b_mlstm_f, g_mlstm_out, w_out, w_mlp1, w_mlp2):
    b, seq, d = x.shape
    lc = ctx.shape[1]
    depth = w_mod.shape[0]
    tm = 256
    assert d == 4 * GROUP_WIDTH and lc % tm == 0 and seq % tm == 0 and seq % GRID_W == 0 and b + 1 <= MOD_ROWS

    cols, in_width, half_split = _projection_columns()
    cos_t, sin_t = _rope_tables(seq, lc)
    attn_rows = np.concatenate([h * HEAD_DIM + np.arange(HEAD_DIM) for h in ATTN_HEAD_ORDER])
    out_rows = np.concatenate([attn_rows, GROUP_WIDTH + attn_rows, np.arange(2 * GROUP_WIDTH, d)])

    cc = jnp.zeros((MOD_ROWS, d), F32).at[0:b].set(c).at[b].set(c_ctx)
    mod_all = _modulation(cc, w_mod, b_mod)

    xs = (ctx, x)
    for l in range(depth):
        modsel = jnp.stack([jnp.broadcast_to(mod_all[l, b], (b, N_MOD * d)), mod_all[l, 0:b]], axis=1)
        modsel = modsel.reshape(b, 2, 1, N_MOD * d)

        w_ext = jnp.concatenate([w_in[l], jnp.zeros((d, 1), F32)], axis=1)
        w_big = jnp.take(w_ext, cols, axis=1).astype(BF16)
        tiled = lambda g, n: jnp.tile(g[half_split], n)
        gains_qk = jnp.stack([tiled(g_q_a[l], GROUP_HEADS), tiled(g_q_b[l], GROUP_HEADS),
                              jnp.concatenate([tiled(g_k_a[l], KV_HEADS), tiled(g_k_b[l], KV_HEADS)])])
        wg = jnp.zeros((LANES, 2 * GLA_KW), F32)
        wg = wg.at[0:GLA_GATE_RANK, 0:GLA_KW].set(w_gla_gate[l, 0])
        wg = wg.at[GLA_GATE_RANK:2 * GLA_GATE_RANK, GLA_KW:].set(w_gla_gate[l, 1]).astype(BF16)
        bg = b_gla_gate[l].reshape(1, 2 * GLA_KW)
        bm = jnp.zeros((1, LANES), F32)
        bm = bm.at[0, 0:2 * GROUP_HEADS].set(b_mlstm_i[l].reshape(-1))
        bm = bm.at[0, 2 * GROUP_HEADS:4 * GROUP_HEADS].set(b_mlstm_f[l].reshape(-1))

        q, k, vt, zc, zd = _projection(xs, modsel, g_norm1[l].reshape(1, d), w_big, gains_qk,
                                      cos_t, sin_t, wg, bg, bm, tm, lc // tm)
        oa, ob = _attention(q, k, vt, sink_b[l], lc, tq=256, tk=256)
        gf, gb, mf, mb = _scans(zc, zd, lc, tm)

        gains_out = jnp.stack([jnp.tile(g_gla_out[l], 4), jnp.tile(g_mlstm_out[l], 4)])
        last = l == depth - 1
        xs = (_out_mlp(xs, modsel, oa, ob, gf, gb, zc, mf, mb, zd, gains_out, g_norm2[l].reshape(1, d),
                       w_out[l][out_rows].astype(BF16), w_mlp1[l].astype(BF16), w_mlp2[l].astype(BF16), tm,
                       ncb=lc // tm, skip_blocks=lc // tm if last else 0),)
    return xs[0]
```

```python
import functools

import numpy as np
import jax
import jax.numpy as jnp
from jax import lax
from jax.experimental import pallas as pl
from jax.experimental.pallas import tpu as pltpu

F32 = jnp.float32
BF16 = jnp.bfloat16

HEAD_DIM = 64
GROUP_HEADS = 4
GROUP_WIDTH = GROUP_HEADS * HEAD_DIM
KV_HEADS = 2
GRID_W = 64
Q_BLOCK = 128
ROPE_THETA = 10000.0
GLA_DK = 32
GLA_KW = GROUP_HEADS * GLA_DK
GLA_GATE_RANK = 16
GLA_GATE_TAU = 16.0
N_MOD = 6
EPS = 1e-6
LOG2E = 1.4426950408889634
NEG = -1e30
CHUNK = 64
LANES = 128
MOD_ROWS = 16
VMEM_LIMIT = 56 * 1024 * 1024

ATTN_HEAD_ORDER = (0, 2, 1, 3)
W_Q = 2 * GROUP_WIDTH
W_KV = 2 * KV_HEADS * HEAD_DIM
W_C = 896
W_D = 1152
W_ALL = W_Q + 2 * W_KV + W_C + W_D


NB = 2
ROW_TILE = 256
KEY_CHUNK = 256


def _per_batch_row(body, layout):
    def kernel(*refs):
        assert len(refs) == len(layout)
        for i in range(NB):
            body(*[r.at[i:i + 1] if how == "b" else r for r, how in zip(refs, layout)])
    return kernel


def _cparams(sem):
    return pltpu.CompilerParams(dimension_semantics=sem, vmem_limit_bytes=VMEM_LIMIT)


def _dot(a, b):
    return jnp.dot(a, b, preferred_element_type=F32)


def _dot_nt(a, b):
    return lax.dot_general(a, b, (((1,), (1,)), ((), ())), preferred_element_type=F32)


def _dot_tn(a, b):
    return lax.dot_general(a, b, (((0,), (0,)), ((), ())), preferred_element_type=F32)


def _split3(x):
    hi = x.astype(BF16)
    r1 = x - hi.astype(F32)
    mid = r1.astype(BF16)
    lo = (r1 - mid.astype(F32)).astype(BF16)
    return hi, mid, lo


def _dot01(a01, x):
    hi, mid, lo = _split3(x)
    return _dot(a01, hi) + _dot(a01, mid) + _dot(a01, lo)


def _dot01_r(x, b01):
    hi, mid, lo = _split3(x)
    return _dot(hi, b01) + _dot(mid, b01) + _dot(lo, b01)


def _dot01_r2(x, b01):
    hi = x.astype(BF16)
    lo = (x - hi.astype(F32)).astype(BF16)
    return _dot(hi, b01) + _dot(lo, b01)


def _log_sigmoid(x):
    return jnp.minimum(x, 0.0) - jnp.log(1.0 + jnp.exp(-jnp.abs(x)))


def _sigmoid(x):
    return 1.0 / (1.0 + jnp.exp(-x))


def _iota(shape, dim):
    return lax.broadcasted_iota(jnp.int32, shape, dim)


def _group_ones(n, group):
    return (_iota((n, n), 0) // group == _iota((n, n), 1) // group).astype(BF16)


def _mod_kernel(c_ref, w_ref, b_ref, o_ref):
    c = c_ref[...]
    s = (c * _sigmoid(c)).astype(BF16)
    o_ref[0] = _dot(s, w_ref[0].astype(BF16)) + b_ref[0]


def _modulation(cc, w_mod, b_mod):
    depth, d, n = w_mod.shape
    tn = 1536
    return pl.pallas_call(
        _mod_kernel,
        out_shape=jax.ShapeDtypeStruct((depth, MOD_ROWS, n), F32),
        grid=(depth, n // tn),
        in_specs=[pl.BlockSpec((MOD_ROWS, d), lambda l, j: (0, 0)),
                  pl.BlockSpec((1, d, tn), lambda l, j: (l, 0, j)),
                  pl.BlockSpec((1, 1, tn), lambda l, j: (l, 0, j))],
        out_specs=pl.BlockSpec((1, MOD_ROWS, tn), lambda l, j: (l, 0, j)),
        compiler_params=_cparams(("arbitrary", "arbitrary")),
        name="modulation",
    )(cc, w_mod, b_mod.reshape(depth, 1, n))


def _qk_norm_rope(z, gain, cos, sin, ones_bd, scale):
    ss = _dot01_r2(z * z, ones_bd)
    y = z * lax.rsqrt(ss * (1.0 / HEAD_DIM) + EPS) * gain
    first_half = (_iota((z.shape[0], LANES), 1) % HEAD_DIM) < (HEAD_DIM // 2)
    outs = []
    for cb in range(z.shape[1] // LANES):
        yc = y[:, cb * LANES:(cb + 1) * LANES]
        partner = jnp.where(first_half, pltpu.roll(yc, LANES - HEAD_DIM // 2, 1),
                            pltpu.roll(yc, HEAD_DIM // 2, 1))
        outs.append((yc * cos + partner * sin) * scale)
    return jnp.concatenate(outs, axis=1)


def _residual_specs(xs, tm, ncb, skip_blocks=0):
    d = xs[0].shape[2]
    if len(xs) == 1:
        return [pl.BlockSpec((NB, tm, d), lambda bi, j: (bi, j + skip_blocks, 0))]
    return [pl.BlockSpec((NB, tm, d), lambda bi, j: (bi, jnp.minimum(j + skip_blocks, ncb - 1), 0)),
            pl.BlockSpec((NB, tm, d), lambda bi, j: (bi, jnp.maximum(j + skip_blocks - ncb, 0), 0))]


def _residual_rows(x_refs, ncb, skip_blocks=0):
    if len(x_refs) == 1:
        return x_refs[0][0]
    return jnp.where(pl.program_id(1) + skip_blocks < ncb, x_refs[0][0], x_refs[1][0])


def _proj_kernel(*refs, n_x, ncb):
    x_refs = refs[:n_x]
    (mod_ref, g1_ref, w_ref, gains_ref, cos_ref, sin_ref, wg_ref, bg_ref, bm_ref,
     q_ref, k_ref, vt_ref, zc_ref, zd_ref) = refs[n_x:]
    d = x_refs[0].shape[2]
    tm = x_refs[0].shape[1]
    x = _residual_rows(x_refs, ncb)
    mod = mod_ref[0, 0]
    sh1 = mod[:, 0:d]
    sc1 = mod[:, d:2 * d]
    ms = jnp.mean(x * x, axis=-1, keepdims=True)
    h = (x * lax.rsqrt(ms + EPS)) * g1_ref[...]
    hb = (h * (1.0 + sc1) + sh1).astype(BF16)

    ones_bd = _group_ones(GROUP_WIDTH, HEAD_DIM)
    cos = cos_ref[...]
    sin = sin_ref[...]
    gains = gains_ref[...]
    gw = GROUP_WIDTH

    z = _dot(hb, w_ref[:, 0:W_Q + 2 * W_KV])
    q_scale = HEAD_DIM ** -0.5 * LOG2E
    for n in range(2):
        q = _qk_norm_rope(z[:, n * gw:(n + 1) * gw], gains[n:n + 1], cos, sin, ones_bd, q_scale)
        q_ref[0, :, n * gw:(n + 1) * gw] = q.astype(BF16)
    k_ref[0] = _qk_norm_rope(z[:, W_Q:W_Q + W_KV], gains[2:3], cos, sin, ones_bd, 1.0).astype(BF16)
    vt_ref[0] = z[:, W_Q + W_KV:W_Q + 2 * W_KV].T.astype(BF16)

    base = W_Q + 2 * W_KV
    z = _dot(hb, w_ref[:, base:base + W_C])
    zc_ref[0, :, 0:GLA_KW] = z[:, 0:GLA_KW] * (GLA_DK ** -0.5)
    zc_ref[0, :, GLA_KW:768] = z[:, GLA_KW:768]
    pre = _dot(z[:, 768:896].astype(BF16), wg_ref[...]) + bg_ref[...]
    zc_ref[0, :, 768:1024] = _log_sigmoid(pre) * (1.0 / GLA_GATE_TAU)

    z = _dot(hb, w_ref[:, base + W_C:W_ALL])
    zd_ref[0, :, 0:gw] = z[:, 0:gw]
    zd_ref[0, :, gw:2 * gw] = z[:, gw:2 * gw] * (HEAD_DIM ** -0.5)
    zd_ref[0, :, 2 * gw:4 * gw] = z[:, 2 * gw:4 * gw]
    gates = z[:, 4 * gw:4 * gw + LANES] + bm_ref[...]
    is_input_gate = _iota((tm, LANES), 1) < 2 * GROUP_HEADS
    zd_ref[0, :, 4 * gw:4 * gw + LANES] = jnp.where(is_input_gate, gates, _log_sigmoid(gates))


def _projection(xs, modsel, g1, w_big, gains, cos_t, sin_t, wg, bg, bm, tm, ncb):
    b, _, d = xs[0].shape
    t = sum(a.shape[1] for a in xs)
    nblk = t // tm
    const = lambda bi, j: (0, 0)
    return pl.pallas_call(
        _per_batch_row(functools.partial(_proj_kernel, n_x=len(xs), ncb=ncb), "b" * (len(xs) + 1) + "-" * 8 + "b" * 5),
        out_shape=(jax.ShapeDtypeStruct((b, t, W_Q), BF16),
                   jax.ShapeDtypeStruct((b, t, W_KV), BF16),
                   jax.ShapeDtypeStruct((b, W_KV, t), BF16),
                   jax.ShapeDtypeStruct((b, t, 1024), F32),
                   jax.ShapeDtypeStruct((b, t, W_D), F32)),
        grid=(b // NB, nblk),
        in_specs=_residual_specs(xs, tm, ncb) + [
                  pl.BlockSpec((NB, 1, 1, N_MOD * d), lambda bi, j: (bi, jnp.where(j >= ncb, 1, 0), 0, 0)),
                  pl.BlockSpec((1, d), const),
                  pl.BlockSpec((d, W_ALL), const),
                  pl.BlockSpec((3, GROUP_WIDTH), const),
                  pl.BlockSpec((tm, LANES), lambda bi, j: (j, 0)),
                  pl.BlockSpec((tm, LANES), lambda bi, j: (j, 0)),
                  pl.BlockSpec((LANES, 2 * GLA_KW), const),
                  pl.BlockSpec((1, 2 * GLA_KW), const),
                  pl.BlockSpec((1, LANES), const)],
        out_specs=(pl.BlockSpec((NB, tm, W_Q), lambda bi, j: (bi, j, 0)),
                   pl.BlockSpec((NB, tm, W_KV), lambda bi, j: (bi, j, 0)),
                   pl.BlockSpec((NB, W_KV, tm), lambda bi, j: (bi, 0, j)),
                   pl.BlockSpec((NB, tm, 1024), lambda bi, j: (bi, j, 0)),
                   pl.BlockSpec((NB, tm, W_D), lambda bi, j: (bi, j, 0))),
        compiler_params=_cparams(("arbitrary", "arbitrary")),
        name="projection",
    )(*xs, modsel, g1, w_big, gains, cos_t, sin_t, wg, bg, bm)


ONES_ROWS = 16


def _masked_query_tiles(q_ref):
    low = _iota((q_ref.shape[1], LANES), 1) < HEAD_DIM
    tiles = []
    for n in range(GROUP_WIDTH // LANES):
        q128 = q_ref[0, :, n * LANES:(n + 1) * LANES]
        zero = jnp.zeros_like(q128)
        tiles += [jnp.where(low, q128, zero), jnp.where(low, zero, q128)]
    return tiles


def _heads_to_rows(o_t):
    halves = [jnp.concatenate(o_t[2 * n:2 * n + 2], axis=0).T for n in range(len(o_t) // 2)]
    return jnp.concatenate(halves, axis=1)


def _window_stages(sink_ref, q_ref, k_refs, v_refs, o_ref, nctx, nblk):
    tq = q_ref.shape[1]
    halo = Q_BLOCK
    n_win = tq + 2 * halo
    n_keys = n_win + k_refs[3].shape[1]
    j = pl.program_id(1)
    is_lat = j >= nctx
    off = 4 * tq
    lo_prev = jnp.where(j > nctx, -halo, off)
    lo_cur = jnp.where(is_lat, -halo, off)
    lo_next = jnp.where(jnp.logical_and(is_lat, j + 1 <= nblk - 1), -halo, off)
    row = _iota((n_keys, tq), 0)
    diff = row - halo - _iota((n_keys, tq), 1)
    lo = jnp.where(row < halo, lo_prev, jnp.where(row < halo + tq, lo_cur, lo_next))
    ok = jnp.logical_or(row >= n_win, jnp.logical_and(diff >= lo, diff <= halo))
    keys = jnp.concatenate([r[0] for r in k_refs], axis=0)
    v_t = jnp.concatenate([r[0] for r in v_refs], axis=1)
    ones = jnp.ones((ONES_ROWS, n_keys), BF16)
    tiles = _masked_query_tiles(q_ref)
    yield
    scores = [jnp.where(ok, _dot_nt(keys, qp), NEG) for qp in tiles]
    yield
    o_t = []
    for c, s_t in enumerate(scores):
        sink = sink_ref[ATTN_HEAD_ORDER[c]] * LOG2E
        m = jnp.maximum(jnp.max(s_t, axis=0, keepdims=True), sink)
        kv = c % KV_HEADS
        vx = jnp.concatenate([v_t[kv * HEAD_DIM:(kv + 1) * HEAD_DIM], ones], axis=0)
        acc = _dot(vx, jnp.exp2(s_t - m).astype(BF16))
        o_t.append(acc[0:HEAD_DIM] / (acc[HEAD_DIM:HEAD_DIM + 1] + jnp.exp2(sink - m)))
        yield
    o_ref[0] = _heads_to_rows(o_t).astype(BF16)


def _attention_kernel(sink_ref, qa_ref, qb_ref, ka_ref, vta_ref, kp_ref, kc_ref, kn_ref, kx_ref,
                      vp_ref, vc_ref, vn_ref, vx_ref, oa_ref, ob_ref, *, lc, tk, nk, nctx, nblk):
    tq = qa_ref.shape[1]
    n_tiles = GROUP_HEADS
    tiles = _masked_query_tiles(qa_ref)
    ones = jnp.ones((ONES_ROWS, tk), BF16)

    def window():
        return _window_stages(sink_ref, qb_ref, (kp_ref, kc_ref, kn_ref, kx_ref),
                              (vp_ref, vc_ref, vn_ref, vx_ref), ob_ref, nctx, nblk)

    def scores(rows):
        k = ka_ref[0, rows, :]
        return [_dot_nt(k, qp) for qp in tiles]

    def chunk(s_t, rows, m, accs):
        m_out, acc_out = [], []
        for c in range(n_tiles):
            col_max = jnp.max(s_t[c], axis=0, keepdims=True)
            m_new = col_max if m is None else jnp.maximum(m[c], col_max)
            p_t = jnp.exp2(s_t[c] - m_new).astype(BF16)
            kv = c % KV_HEADS
            vx = jnp.concatenate([vta_ref[0, kv * HEAD_DIM:(kv + 1) * HEAD_DIM, rows], ones], axis=0)
            upd = _dot(vx, p_t)
            m_out.append(m_new)
            acc_out.append(upd if m is None else jnp.exp2(m[c] - m_new) * accs[c] + upd)
        return m_out, acc_out

    def key_chunks(first, count, m, accs, side=None):
        rows = [slice((first + c) * tk, (first + c + 1) * tk) for c in range(count)]
        s_next = scores(rows[0])
        for c in range(count):
            s_t = s_next
            if c + 1 < count:
                s_next = scores(rows[c + 1])
            if side is not None:
                next(side, None)
            m, accs = chunk(s_t, rows[c], m, accs)
        return m, accs

    n_ctx = lc // tk
    m, accs = key_chunks(0, n_ctx, None, None)

    def finish(accs):
        return _heads_to_rows([a[0:HEAD_DIM] / a[HEAD_DIM:HEAD_DIM + 1] for a in accs])

    def latent_queries():
        side = window()
        out = finish(key_chunks(n_ctx, nk, m, accs, side)[1])
        for _ in side:
            pass
        return out

    def context_queries():
        for _ in window():
            pass
        return finish(accs)

    o = lax.cond(pl.program_id(1) * tq >= lc, latent_queries, context_queries)
    oa_ref[0] = o.astype(BF16)


def _attention(q, k, vt, sink, lc, tq, tk):
    b, t, _ = q.shape
    nblk = t // tq
    nctx = lc // tq
    nk = (t - lc) // tk
    assert lc % tk == 0 and (t - lc) % tk == 0
    per = tq // Q_BLOCK
    prev_blk = lambda j: jnp.maximum(j * per - 1, 0)
    next_blk = lambda j: jnp.minimum((j + 1) * per, nblk * per - 1)
    k_specs = [pl.BlockSpec((NB, Q_BLOCK, LANES), lambda bi, j: (bi, prev_blk(j), 1)),
               pl.BlockSpec((NB, tq, LANES), lambda bi, j: (bi, j, 1)),
               pl.BlockSpec((NB, Q_BLOCK, LANES), lambda bi, j: (bi, next_blk(j), 1)),
               pl.BlockSpec((NB, lc, LANES), lambda bi, j: (bi, 0, 1))]
    v_specs = [pl.BlockSpec((NB, LANES, Q_BLOCK), lambda bi, j: (bi, 1, prev_blk(j))),
               pl.BlockSpec((NB, LANES, tq), lambda bi, j: (bi, 1, j)),
               pl.BlockSpec((NB, LANES, Q_BLOCK), lambda bi, j: (bi, 1, next_blk(j))),
               pl.BlockSpec((NB, LANES, lc), lambda bi, j: (bi, 1, 0))]
    out = jax.ShapeDtypeStruct((b, t, GROUP_WIDTH), BF16)
    q_spec = lambda col: pl.BlockSpec((NB, tq, GROUP_WIDTH), lambda bi, j: (bi, j, col))
    return pl.pallas_call(
        _per_batch_row(functools.partial(_attention_kernel, lc=lc, tk=tk, nk=nk, nctx=nctx, nblk=nblk),
                       "-" + "b" * 14),
        out_shape=(out, out),
        grid=(b // NB, nblk),
        in_specs=[pl.BlockSpec(memory_space=pltpu.SMEM), q_spec(0), q_spec(1),
                  pl.BlockSpec((NB, t, LANES), lambda bi, j: (bi, 0, 0)),
                  pl.BlockSpec((NB, LANES, t), lambda bi, j: (bi, 0, 0))] + k_specs + v_specs,
        out_specs=(q_spec(0), q_spec(0)),
        compiler_params=_cparams(("arbitrary", "arbitrary")),
        name="attention",
    )(sink, q, q, k, vt, k, k, k, k, vt, vt, vt, vt)


def _chunk_tri(rows, reverse):
    t = _iota((rows, rows), 0)
    u = _iota((rows, rows), 1)
    same = t // CHUNK == u // CHUNK
    return jnp.logical_and(same, (u >= t) if reverse else (u <= t)).astype(BF16)


def _tile4(x):
    return jnp.concatenate([x, x, x, x], axis=0)


def _rev_block(i, ncb, nblk):
    return jnp.where(i < ncb, ncb - 1 - i, nblk - 1 - (i - ncb))


class _Item:
    pass


def _scan_items(directions, nchunk):
    items = []
    for pos in range(nchunk):
        for d in directions:
            it = _Item()
            it.d = d
            it.c = nchunk - 1 - pos if d.reverse else pos
            it.sl = slice(it.c * CHUNK, (it.c + 1) * CHUNK)
            items.append(it)
    return items


def _gla_stages(qkv_f_ref, la_f_ref, qkv_b_ref, la_b_ref, of_ref, ob_ref, sf_ref, sb_ref):
    @pl.when(pl.program_id(1) == 0)
    def _():
        sf_ref[...] = jnp.zeros_like(sf_ref)
        sb_ref[...] = jnp.zeros_like(sb_ref)

    rows = qkv_f_ref.shape[1]
    same_kd = _iota((GROUP_WIDTH, GLA_KW), 0) // CHUNK == _iota((GROUP_WIDTH, GLA_KW), 1) // GLA_DK
    same_kv = _iota((GROUP_WIDTH, GROUP_WIDTH), 0) // CHUNK == _iota((GROUP_WIDTH, GROUP_WIDTH), 1) // HEAD_DIM
    t_pos = _iota((CHUNK, GROUP_WIDTH), 0)
    s_pos = _iota((CHUNK, GROUP_WIDTH), 1) % CHUNK

    directions = []
    for qkv_ref, la_ref, o_ref, s_ref, reverse in ((qkv_f_ref, la_f_ref, of_ref, sf_ref, False),
                                                   (qkv_b_ref, la_b_ref, ob_ref, sb_ref, True)):
        d = _Item()
        d.reverse, d.o_ref, d.s_ref = reverse, o_ref, s_ref
        d.b = _dot01(_chunk_tri(rows, reverse), la_ref[0])
        d.q = qkv_ref[0, :, 0:GLA_KW]
        d.k = qkv_ref[0, :, GLA_KW:2 * GLA_KW]
        d.v = qkv_ref[0, :, 2 * GLA_KW:2 * GLA_KW + GROUP_WIDTH].astype(BF16)
        d.causal = (s_pos >= t_pos) if reverse else (s_pos <= t_pos)
        d.end = 0 if reverse else CHUNK - 1
        d.state = s_ref[...]
        directions.append(d)
    items = _scan_items(directions, rows // CHUNK)

    yield
    for it in items:
        d = it.d
        bc = d.b[it.sl]
        b_end = bc[d.end:d.end + 1]
        b_mid = bc[CHUNK // 2:CHUNK // 2 + 1]
        qc, kc = d.q[it.sl], d.k[it.sl]
        it.vc = d.v[it.sl]
        it.q_inter = (qc * jnp.exp(bc)).astype(BF16)
        it.q_intra = (qc * jnp.exp(bc - b_mid)).astype(BF16)
        k_intra = (kc * jnp.exp(b_mid - bc)).astype(BF16)
        it.k_state = (kc * jnp.exp(b_end - bc)).astype(BF16)
        it.decay = jnp.exp(b_end)
        it.k_bd = jnp.where(same_kd, _tile4(k_intra), jnp.zeros((), BF16))
        it.v_bd = jnp.where(same_kv, _tile4(it.vc), jnp.zeros((), BF16))
    yield
    for it in items:
        it.scores = jnp.where(it.d.causal, _dot_nt(it.q_intra, it.k_bd), 0.0).astype(BF16)
    yield
    for it in items:
        it.o_intra = _dot(it.scores, it.v_bd)
        it.update = jnp.where(same_kd, _dot_tn(it.vc, it.k_state), 0.0)
    yield
    for it in items:
        it.state_in = it.d.state
        it.d.state = it.d.state * it.decay + it.update
    yield
    for it in items:
        it.d.o_ref[0, it.sl, :] = it.o_intra + _dot_nt(it.q_inter, it.state_in.astype(BF16))
    yield
    for d in directions:
        d.s_ref[...] = d.state


def _expand_heads(g, base, rows):
    hid = _iota((rows, GROUP_WIDTH), 1) // HEAD_DIM
    cols = [jnp.broadcast_to(g[:, base + h:base + h + 1], (rows, GROUP_WIDTH)) for h in range(GROUP_HEADS)]
    return jnp.where(hid == 0, cols[0], jnp.where(hid == 1, cols[1], jnp.where(hid == 2, cols[2], cols[3])))


def _mlstm_stages(qkv_f_ref, gate_f_ref, qkv_b_ref, gate_b_ref, of_ref, ob_ref,
                  cf_ref, nf_ref, mf_ref, cb_ref, nb_ref, mb_ref):
    @pl.when(pl.program_id(1) == 0)
    def _():
        for r in (cf_ref, nf_ref, mf_ref, cb_ref, nb_ref, mb_ref):
            r[...] = jnp.zeros_like(r)

    rows = qkv_f_ref.shape[1]
    gw = GROUP_WIDTH
    same_head = _iota((gw, gw), 0) // HEAD_DIM == _iota((gw, gw), 1) // HEAD_DIM
    ones_bd = same_head.astype(BF16)
    hid = _iota((CHUNK, gw), 1) // HEAD_DIM
    t_pos = _iota((CHUNK, gw), 0)
    s_pos = _iota((CHUNK, gw), 1) % CHUNK
    diag = (s_pos == t_pos).astype(F32)
    ones_cc = jnp.ones((CHUNK, CHUNK), BF16)

    directions = []
    for index, (qkv_ref, gate_ref, o_ref, c_ref, n_ref, m_ref) in enumerate(
            ((qkv_f_ref, gate_f_ref, of_ref, cf_ref, nf_ref, mf_ref),
             (qkv_b_ref, gate_b_ref, ob_ref, cb_ref, nb_ref, mb_ref))):
        d = _Item()
        d.reverse = index == 1
        d.o_ref, d.c_ref, d.n_ref, d.m_ref = o_ref, c_ref, n_ref, m_ref
        g = gate_ref[0]
        i_col = GROUP_HEADS * index
        f_col = 2 * GROUP_HEADS + GROUP_HEADS * index
        f_all = _dot01(_chunk_tri(rows, d.reverse), g)
        d.ig = _expand_heads(g, i_col, rows)
        d.f_cum = _expand_heads(f_all, f_col, rows)
        d.q = qkv_ref[0, :, 0:gw]
        d.k = qkv_ref[0, :, gw:2 * gw]
        d.v = qkv_ref[0, :, 2 * gw:3 * gw].astype(BF16)
        d.causal = (s_pos >= t_pos) if d.reverse else (s_pos <= t_pos)
        d.end = 0 if d.reverse else CHUNK - 1
        d.c_state = c_ref[...]
        d.n_state = n_ref[...]
        d.m_state = m_ref[...]
        directions.append(d)
    items = _scan_items(directions, rows // CHUNK)

    yield
    for it in items:
        d = it.d
        it.fc, it.ic = d.f_cum[it.sl], d.ig[it.sl]
        it.qc, it.kc, it.vc = d.q[it.sl], d.k[it.sl], d.v[it.sl]
        it.qb = it.qc.astype(BF16)
        it.key_term = _dot01(ones_cc, (it.fc - it.ic) * diag)
    yield
    for it in items:
        it.logw = jnp.where(it.d.causal, it.fc - it.key_term, NEG)
        head_max = [jnp.max(it.logw[:, h * HEAD_DIM:(h + 1) * HEAD_DIM], axis=-1, keepdims=True)
                    for h in range(GROUP_HEADS)]
        it.row_max = jnp.where(hid < 2, jnp.where(hid == 0, head_max[0], head_max[1]),
                               jnp.where(hid == 2, head_max[2], head_max[3]))
        it.k_bd = jnp.where(same_head, _tile4(it.kc.astype(BF16)), jnp.zeros((), BF16))
        it.v_bd = jnp.where(same_head, _tile4(it.vc), jnp.zeros((), BF16))
    yield
    for it in items:
        d = it.d
        it.log_inter = it.fc + d.m_state
        it.m_t = jnp.maximum(it.log_inter, it.row_max)
        d.m_state = it.m_new = it.m_t[d.end:d.end + 1]
    yield
    for it in items:
        end = it.d.end
        it.w = jnp.exp(it.logw - it.m_t)
        it.w_inter = jnp.exp(it.log_inter - it.m_t)
        it.decay = it.w_inter[end:end + 1]
        it.k_end = it.kc * jnp.exp(it.fc[end:end + 1] - it.fc + it.ic - it.m_new)
    yield
    for it in items:
        it.qk = _dot_nt(it.qb, it.k_bd) * it.w
    yield
    for it in items:
        it.num = _dot(it.qk.astype(BF16), it.v_bd)
        it.den = _dot01_r(it.qk, ones_bd)
        it.update = jnp.where(same_head, _dot_tn(it.k_end.astype(BF16), it.vc), 0.0)
    yield
    for it in items:
        d = it.d
        it.c_in, it.n_in = d.c_state, d.n_state
        d.c_state = d.c_state * it.decay + it.update
        d.n_state = d.n_state * it.decay + jnp.sum(it.k_end, axis=0, keepdims=True)
    yield
    for it in items:
        num = it.num + it.w_inter * _dot(it.qb, it.c_in.astype(BF16))
        den = it.den + it.w_inter * _dot((it.qc * it.n_in).astype(BF16), ones_bd)
        it.d.o_ref[0, it.sl, :] = num / jnp.maximum(jnp.abs(den), jnp.exp(-it.m_t))
    yield
    for d in directions:
        d.c_ref[...] = d.c_state
        d.n_ref[...] = d.n_state
        d.m_ref[...] = d.m_state


def _scans_kernel(*refs):
    running = []
    for i in range(NB):
        blocks = [r.at[i:i + 1] for r in refs[0:12]]
        state = [r.at[i] for r in refs[12:20]]
        running.append(_gla_stages(*blocks[0:4], *blocks[8:10], *state[0:2]))
        running.append(_mlstm_stages(*blocks[4:8], *blocks[10:12], *state[2:8]))
    while running:
        running = [g for g in running if next(g, StopIteration) is not StopIteration]


def _scans(zc, zd, lc, rows):
    b, t, _ = zc.shape
    nblk = t // rows
    ncb = lc // rows
    gw = GROUP_WIDTH
    fwd = lambda col: (lambda bi, i: (bi, i, col))
    rev = lambda col: (lambda bi, i: (bi, _rev_block(i, ncb, nblk), col))
    gla_qkv_w = 2 * GLA_KW + gw
    gate_col = 4 * gw // LANES
    mlstm_state = [pltpu.VMEM((NB, gw, gw), F32), pltpu.VMEM((NB, 1, gw), F32), pltpu.VMEM((NB, 1, gw), F32)]
    gla_state = [pltpu.VMEM((NB, gw, GLA_KW), F32)]
    out = jax.ShapeDtypeStruct((b, t, gw), F32)
    return pl.pallas_call(
        _scans_kernel,
        out_shape=(out, out, out, out),
        grid=(b // NB, nblk),
        in_specs=[pl.BlockSpec((NB, rows, gla_qkv_w), fwd(0)),
                  pl.BlockSpec((NB, rows, GLA_KW), fwd(6)),
                  pl.BlockSpec((NB, rows, gla_qkv_w), rev(0)),
                  pl.BlockSpec((NB, rows, GLA_KW), rev(7)),
                  pl.BlockSpec((NB, rows, 3 * gw), fwd(0)),
                  pl.BlockSpec((NB, rows, LANES), fwd(gate_col)),
                  pl.BlockSpec((NB, rows, 3 * gw), rev(0)),
                  pl.BlockSpec((NB, rows, LANES), rev(gate_col))],
        out_specs=(pl.BlockSpec((NB, rows, gw), fwd(0)), pl.BlockSpec((NB, rows, gw), rev(0)),
                   pl.BlockSpec((NB, rows, gw), fwd(0)), pl.BlockSpec((NB, rows, gw), rev(0))),
        scratch_shapes=gla_state + gla_state + mlstm_state + mlstm_state,
        compiler_params=_cparams(("arbitrary", "arbitrary")),
        name="scans",
    )(zc, zc, zc, zc, zd, zd, zd, zd)


def _head_rms(o, gain, ones_bd):
    ss = _dot01_r2(o * o, ones_bd)
    return o * lax.rsqrt(ss * (1.0 / HEAD_DIM) + EPS) * gain


def _out_kernel(*refs, n_x, ncb, skip_blocks):
    x_refs = refs[:n_x]
    (mod_ref, oa_ref, ob_ref, gf_ref, gb_ref, r_ref, mf_ref, mb_ref, og_ref,
     gains_ref, g2_ref, wo_ref, w1_ref, w2_ref, y_ref) = refs[n_x:]
    d = x_refs[0].shape[2]
    x = _residual_rows(x_refs, ncb, skip_blocks)
    mod = mod_ref[0, 0]
    ga1 = mod[:, 2 * d:3 * d]
    sh2 = mod[:, 3 * d:4 * d]
    sc2 = mod[:, 4 * d:5 * d]
    ga2 = mod[:, 5 * d:6 * d]
    ones_bd = _group_ones(GROUP_WIDTH, HEAD_DIM)
    gains = gains_ref[...]

    r = r_ref[0]
    o_c = _head_rms(gf_ref[0] + gb_ref[0], gains[0:1], ones_bd) * (r * _sigmoid(r))
    o_d = _sigmoid(og_ref[0]) * _head_rms(mf_ref[0] + mb_ref[0], gains[1:2], ones_bd)
    o_cat = jnp.concatenate([oa_ref[0], ob_ref[0], o_c.astype(BF16), o_d.astype(BF16)], axis=1)
    x = x + ga1 * _dot(o_cat, wo_ref[...])

    ms = jnp.mean(x * x, axis=-1, keepdims=True)
    h = (x * lax.rsqrt(ms + EPS)) * g2_ref[...]
    hb = (h * (1.0 + sc2) + sh2).astype(BF16)
    hid = jnp.maximum(_dot(hb, w1_ref[...]), 0.0)
    hid = (hid * hid).astype(BF16)
    y_ref[0] = x + ga2 * _dot(hid, w2_ref[...])


def _out_mlp(xs, modsel, oa, ob, gf, gb, zc, mf, mb, zd, gains, g2, wo, w1, w2, tm, ncb, skip_blocks):
    b, _, d = xs[0].shape
    t = sum(a.shape[1] for a in xs)
    nblk = t // tm - skip_blocks
    gw = GROUP_WIDTH
    row = lambda col: (lambda bi, j: (bi, j + skip_blocks, col))
    const = lambda bi, j: (0, 0)
    resident = lambda shape: pl.BlockSpec(shape, const, pipeline_mode=pl.Buffered(1))
    return pl.pallas_call(
        _per_batch_row(functools.partial(_out_kernel, n_x=len(xs), ncb=ncb, skip_blocks=skip_blocks),
                       "b" * (len(xs) + 9) + "-" * 5 + "b"),
        out_shape=jax.ShapeDtypeStruct((b, nblk * tm, d), F32),
        grid=(b // NB, nblk),
        in_specs=_residual_specs(xs, tm, ncb, skip_blocks) + [
                  pl.BlockSpec((NB, 1, 1, N_MOD * d),
                               lambda bi, j: (bi, jnp.where(j + skip_blocks >= ncb, 1, 0), 0, 0)),
                  pl.BlockSpec((NB, tm, gw), row(0)),
                  pl.BlockSpec((NB, tm, gw), row(0)),
                  pl.BlockSpec((NB, tm, gw), row(0)),
                  pl.BlockSpec((NB, tm, gw), row(0)),
                  pl.BlockSpec((NB, tm, gw), row(2)),
                  pl.BlockSpec((NB, tm, gw), row(0)),
                  pl.BlockSpec((NB, tm, gw), row(0)),
                  pl.BlockSpec((NB, tm, gw), row(3)),
                  pl.BlockSpec((2, gw), const),
                  pl.BlockSpec((1, d), const),
                  resident((d, d)),
                  resident((d, 4 * d)),
                  resident((4 * d, d))],
        out_specs=pl.BlockSpec((NB, tm, d), lambda bi, j: (bi, j, 0)),
        compiler_params=_cparams(("arbitrary", "arbitrary")),
        name="out_mlp",
    )(*xs, modsel, oa, ob, gf, gb, zc, mf, mb, zd, gains, g2, wo, w1, w2)


def _projection_columns():
    gw, kw = GROUP_WIDTH, KV_HEADS * HEAD_DIM
    splits = (gw, kw, kw, gw, kw, kw, GLA_KW, GLA_KW, gw, gw, 2 * GLA_GATE_RANK,
              gw, gw, gw, gw, 2 * GROUP_HEADS, 2 * GROUP_HEADS)
    offs = np.concatenate([[0], np.cumsum(splits)])
    in_width = int(offs[-1])
    half_split = np.concatenate([np.arange(0, HEAD_DIM, 2), np.arange(1, HEAD_DIM, 2)])
    plain = np.arange(HEAD_DIM)
    cols = []
    mixers = (0, 3)
    for base in mixers:
        cols += [offs[base] + h * HEAD_DIM + half_split for h in ATTN_HEAD_ORDER]
    for base in mixers:
        cols += [offs[base + 1] + g * HEAD_DIM + half_split for g in range(KV_HEADS)]
    for base in mixers:
        cols += [offs[base + 2] + g * HEAD_DIM + plain for g in range(KV_HEADS)]
    cols.append(np.arange(offs[6], offs[11]))
    cols.append(np.full(W_C - (offs[11] - offs[6]), in_width))
    cols.append(np.arange(offs[11], offs[17]))
    cols.append(np.full(W_D - (offs[17] - offs[11]), in_width))
    cols = np.concatenate(cols).astype(np.int32)
    assert cols.shape[0] == W_ALL
    return cols, in_width, half_split


def _rope_tables(seq, lc):
    rows = seq // GRID_W
    row = jnp.repeat(jnp.arange(rows, dtype=F32), GRID_W)
    col = jnp.tile(jnp.arange(GRID_W, dtype=F32), rows)
    n_freq = HEAD_DIM // 4
    inv = ROPE_THETA ** (-jnp.arange(n_freq, dtype=F32) / n_freq)
    ang = jnp.concatenate([row[:, None] * inv, col[:, None] * inv], axis=-1)
    cos, sin = jnp.cos(ang), jnp.sin(ang)
    cos_t = jnp.tile(cos, (1, 4))
    sin_t = jnp.tile(jnp.concatenate([-sin, sin], axis=-1), (1, 2))
    cos_t = jnp.concatenate([jnp.ones((lc, LANES), F32), cos_t], axis=0)
    sin_t = jnp.concatenate([jnp.zeros((lc, LANES), F32), sin_t], axis=0)
    return cos_t, sin_t


def kernel(x, c, ctx, c_ctx, w_mod, b_mod, g_norm1, g_norm2, w_in, g_q_a, g_k_a, g_q_b, g_k_b, sink_b,
           w_gla_gate, b_gla_gate, g_gla_out, b_mlstm_i, b_mlstm_f, g_mlstm_out, w_out, w_mlp1, w_mlp2):
    b, seq, d = x.shape
    lc = ctx.shape[1]
    depth = w_mod.shape[0]
    tm = ROW_TILE
    assert b % NB == 0 and d == 4 * GROUP_WIDTH and lc % tm == 0 and seq % tm == 0 and seq % GRID_W == 0 and b + 1 <= MOD_ROWS

    cols, in_width, half_split = _projection_columns()
    cos_t, sin_t = _rope_tables(seq, lc)
    attn_rows = np.concatenate([h * HEAD_DIM + np.arange(HEAD_DIM) for h in ATTN_HEAD_ORDER])
    out_rows = np.concatenate([attn_rows, GROUP_WIDTH + attn_rows, np.arange(2 * GROUP_WIDTH, d)])

    cc = jnp.zeros((MOD_ROWS, d), F32).at[0:b].set(c).at[b].set(c_ctx)
    mod_all = _modulation(cc, w_mod, b_mod)

    xs = (ctx, x)
    for l in range(depth):
        modsel = jnp.stack([jnp.broadcast_to(mod_all[l, b], (b, N_MOD * d)), mod_all[l, 0:b]], axis=1)
        modsel = modsel.reshape(b, 2, 1, N_MOD * d)

        w_ext = jnp.concatenate([w_in[l], jnp.zeros((d, 1), F32)], axis=1)
        w_big = jnp.take(w_ext, cols, axis=1).astype(BF16)
        tiled = lambda g, n: jnp.tile(g[half_split], n)
        gains_qk = jnp.stack([tiled(g_q_a[l], GROUP_HEADS), tiled(g_q_b[l], GROUP_HEADS),
                              jnp.concatenate([tiled(g_k_a[l], KV_HEADS), tiled(g_k_b[l], KV_HEADS)])])
        wg = jnp.zeros((LANES, 2 * GLA_KW), F32)
        wg = wg.at[0:GLA_GATE_RANK, 0:GLA_KW].set(w_gla_gate[l, 0])
        wg = wg.at[GLA_GATE_RANK:2 * GLA_GATE_RANK, GLA_KW:].set(w_gla_gate[l, 1]).astype(BF16)
        bg = b_gla_gate[l].reshape(1, 2 * GLA_KW)
        bm = jnp.zeros((1, LANES), F32)
        bm = bm.at[0, 0:2 * GROUP_HEADS].set(b_mlstm_i[l].reshape(-1))
        bm = bm.at[0, 2 * GROUP_HEADS:4 * GROUP_HEADS].set(b_mlstm_f[l].reshape(-1))

        q, k, vt, zc, zd = _projection(xs, modsel, g_norm1[l].reshape(1, d), w_big, gains_qk,
                                      cos_t, sin_t, wg, bg, bm, tm, lc // tm)
        oa, ob = _attention(q, k, vt, sink_b[l], lc, tq=ROW_TILE, tk=KEY_CHUNK)
        gf, gb, mf, mb = _scans(zc, zd, lc, tm)

        gains_out = jnp.stack([jnp.tile(g_gla_out[l], 4), jnp.tile(g_mlstm_out[l], 4)])
        last = l == depth - 1
        xs = (_out_mlp(xs, modsel, oa, ob, gf, gb, zc, mf, mb, zd, gains_out, g_norm2[l].reshape(1, d),
                       w_out[l][out_rows].astype(BF16), w_mlp1[l].astype(BF16), w_mlp2[l].astype(BF16), tm,
                       ncb=lc // tm, skip_blocks=lc // tm if last else 0),)
    return xs[0]
```

```python
import functools

import numpy as np
import jax
import jax.numpy as jnp
from jax import lax
from jax.experimental import pallas as pl
from jax.experimental.pallas import tpu as pltpu

F32 = jnp.float32
BF16 = jnp.bfloat16

HEAD_DIM = 64
GROUP_HEADS = 4
GROUP_WIDTH = GROUP_HEADS * HEAD_DIM
KV_HEADS = 2
GRID_W = 64
Q_BLOCK = 128
ROPE_THETA = 10000.0
GLA_DK = 32
GLA_KW = GROUP_HEADS * GLA_DK
GLA_GATE_RANK = 16
GLA_GATE_TAU = 16.0
N_MOD = 6
EPS = 1e-6
LOG2E = 1.4426950408889634
NEG = -1e30
CHUNK = 64
LANES = 128
MOD_ROWS = 16
VMEM_LIMIT = 56 * 1024 * 1024

ATTN_HEAD_ORDER = (0, 2, 1, 3)
W_Q = 2 * GROUP_WIDTH
W_KV = 2 * KV_HEADS * HEAD_DIM
W_C = 896
W_D = 1152
W_ALL = W_Q + 2 * W_KV + W_C + W_D


NB = 2
ROW_TILE = 256


def _per_batch_row(body, layout):
    def kernel(*refs):
        assert len(refs) == len(layout)
        for i in range(NB):
            body(*[r.at[i:i + 1] if how == "b" else r for r, how in zip(refs, layout)])
    return kernel


def _cparams(sem):
    return pltpu.CompilerParams(dimension_semantics=sem, vmem_limit_bytes=VMEM_LIMIT)


def _dot(a, b):
    return jnp.dot(a, b, preferred_element_type=F32)


def _dot_nt(a, b):
    return lax.dot_general(a, b, (((1,), (1,)), ((), ())), preferred_element_type=F32)


def _dot_tn(a, b):
    return lax.dot_general(a, b, (((0,), (0,)), ((), ())), preferred_element_type=F32)


def _split3(x):
    hi = x.astype(BF16)
    r1 = x - hi.astype(F32)
    mid = r1.astype(BF16)
    lo = (r1 - mid.astype(F32)).astype(BF16)
    return hi, mid, lo


def _dot01(a01, x):
    hi, mid, lo = _split3(x)
    return _dot(a01, hi) + _dot(a01, mid) + _dot(a01, lo)


def _dot01_r(x, b01):
    hi, mid, lo = _split3(x)
    return _dot(hi, b01) + _dot(mid, b01) + _dot(lo, b01)


def _dot01_r2(x, b01):
    hi = x.astype(BF16)
    lo = (x - hi.astype(F32)).astype(BF16)
    return _dot(hi, b01) + _dot(lo, b01)


def _log_sigmoid(x):
    return jnp.minimum(x, 0.0) - jnp.log(1.0 + jnp.exp(-jnp.abs(x)))


def _sigmoid(x):
    return 1.0 / (1.0 + jnp.exp(-x))


def _iota(shape, dim):
    return lax.broadcasted_iota(jnp.int32, shape, dim)


def _group_ones(n, group):
    return (_iota((n, n), 0) // group == _iota((n, n), 1) // group).astype(BF16)


def _mod_kernel(c_ref, w_ref, b_ref, o_ref):
    c = c_ref[...]
    s = (c * _sigmoid(c)).astype(BF16)
    o_ref[0] = _dot(s, w_ref[0].astype(BF16)) + b_ref[0]


def _modulation(cc, w_mod, b_mod):
    depth, d, n = w_mod.shape
    tn = 1536
    return pl.pallas_call(
        _mod_kernel,
        out_shape=jax.ShapeDtypeStruct((depth, MOD_ROWS, n), F32),
        grid=(depth, n // tn),
        in_specs=[pl.BlockSpec((MOD_ROWS, d), lambda l, j: (0, 0)),
                  pl.BlockSpec((1, d, tn), lambda l, j: (l, 0, j)),
                  pl.BlockSpec((1, 1, tn), lambda l, j: (l, 0, j))],
        out_specs=pl.BlockSpec((1, MOD_ROWS, tn), lambda l, j: (l, 0, j)),
        compiler_params=_cparams(("arbitrary", "arbitrary")),
        name="modulation",
    )(cc, w_mod, b_mod.reshape(depth, 1, n))


def _qk_norm_rope(z, gain, cos, sin, ones_bd, scale):
    ss = _dot01_r2(z * z, ones_bd)
    y = z * lax.rsqrt(ss * (1.0 / HEAD_DIM) + EPS) * gain
    first_half = (_iota((z.shape[0], LANES), 1) % HEAD_DIM) < (HEAD_DIM // 2)
    outs = []
    for cb in range(z.shape[1] // LANES):
        yc = y[:, cb * LANES:(cb + 1) * LANES]
        partner = jnp.where(first_half, pltpu.roll(yc, LANES - HEAD_DIM // 2, 1),
                            pltpu.roll(yc, HEAD_DIM // 2, 1))
        outs.append((yc * cos + partner * sin) * scale)
    return jnp.concatenate(outs, axis=1)


def _residual_specs(xs, tm, ncb, skip_blocks=0):
    d = xs[0].shape[2]
    if len(xs) == 1:
        return [pl.BlockSpec((NB, tm, d), lambda bi, j: (bi, j + skip_blocks, 0))]
    return [pl.BlockSpec((NB, tm, d), lambda bi, j: (bi, jnp.minimum(j + skip_blocks, ncb - 1), 0)),
            pl.BlockSpec((NB, tm, d), lambda bi, j: (bi, jnp.maximum(j + skip_blocks - ncb, 0), 0))]


def _residual_rows(x_refs, ncb, skip_blocks=0):
    if len(x_refs) == 1:
        return x_refs[0][0]
    return jnp.where(pl.program_id(1) + skip_blocks < ncb, x_refs[0][0], x_refs[1][0])


def _proj_kernel(*refs, n_x, ncb):
    x_refs = refs[:n_x]
    (mod_ref, g1_ref, w_ref, gains_ref, cos_ref, sin_ref, wg_ref, bg_ref, bm_ref,
     q_ref, k_ref, vt_ref, zc_ref, zd_ref) = refs[n_x:]
    d = x_refs[0].shape[2]
    tm = x_refs[0].shape[1]
    x = _residual_rows(x_refs, ncb)
    mod = mod_ref[0, 0]
    sh1 = mod[:, 0:d]
    sc1 = mod[:, d:2 * d]
    ms = jnp.mean(x * x, axis=-1, keepdims=True)
    h = (x * lax.rsqrt(ms + EPS)) * g1_ref[...]
    hb = (h * (1.0 + sc1) + sh1).astype(BF16)

    ones_bd = _group_ones(GROUP_WIDTH, HEAD_DIM)
    cos = cos_ref[...]
    sin = sin_ref[...]
    gains = gains_ref[...]
    gw = GROUP_WIDTH

    z = _dot(hb, w_ref[:, 0:W_Q + 2 * W_KV])
    q_scale = HEAD_DIM ** -0.5 * LOG2E
    for n in range(2):
        q = _qk_norm_rope(z[:, n * gw:(n + 1) * gw], gains[n:n + 1], cos, sin, ones_bd, q_scale)
        q_ref[0, :, n * gw:(n + 1) * gw] = q.astype(BF16)
    k_ref[0] = _qk_norm_rope(z[:, W_Q:W_Q + W_KV], gains[2:3], cos, sin, ones_bd, 1.0).astype(BF16)
    vt_ref[0] = z[:, W_Q + W_KV:W_Q + 2 * W_KV].T.astype(BF16)

    base = W_Q + 2 * W_KV
    z = _dot(hb, w_ref[:, base:base + W_C])
    zc_ref[0, :, 0:GLA_KW] = z[:, 0:GLA_KW] * (GLA_DK ** -0.5)
    zc_ref[0, :, GLA_KW:768] = z[:, GLA_KW:768]
    pre = _dot(z[:, 768:896].astype(BF16), wg_ref[...]) + bg_ref[...]
    zc_ref[0, :, 768:1024] = _log_sigmoid(pre) * (1.0 / GLA_GATE_TAU)

    z = _dot(hb, w_ref[:, base + W_C:W_ALL])
    zd_ref[0, :, 0:gw] = z[:, 0:gw]
    zd_ref[0, :, gw:2 * gw] = z[:, gw:2 * gw] * (HEAD_DIM ** -0.5)
    zd_ref[0, :, 2 * gw:4 * gw] = z[:, 2 * gw:4 * gw]
    gates = z[:, 4 * gw:4 * gw + LANES] + bm_ref[...]
    is_input_gate = _iota((tm, LANES), 1) < 2 * GROUP_HEADS
    zd_ref[0, :, 4 * gw:4 * gw + LANES] = jnp.where(is_input_gate, gates, _log_sigmoid(gates))


def _projection(xs, modsel, g1, w_big, gains, cos_t, sin_t, wg, bg, bm, tm, ncb):
    b, _, d = xs[0].shape
    t = sum(a.shape[1] for a in xs)
    nblk = t // tm
    const = lambda bi, j: (0, 0)
    return pl.pallas_call(
        _per_batch_row(functools.partial(_proj_kernel, n_x=len(xs), ncb=ncb), "b" * (len(xs) + 1) + "-" * 8 + "b" * 5),
        out_shape=(jax.ShapeDtypeStruct((b, t, W_Q), BF16),
                   jax.ShapeDtypeStruct((b, t, W_KV), BF16),
                   jax.ShapeDtypeStruct((b, W_KV, t), BF16),
                   jax.ShapeDtypeStruct((b, t, 1024), F32),
                   jax.ShapeDtypeStruct((b, t, W_D), F32)),
        grid=(b // NB, nblk),
        in_specs=_residual_specs(xs, tm, ncb) + [
                  pl.BlockSpec((NB, 1, 1, N_MOD * d), lambda bi, j: (bi, jnp.where(j >= ncb, 1, 0), 0, 0)),
                  pl.BlockSpec((1, d), const),
                  pl.BlockSpec((d, W_ALL), const),
                  pl.BlockSpec((3, GROUP_WIDTH), const),
                  pl.BlockSpec((tm, LANES), lambda bi, j: (j, 0)),
                  pl.BlockSpec((tm, LANES), lambda bi, j: (j, 0)),
                  pl.BlockSpec((LANES, 2 * GLA_KW), const),
                  pl.BlockSpec((1, 2 * GLA_KW), const),
                  pl.BlockSpec((1, LANES), const)],
        out_specs=(pl.BlockSpec((NB, tm, W_Q), lambda bi, j: (bi, j, 0)),
                   pl.BlockSpec((NB, tm, W_KV), lambda bi, j: (bi, j, 0)),
                   pl.BlockSpec((NB, W_KV, tm), lambda bi, j: (bi, 0, j)),
                   pl.BlockSpec((NB, tm, 1024), lambda bi, j: (bi, j, 0)),
                   pl.BlockSpec((NB, tm, W_D), lambda bi, j: (bi, j, 0))),
        compiler_params=_cparams(("arbitrary", "arbitrary")),
        name="projection",
    )(*xs, modsel, g1, w_big, gains, cos_t, sin_t, wg, bg, bm)


ONES_ROWS = 16


def _masked_query_tiles(q_ref):
    low = _iota((q_ref.shape[1], LANES), 1) < HEAD_DIM
    tiles = []
    for n in range(GROUP_WIDTH // LANES):
        q128 = q_ref[0, :, n * LANES:(n + 1) * LANES]
        zero = jnp.zeros_like(q128)
        tiles += [jnp.where(low, q128, zero), jnp.where(low, zero, q128)]
    return tiles


def _heads_to_rows(o_t):
    halves = [jnp.concatenate(o_t[2 * n:2 * n + 2], axis=0).T for n in range(len(o_t) // 2)]
    return jnp.concatenate(halves, axis=1)


def _window_stages(sink_ref, q_ref, k_refs, v_refs, o_ref, j, nctx, nblk):
    tq = q_ref.shape[1]
    halo = Q_BLOCK
    n_win = tq + 2 * halo
    n_keys = n_win + k_refs[3].shape[1]
    is_lat = j >= nctx
    off = 4 * tq
    lo_prev = jnp.where(j > nctx, -halo, off)
    lo_cur = jnp.where(is_lat, -halo, off)
    lo_next = jnp.where(jnp.logical_and(is_lat, j + 1 <= nblk - 1), -halo, off)
    row = _iota((n_keys, tq), 0)
    diff = row - halo - _iota((n_keys, tq), 1)
    lo = jnp.where(row < halo, lo_prev, jnp.where(row < halo + tq, lo_cur, lo_next))
    ok = jnp.logical_or(row >= n_win, jnp.logical_and(diff >= lo, diff <= halo))
    keys = jnp.concatenate([r[0] for r in k_refs], axis=0)
    v_t = jnp.concatenate([r[0] for r in v_refs], axis=1)
    ones = jnp.ones((ONES_ROWS, n_keys), BF16)
    tiles = _masked_query_tiles(q_ref)
    yield
    scores = [jnp.where(ok, _dot_nt(keys, qp), NEG) for qp in tiles]
    yield
    o_t = []
    for c, s_t in enumerate(scores):
        sink = sink_ref[ATTN_HEAD_ORDER[c]] * LOG2E
        m = jnp.maximum(jnp.max(s_t, axis=0, keepdims=True), sink)
        kv = c % KV_HEADS
        vx = jnp.concatenate([v_t[kv * HEAD_DIM:(kv + 1) * HEAD_DIM], ones], axis=0)
        acc = _dot(vx, jnp.exp2(s_t - m).astype(BF16))
        o_t.append(acc[0:HEAD_DIM] / (acc[HEAD_DIM:HEAD_DIM + 1] + jnp.exp2(sink - m)))
        yield
    o_ref[0] = _heads_to_rows(o_t).astype(BF16)


def _attention_kernel(sink_ref, qa_ref, qb_ref, ka_ref, vta_ref, kp_ref, kc_ref, kn_ref, kx_ref,
                      vp_ref, vc_ref, vn_ref, vx_ref, oa_ref, ob_ref, *, lc, nctx, nblk, first):
    n_tiles = GROUP_HEADS
    j = pl.program_id(1) + first
    tiles = _masked_query_tiles(qa_ref)

    def attend(n_keys):
        side = _window_stages(sink_ref, qb_ref, (kp_ref, kc_ref, kn_ref, kx_ref),
                              (vp_ref, vc_ref, vn_ref, vx_ref), ob_ref, j, nctx, nblk)
        keys = ka_ref[0, 0:n_keys, :]
        ones = jnp.ones((ONES_ROWS, n_keys), BF16)
        o_t = []
        scores = [_dot_nt(keys, qp) for qp in tiles]
        for c in range(n_tiles):
            s_t = scores[c]
            next(side, None)
            p_t = jnp.exp2(s_t - jnp.max(s_t, axis=0, keepdims=True)).astype(BF16)
            kv = c % KV_HEADS
            vx = jnp.concatenate([vta_ref[0, kv * HEAD_DIM:(kv + 1) * HEAD_DIM, 0:n_keys], ones], axis=0)
            acc = _dot(vx, p_t)
            o_t.append(acc[0:HEAD_DIM] / acc[HEAD_DIM:HEAD_DIM + 1])
        for _ in side:
            pass
        return _heads_to_rows(o_t)

    def latent_queries():
        return attend(ka_ref.shape[1])

    def context_queries():
        return attend(lc)

    o = lax.cond(j >= nctx, latent_queries, context_queries) if first < nctx else latent_queries()
    oa_ref[0] = o.astype(BF16)


def _attention(q, k, vt, sink, lc, tq, context_queries):
    b, t, _ = q.shape
    nblk = t // tq
    nctx = lc // tq
    first = 0 if context_queries else nctx
    per = tq // Q_BLOCK
    prev_blk = lambda j: jnp.maximum(j * per - 1, 0)
    next_blk = lambda j: jnp.minimum((j + 1) * per, nblk * per - 1)
    spec = lambda shape, index: pl.BlockSpec(shape, lambda bi, s: index(bi, s + first))
    k_specs = [spec((NB, Q_BLOCK, LANES), lambda bi, j: (bi, prev_blk(j), 1)),
               spec((NB, tq, LANES), lambda bi, j: (bi, j, 1)),
               spec((NB, Q_BLOCK, LANES), lambda bi, j: (bi, next_blk(j), 1)),
               spec((NB, lc, LANES), lambda bi, j: (bi, 0, 1))]
    v_specs = [spec((NB, LANES, Q_BLOCK), lambda bi, j: (bi, 1, prev_blk(j))),
               spec((NB, LANES, tq), lambda bi, j: (bi, 1, j)),
               spec((NB, LANES, Q_BLOCK), lambda bi, j: (bi, 1, next_blk(j))),
               spec((NB, LANES, lc), lambda bi, j: (bi, 1, 0))]
    out = jax.ShapeDtypeStruct((b, t, GROUP_WIDTH), BF16)
    q_spec = lambda col: spec((NB, tq, GROUP_WIDTH), lambda bi, j: (bi, j, col))
    return pl.pallas_call(
        _per_batch_row(functools.partial(_attention_kernel, lc=lc, nctx=nctx, nblk=nblk, first=first),
                       "-" + "b" * 14),
        out_shape=(out, out),
        grid=(b // NB, nblk - first),
        in_specs=[pl.BlockSpec(memory_space=pltpu.SMEM), q_spec(0), q_spec(1),
                  spec((NB, t, LANES), lambda bi, j: (bi, 0, 0)),
                  spec((NB, LANES, t), lambda bi, j: (bi, 0, 0))] + k_specs + v_specs,
        out_specs=(q_spec(0), q_spec(0)),
        compiler_params=_cparams(("arbitrary", "arbitrary")),
        name="attention",
    )(sink, q, q, k, vt, k, k, k, k, vt, vt, vt, vt)


def _chunk_tri(rows, reverse):
    t = _iota((rows, rows), 0)
    u = _iota((rows, rows), 1)
    same = t // CHUNK == u // CHUNK
    return jnp.logical_and(same, (u >= t) if reverse else (u <= t)).astype(BF16)


def _tile4(x):
    return jnp.concatenate([x, x, x, x], axis=0)


def _rev_block(i, ncb, nblk):
    return jnp.where(i < ncb, ncb - 1 - i, nblk - 1 - (i - ncb))


class _Item:
    pass


def _scan_items(directions, nchunk):
    items = []
    for pos in range(nchunk):
        for d in directions:
            it = _Item()
            it.d = d
            it.c = nchunk - 1 - pos if d.reverse else pos
            it.sl = slice(it.c * CHUNK, (it.c + 1) * CHUNK)
            items.append(it)
    return items


def _gla_stages(qkv_f_ref, la_f_ref, qkv_b_ref, la_b_ref, of_ref, ob_ref, sf_ref, sb_ref):
    @pl.when(pl.program_id(1) == 0)
    def _():
        sf_ref[...] = jnp.zeros_like(sf_ref)
        sb_ref[...] = jnp.zeros_like(sb_ref)

    rows = qkv_f_ref.shape[1]
    same_kd = _iota((GROUP_WIDTH, GLA_KW), 0) // CHUNK == _iota((GROUP_WIDTH, GLA_KW), 1) // GLA_DK
    same_kv = _iota((GROUP_WIDTH, GROUP_WIDTH), 0) // CHUNK == _iota((GROUP_WIDTH, GROUP_WIDTH), 1) // HEAD_DIM
    t_pos = _iota((CHUNK, GROUP_WIDTH), 0)
    s_pos = _iota((CHUNK, GROUP_WIDTH), 1) % CHUNK

    directions = []
    for qkv_ref, la_ref, o_ref, s_ref, reverse in ((qkv_f_ref, la_f_ref, of_ref, sf_ref, False),
                                                   (qkv_b_ref, la_b_ref, ob_ref, sb_ref, True)):
        d = _Item()
        d.reverse, d.o_ref, d.s_ref = reverse, o_ref, s_ref
        d.b = _dot01(_chunk_tri(rows, reverse), la_ref[0])
        d.q = qkv_ref[0, :, 0:GLA_KW]
        d.k = qkv_ref[0, :, GLA_KW:2 * GLA_KW]
        d.v = qkv_ref[0, :, 2 * GLA_KW:2 * GLA_KW + GROUP_WIDTH].astype(BF16)
        d.causal = (s_pos >= t_pos) if reverse else (s_pos <= t_pos)
        d.end = 0 if reverse else CHUNK - 1
        d.state = s_ref[...]
        directions.append(d)
    items = _scan_items(directions, rows // CHUNK)

    yield
    for it in items:
        d = it.d
        bc = d.b[it.sl]
        b_end = bc[d.end:d.end + 1]
        b_mid = bc[CHUNK // 2:CHUNK // 2 + 1]
        qc, kc = d.q[it.sl], d.k[it.sl]
        it.vc = d.v[it.sl]
        it.q_inter = (qc * jnp.exp(bc)).astype(BF16)
        it.q_intra = (qc * jnp.exp(bc - b_mid)).astype(BF16)
        k_intra = (kc * jnp.exp(b_mid - bc)).astype(BF16)
        it.k_state = (kc * jnp.exp(b_end - bc)).astype(BF16)
        it.decay = jnp.exp(b_end)
        it.k_bd = jnp.where(same_kd, _tile4(k_intra), jnp.zeros((), BF16))
        it.v_bd = jnp.where(same_kv, _tile4(it.vc), jnp.zeros((), BF16))
    yield
    for it in items:
        it.scores = jnp.where(it.d.causal, _dot_nt(it.q_intra, it.k_bd), 0.0).astype(BF16)
    yield
    for it in items:
        it.o_intra = _dot(it.scores, it.v_bd)
        it.update = jnp.where(same_kd, _dot_tn(it.vc, it.k_state), 0.0)
    yield
    for it in items:
        it.state_in = it.d.state
        it.d.state = it.d.state * it.decay + it.update
    yield
    for it in items:
        it.d.o_ref[0, it.sl, :] = it.o_intra + _dot_nt(it.q_inter, it.state_in.astype(BF16))
    yield
    for d in directions:
        d.s_ref[...] = d.state


def _expand_heads(g, base, rows):
    hid = _iota((rows, GROUP_WIDTH), 1) // HEAD_DIM
    cols = [jnp.broadcast_to(g[:, base + h:base + h + 1], (rows, GROUP_WIDTH)) for h in range(GROUP_HEADS)]
    return jnp.where(hid == 0, cols[0], jnp.where(hid == 1, cols[1], jnp.where(hid == 2, cols[2], cols[3])))


def _mlstm_stages(qkv_f_ref, gate_f_ref, qkv_b_ref, gate_b_ref, of_ref, ob_ref,
                  cf_ref, nf_ref, mf_ref, cb_ref, nb_ref, mb_ref):
    @pl.when(pl.program_id(1) == 0)
    def _():
        for r in (cf_ref, nf_ref, mf_ref, cb_ref, nb_ref, mb_ref):
            r[...] = jnp.zeros_like(r)

    rows = qkv_f_ref.shape[1]
    gw = GROUP_WIDTH
    same_head = _iota((gw, gw), 0) // HEAD_DIM == _iota((gw, gw), 1) // HEAD_DIM
    ones_bd = same_head.astype(BF16)
    hid = _iota((CHUNK, gw), 1) // HEAD_DIM
    t_pos = _iota((CHUNK, gw), 0)
    s_pos = _iota((CHUNK, gw), 1) % CHUNK
    diag = (s_pos == t_pos).astype(F32)
    ones_cc = jnp.ones((CHUNK, CHUNK), BF16)

    directions = []
    for index, (qkv_ref, gate_ref, o_ref, c_ref, n_ref, m_ref) in enumerate(
            ((qkv_f_ref, gate_f_ref, of_ref, cf_ref, nf_ref, mf_ref),
             (qkv_b_ref, gate_b_ref, ob_ref, cb_ref, nb_ref, mb_ref))):
        d = _Item()
        d.reverse = index == 1
        d.o_ref, d.c_ref, d.n_ref, d.m_ref = o_ref, c_ref, n_ref, m_ref
        g = gate_ref[0]
        i_col = GROUP_HEADS * index
        f_col = 2 * GROUP_HEADS + GROUP_HEADS * index
        f_all = _dot01(_chunk_tri(rows, d.reverse), g)
        d.ig = _expand_heads(g, i_col, rows)
        d.f_cum = _expand_heads(f_all, f_col, rows)
        d.q = qkv_ref[0, :, 0:gw]
        d.k = qkv_ref[0, :, gw:2 * gw]
        d.v = qkv_ref[0, :, 2 * gw:3 * gw].astype(BF16)
        d.causal = (s_pos >= t_pos) if d.reverse else (s_pos <= t_pos)
        d.end = 0 if d.reverse else CHUNK - 1
        d.c_state = c_ref[...]
        d.n_state = n_ref[...]
        d.m_state = m_ref[...]
        directions.append(d)
    items = _scan_items(directions, rows // CHUNK)

    yield
    for it in items:
        d = it.d
        it.fc, it.ic = d.f_cum[it.sl], d.ig[it.sl]
        it.qc, it.kc, it.vc = d.q[it.sl], d.k[it.sl], d.v[it.sl]
        it.qb = it.qc.astype(BF16)
        it.key_term = _dot01(ones_cc, (it.fc - it.ic) * diag)
    yield
    for it in items:
        it.logw = jnp.where(it.d.causal, it.fc - it.key_term, NEG)
        head_max = [jnp.max(it.logw[:, h * HEAD_DIM:(h + 1) * HEAD_DIM], axis=-1, keepdims=True)
                    for h in range(GROUP_HEADS)]
        it.row_max = jnp.where(hid < 2, jnp.where(hid == 0, head_max[0], head_max[1]),
                               jnp.where(hid == 2, head_max[2], head_max[3]))
        it.k_bd = jnp.where(same_head, _tile4(it.kc.astype(BF16)), jnp.zeros((), BF16))
        it.v_bd = jnp.where(same_head, _tile4(it.vc), jnp.zeros((), BF16))
    yield
    for it in items:
        d = it.d
        it.log_inter = it.fc + d.m_state
        it.m_t = jnp.maximum(it.log_inter, it.row_max)
        d.m_state = it.m_new = it.m_t[d.end:d.end + 1]
    yield
    for it in items:
        end = it.d.end
        it.w = jnp.exp(it.logw - it.m_t)
        it.w_inter = jnp.exp(it.log_inter - it.m_t)
        it.decay = it.w_inter[end:end + 1]
        it.k_end = it.kc * jnp.exp(it.fc[end:end + 1] - it.fc + it.ic - it.m_new)
    yield
    for it in items:
        it.qk = _dot_nt(it.qb, it.k_bd) * it.w
    yield
    for it in items:
        it.num = _dot(it.qk.astype(BF16), it.v_bd)
        it.den = _dot01_r(it.qk, ones_bd)
        it.update = jnp.where(same_head, _dot_tn(it.k_end.astype(BF16), it.vc), 0.0)
    yield
    for it in items:
        d = it.d
        it.c_in, it.n_in = d.c_state, d.n_state
        d.c_state = d.c_state * it.decay + it.update
        d.n_state = d.n_state * it.decay + jnp.sum(it.k_end, axis=0, keepdims=True)
    yield
    for it in items:
        num = it.num + it.w_inter * _dot(it.qb, it.c_in.astype(BF16))
        den = it.den + it.w_inter * _dot((it.qc * it.n_in).astype(BF16), ones_bd)
        it.d.o_ref[0, it.sl, :] = num / jnp.maximum(jnp.abs(den), jnp.exp(-it.m_t))
    yield
    for d in directions:
        d.c_ref[...] = d.c_state
        d.n_ref[...] = d.n_state
        d.m_ref[...] = d.m_state


def _scans_kernel(*refs):
    running = []
    for i in range(NB):
        blocks = [r.at[i:i + 1] for r in refs[0:12]]
        state = [r.at[i] for r in refs[12:20]]
        running.append(_gla_stages(*blocks[0:4], *blocks[8:10], *state[0:2]))
        running.append(_mlstm_stages(*blocks[4:8], *blocks[10:12], *state[2:8]))
    while running:
        running = [g for g in running if next(g, StopIteration) is not StopIteration]


def _scans(zc, zd, lc, rows):
    b, t, _ = zc.shape
    nblk = t // rows
    ncb = lc // rows
    gw = GROUP_WIDTH
    fwd = lambda col: (lambda bi, i: (bi, i, col))
    rev = lambda col: (lambda bi, i: (bi, _rev_block(i, ncb, nblk), col))
    gla_qkv_w = 2 * GLA_KW + gw
    gate_col = 4 * gw // LANES
    mlstm_state = [pltpu.VMEM((NB, gw, gw), F32), pltpu.VMEM((NB, 1, gw), F32), pltpu.VMEM((NB, 1, gw), F32)]
    gla_state = [pltpu.VMEM((NB, gw, GLA_KW), F32)]
    out = jax.ShapeDtypeStruct((b, t, gw), F32)
    return pl.pallas_call(
        _scans_kernel,
        out_shape=(out, out, out, out),
        grid=(b // NB, nblk),
        in_specs=[pl.BlockSpec((NB, rows, gla_qkv_w), fwd(0)),
                  pl.BlockSpec((NB, rows, GLA_KW), fwd(6)),
                  pl.BlockSpec((NB, rows, gla_qkv_w), rev(0)),
                  pl.BlockSpec((NB, rows, GLA_KW), rev(7)),
                  pl.BlockSpec((NB, rows, 3 * gw), fwd(0)),
                  pl.BlockSpec((NB, rows, LANES), fwd(gate_col)),
                  pl.BlockSpec((NB, rows, 3 * gw), rev(0)),
                  pl.BlockSpec((NB, rows, LANES), rev(gate_col))],
        out_specs=(pl.BlockSpec((NB, rows, gw), fwd(0)), pl.BlockSpec((NB, rows, gw), rev(0)),
                   pl.BlockSpec((NB, rows, gw), fwd(0)), pl.BlockSpec((NB, rows, gw), rev(0))),
        scratch_shapes=gla_state + gla_state + mlstm_state + mlstm_state,
        compiler_params=_cparams(("arbitrary", "arbitrary")),
        name="scans",
    )(zc, zc, zc, zc, zd, zd, zd, zd)


def _head_rms(o, gain, ones_bd):
    ss = _dot01_r2(o * o, ones_bd)
    return o * lax.rsqrt(ss * (1.0 / HEAD_DIM) + EPS) * gain


def _out_kernel(*refs, n_x, ncb, skip_blocks):
    x_refs = refs[:n_x]
    (mod_ref, oa_ref, ob_ref, gf_ref, gb_ref, r_ref, mf_ref, mb_ref, og_ref,
     gains_ref, g2_ref, wo_ref, w1_ref, w2_ref, y_ref) = refs[n_x:]
    d = x_refs[0].shape[2]
    x = _residual_rows(x_refs, ncb, skip_blocks)
    mod = mod_ref[0, 0]
    ga1 = mod[:, 2 * d:3 * d]
    sh2 = mod[:, 3 * d:4 * d]
    sc2 = mod[:, 4 * d:5 * d]
    ga2 = mod[:, 5 * d:6 * d]
    ones_bd = _group_ones(GROUP_WIDTH, HEAD_DIM)
    gains = gains_ref[...]

    r = r_ref[0]
    o_c = _head_rms(gf_ref[0] + gb_ref[0], gains[0:1], ones_bd) * (r * _sigmoid(r))
    o_d = _sigmoid(og_ref[0]) * _head_rms(mf_ref[0] + mb_ref[0], gains[1:2], ones_bd)
    o_cat = jnp.concatenate([oa_ref[0], ob_ref[0], o_c.astype(BF16), o_d.astype(BF16)], axis=1)
    x = x + ga1 * _dot(o_cat, wo_ref[...])

    ms = jnp.mean(x * x, axis=-1, keepdims=True)
    h = (x * lax.rsqrt(ms + EPS)) * g2_ref[...]
    hb = (h * (1.0 + sc2) + sh2).astype(BF16)
    hid = jnp.maximum(_dot(hb, w1_ref[...]), 0.0)
    hid = (hid * hid).astype(BF16)
    y_ref[0] = x + ga2 * _dot(hid, w2_ref[...])


def _out_mlp(xs, modsel, oa, ob, gf, gb, zc, mf, mb, zd, gains, g2, wo, w1, w2, tm, ncb, skip_blocks):
    b, _, d = xs[0].shape
    t = sum(a.shape[1] for a in xs)
    nblk = t // tm - skip_blocks
    gw = GROUP_WIDTH
    row = lambda col: (lambda bi, j: (bi, j + skip_blocks, col))
    const = lambda bi, j: (0, 0)
    resident = lambda shape: pl.BlockSpec(shape, const, pipeline_mode=pl.Buffered(1))
    return pl.pallas_call(
        _per_batch_row(functools.partial(_out_kernel, n_x=len(xs), ncb=ncb, skip_blocks=skip_blocks),
                       "b" * (len(xs) + 9) + "-" * 5 + "b"),
        out_shape=jax.ShapeDtypeStruct((b, nblk * tm, d), F32),
        grid=(b // NB, nblk),
        in_specs=_residual_specs(xs, tm, ncb, skip_blocks) + [
                  pl.BlockSpec((NB, 1, 1, N_MOD * d),
                               lambda bi, j: (bi, jnp.where(j + skip_blocks >= ncb, 1, 0), 0, 0)),
                  pl.BlockSpec((NB, tm, gw), row(0)),
                  pl.BlockSpec((NB, tm, gw), row(0)),
                  pl.BlockSpec((NB, tm, gw), row(0)),
                  pl.BlockSpec((NB, tm, gw), row(0)),
                  pl.BlockSpec((NB, tm, gw), row(2)),
                  pl.BlockSpec((NB, tm, gw), row(0)),
                  pl.BlockSpec((NB, tm, gw), row(0)),
                  pl.BlockSpec((NB, tm, gw), row(3)),
                  pl.BlockSpec((2, gw), const),
                  pl.BlockSpec((1, d), const),
                  resident((d, d)),
                  resident((d, 4 * d)),
                  resident((4 * d, d))],
        out_specs=pl.BlockSpec((NB, tm, d), lambda bi, j: (bi, j, 0)),
        compiler_params=_cparams(("arbitrary", "arbitrary")),
        name="out_mlp",
    )(*xs, modsel, oa, ob, gf, gb, zc, mf, mb, zd, gains, g2, wo, w1, w2)


def _projection_columns():
    gw, kw = GROUP_WIDTH, KV_HEADS * HEAD_DIM
    splits = (gw, kw, kw, gw, kw, kw, GLA_KW, GLA_KW, gw, gw, 2 * GLA_GATE_RANK,
              gw, gw, gw, gw, 2 * GROUP_HEADS, 2 * GROUP_HEADS)
    offs = np.concatenate([[0], np.cumsum(splits)])
    in_width = int(offs[-1])
    half_split = np.concatenate([np.arange(0, HEAD_DIM, 2), np.arange(1, HEAD_DIM, 2)])
    plain = np.arange(HEAD_DIM)
    cols = []
    mixers = (0, 3)
    for base in mixers:
        cols += [offs[base] + h * HEAD_DIM + half_split for h in ATTN_HEAD_ORDER]
    for base in mixers:
        cols += [offs[base + 1] + g * HEAD_DIM + half_split for g in range(KV_HEADS)]
    for base in mixers:
        cols += [offs[base + 2] + g * HEAD_DIM + plain for g in range(KV_HEADS)]
    cols.append(np.arange(offs[6], offs[11]))
    cols.append(np.full(W_C - (offs[11] - offs[6]), in_width))
    cols.append(np.arange(offs[11], offs[17]))
    cols.append(np.full(W_D - (offs[17] - offs[11]), in_width))
    cols = np.concatenate(cols).astype(np.int32)
    assert cols.shape[0] == W_ALL
    return cols, in_width, half_split


def _rope_tables(seq, lc):
    rows = seq // GRID_W
    row = jnp.repeat(jnp.arange(rows, dtype=F32), GRID_W)
    col = jnp.tile(jnp.arange(GRID_W, dtype=F32), rows)
    n_freq = HEAD_DIM // 4
    inv = ROPE_THETA ** (-jnp.arange(n_freq, dtype=F32) / n_freq)
    ang = jnp.concatenate([row[:, None] * inv, col[:, None] * inv], axis=-1)
    cos, sin = jnp.cos(ang), jnp.sin(ang)
    cos_t = jnp.tile(cos, (1, 4))
    sin_t = jnp.tile(jnp.concatenate([-sin, sin], axis=-1), (1, 2))
    cos_t = jnp.concatenate([jnp.ones((lc, LANES), F32), cos_t], axis=0)
    sin_t = jnp.concatenate([jnp.zeros((lc, LANES), F32), sin_t], axis=0)
    return cos_t, sin_t


def kernel(x, c, ctx, c_ctx, w_mod, b_mod, g_norm1, g_norm2, w_in, g_q_a, g_k_a, g_q_b, g_k_b, sink_b,
           w_gla_gate, b_gla_gate, g_gla_out, b_mlstm_i, b_mlstm_f, g_mlstm_out, w_out, w_mlp1, w_mlp2):
    b, seq, d = x.shape
    lc = ctx.shape[1]
    depth = w_mod.shape[0]
    tm = ROW_TILE
    assert b % NB == 0 and d == 4 * GROUP_WIDTH and lc % tm == 0 and seq % tm == 0 and seq % GRID_W == 0 and b + 1 <= MOD_ROWS

    cols, in_width, half_split = _projection_columns()
    cos_t, sin_t = _rope_tables(seq, lc)
    attn_rows = np.concatenate([h * HEAD_DIM + np.arange(HEAD_DIM) for h in ATTN_HEAD_ORDER])
    out_rows = np.concatenate([attn_rows, GROUP_WIDTH + attn_rows, np.arange(2 * GROUP_WIDTH, d)])

    cc = jnp.zeros((MOD_ROWS, d), F32).at[0:b].set(c).at[b].set(c_ctx)
    mod_all = _modulation(cc, w_mod, b_mod)

    xs = (ctx, x)
    for l in range(depth):
        modsel = jnp.stack([jnp.broadcast_to(mod_all[l, b], (b, N_MOD * d)), mod_all[l, 0:b]], axis=1)
        modsel = modsel.reshape(b, 2, 1, N_MOD * d)

        w_ext = jnp.concatenate([w_in[l], jnp.zeros((d, 1), F32)], axis=1)
        w_big = jnp.take(w_ext, cols, axis=1).astype(BF16)
        tiled = lambda g, n: jnp.tile(g[half_split], n)
        gains_qk = jnp.stack([tiled(g_q_a[l], GROUP_HEADS), tiled(g_q_b[l], GROUP_HEADS),
                              jnp.concatenate([tiled(g_k_a[l], KV_HEADS), tiled(g_k_b[l], KV_HEADS)])])
        wg = jnp.zeros((LANES, 2 * GLA_KW), F32)
        wg = wg.at[0:GLA_GATE_RANK, 0:GLA_KW].set(w_gla_gate[l, 0])
        wg = wg.at[GLA_GATE_RANK:2 * GLA_GATE_RANK, GLA_KW:].set(w_gla_gate[l, 1]).astype(BF16)
        bg = b_gla_gate[l].reshape(1, 2 * GLA_KW)
        bm = jnp.zeros((1, LANES), F32)
        bm = bm.at[0, 0:2 * GROUP_HEADS].set(b_mlstm_i[l].reshape(-1))
        bm = bm.at[0, 2 * GROUP_HEADS:4 * GROUP_HEADS].set(b_mlstm_f[l].reshape(-1))

        q, k, vt, zc, zd = _projection(xs, modsel, g_norm1[l].reshape(1, d), w_big, gains_qk,
                                      cos_t, sin_t, wg, bg, bm, tm, lc // tm)
        last = l == depth - 1
        oa, ob = _attention(q, k, vt, sink_b[l], lc, tq=ROW_TILE, context_queries=not last)
        gf, gb, mf, mb = _scans(zc, zd, lc, tm)

        gains_out = jnp.stack([jnp.tile(g_gla_out[l], 4), jnp.tile(g_mlstm_out[l], 4)])
        xs = (_out_mlp(xs, modsel, oa, ob, gf, gb, zc, mf, mb, zd, gains_out, g_norm2[l].reshape(1, d),
                       w_out[l][out_rows].astype(BF16), w_mlp1[l].astype(BF16), w_mlp2[l].astype(BF16), tm,
                       ncb=lc // tm, skip_blocks=lc // tm if last else 0),)
    return xs[0]
```

```python
import functools

import numpy as np
import jax
import jax.numpy as jnp
from jax import lax
from jax.experimental import pallas as pl
from jax.experimental.pallas import tpu as pltpu

F32 = jnp.float32
BF16 = jnp.bfloat16

HEAD_DIM = 64
GROUP_HEADS = 4
GROUP_WIDTH = GROUP_HEADS * HEAD_DIM
KV_HEADS = 2
GRID_W = 64
Q_BLOCK = 128
ROPE_THETA = 10000.0
GLA_DK = 32
GLA_KW = GROUP_HEADS * GLA_DK
GLA_GATE_RANK = 16
GLA_GATE_TAU = 16.0
N_MOD = 6
EPS = 1e-6
LOG2E = 1.4426950408889634
NEG = -1e30
CHUNK = 64
LANES = 128
MOD_ROWS = 16
VMEM_LIMIT = 56 * 1024 * 1024

ATTN_HEAD_ORDER = (0, 2, 1, 3)
W_Q = 2 * GROUP_WIDTH
W_KV = 2 * KV_HEADS * HEAD_DIM
W_C = 896
W_D = 1152
W_ALL = W_Q + 2 * W_KV + W_C + W_D


NB = 2
ROW_TILE = 256
KEY_CHUNK = 256


def _per_batch_row(body, layout):
    def kernel(*refs):
        assert len(refs) == len(layout)
        for i in range(NB):
            body(*[r.at[i:i + 1] if how == "b" else r for r, how in zip(refs, layout)])
    return kernel


def _cparams(sem):
    return pltpu.CompilerParams(dimension_semantics=sem, vmem_limit_bytes=VMEM_LIMIT)


def _dot(a, b):
    return jnp.dot(a, b, preferred_element_type=F32)


def _dot_nt(a, b):
    return lax.dot_general(a, b, (((1,), (1,)), ((), ())), preferred_element_type=F32)


def _dot_tn(a, b):
    return lax.dot_general(a, b, (((0,), (0,)), ((), ())), preferred_element_type=F32)


def _split3(x):
    hi = x.astype(BF16)
    r1 = x - hi.astype(F32)
    mid = r1.astype(BF16)
    lo = (r1 - mid.astype(F32)).astype(BF16)
    return hi, mid, lo


def _dot01(a01, x):
    hi, mid, lo = _split3(x)
    return _dot(a01, hi) + _dot(a01, mid) + _dot(a01, lo)


def _dot01_r(x, b01):
    hi, mid, lo = _split3(x)
    return _dot(hi, b01) + _dot(mid, b01) + _dot(lo, b01)


def _dot01_r2(x, b01):
    hi = x.astype(BF16)
    lo = (x - hi.astype(F32)).astype(BF16)
    return _dot(hi, b01) + _dot(lo, b01)


def _log_sigmoid(x):
    return jnp.minimum(x, 0.0) - jnp.log(1.0 + jnp.exp(-jnp.abs(x)))


def _sigmoid(x):
    return 1.0 / (1.0 + jnp.exp(-x))


def _iota(shape, dim):
    return lax.broadcasted_iota(jnp.int32, shape, dim)


def _group_ones(n, group):
    return (_iota((n, n), 0) // group == _iota((n, n), 1) // group).astype(BF16)


def _mod_kernel(c_ref, w_ref, b_ref, o_ref):
    c = c_ref[...]
    s = (c * _sigmoid(c)).astype(BF16)
    o_ref[0] = _dot(s, w_ref[0].astype(BF16)) + b_ref[0]


def _modulation(cc, w_mod, b_mod):
    depth, d, n = w_mod.shape
    tn = 1536
    return pl.pallas_call(
        _mod_kernel,
        out_shape=jax.ShapeDtypeStruct((depth, MOD_ROWS, n), F32),
        grid=(depth, n // tn),
        in_specs=[pl.BlockSpec((MOD_ROWS, d), lambda l, j: (0, 0)),
                  pl.BlockSpec((1, d, tn), lambda l, j: (l, 0, j)),
                  pl.BlockSpec((1, 1, tn), lambda l, j: (l, 0, j))],
        out_specs=pl.BlockSpec((1, MOD_ROWS, tn), lambda l, j: (l, 0, j)),
        compiler_params=_cparams(("arbitrary", "arbitrary")),
        name="modulation",
    )(cc, w_mod, b_mod.reshape(depth, 1, n))


def _qk_norm_rope(z, gain, cos, sin, ones_bd, scale):
    ss = _dot01_r2(z * z, ones_bd)
    y = z * lax.rsqrt(ss * (1.0 / HEAD_DIM) + EPS) * gain
    first_half = (_iota((z.shape[0], LANES), 1) % HEAD_DIM) < (HEAD_DIM // 2)
    outs = []
    for cb in range(z.shape[1] // LANES):
        yc = y[:, cb * LANES:(cb + 1) * LANES]
        partner = jnp.where(first_half, pltpu.roll(yc, LANES - HEAD_DIM // 2, 1),
                            pltpu.roll(yc, HEAD_DIM // 2, 1))
        outs.append((yc * cos + partner * sin) * scale)
    return jnp.concatenate(outs, axis=1)


def _residual_specs(xs, tm, ncb, skip_blocks=0):
    d = xs[0].shape[2]
    if len(xs) == 1:
        return [pl.BlockSpec((NB, tm, d), lambda bi, j: (bi, j + skip_blocks, 0))]
    return [pl.BlockSpec((NB, tm, d), lambda bi, j: (bi, jnp.minimum(j + skip_blocks, ncb - 1), 0)),
            pl.BlockSpec((NB, tm, d), lambda bi, j: (bi, jnp.maximum(j + skip_blocks - ncb, 0), 0))]


def _residual_rows(x_refs, ncb, skip_blocks=0):
    if len(x_refs) == 1:
        return x_refs[0][0]
    return jnp.where(pl.program_id(1) + skip_blocks < ncb, x_refs[0][0], x_refs[1][0])


def _proj_kernel(*refs, n_x, ncb):
    x_refs = refs[:n_x]
    (mod_ref, g1_ref, w_ref, gains_ref, cos_ref, sin_ref, wg_ref, bg_ref, bm_ref,
     q_ref, k_ref, vt_ref, zc_ref, zd_ref) = refs[n_x:]
    d = x_refs[0].shape[2]
    tm = x_refs[0].shape[1]
    x = _residual_rows(x_refs, ncb)
    mod = mod_ref[0, 0]
    sh1 = mod[:, 0:d]
    sc1 = mod[:, d:2 * d]
    ms = jnp.mean(x * x, axis=-1, keepdims=True)
    h = (x * lax.rsqrt(ms + EPS)) * g1_ref[...]
    hb = (h * (1.0 + sc1) + sh1).astype(BF16)

    ones_bd = _group_ones(GROUP_WIDTH, HEAD_DIM)
    cos = cos_ref[...]
    sin = sin_ref[...]
    gains = gains_ref[...]
    gw = GROUP_WIDTH

    z = _dot(hb, w_ref[:, 0:W_Q + 2 * W_KV])
    q_scale = HEAD_DIM ** -0.5 * LOG2E
    for n in range(2):
        q = _qk_norm_rope(z[:, n * gw:(n + 1) * gw], gains[n:n + 1], cos, sin, ones_bd, q_scale)
        q_ref[0, :, n * gw:(n + 1) * gw] = q.astype(BF16)
    k_ref[0] = _qk_norm_rope(z[:, W_Q:W_Q + W_KV], gains[2:3], cos, sin, ones_bd, 1.0).astype(BF16)
    vt_ref[0] = z[:, W_Q + W_KV:W_Q + 2 * W_KV].T.astype(BF16)

    base = W_Q + 2 * W_KV
    z = _dot(hb, w_ref[:, base:base + W_C])
    zc_ref[0, :, 0:GLA_KW] = z[:, 0:GLA_KW] * (GLA_DK ** -0.5)
    zc_ref[0, :, GLA_KW:768] = z[:, GLA_KW:768]
    pre = _dot(z[:, 768:896].astype(BF16), wg_ref[...]) + bg_ref[...]
    zc_ref[0, :, 768:1024] = _log_sigmoid(pre) * (1.0 / GLA_GATE_TAU)

    z = _dot(hb, w_ref[:, base + W_C:W_ALL])
    zd_ref[0, :, 0:gw] = z[:, 0:gw]
    zd_ref[0, :, gw:2 * gw] = z[:, gw:2 * gw] * (HEAD_DIM ** -0.5)
    zd_ref[0, :, 2 * gw:4 * gw] = z[:, 2 * gw:4 * gw]
    gates = z[:, 4 * gw:4 * gw + LANES] + bm_ref[...]
    is_input_gate = _iota((tm, LANES), 1) < 2 * GROUP_HEADS
    zd_ref[0, :, 4 * gw:4 * gw + LANES] = jnp.where(is_input_gate, gates, _log_sigmoid(gates))


def _projection(xs, modsel, g1, w_big, gains, cos_t, sin_t, wg, bg, bm, tm, ncb):
    b, _, d = xs[0].shape
    t = sum(a.shape[1] for a in xs)
    nblk = t // tm
    const = lambda bi, j: (0, 0)
    return pl.pallas_call(
        _per_batch_row(functools.partial(_proj_kernel, n_x=len(xs), ncb=ncb), "b" * (len(xs) + 1) + "-" * 8 + "b" * 5),
        out_shape=(jax.ShapeDtypeStruct((b, t, W_Q), BF16),
                   jax.ShapeDtypeStruct((b, t, W_KV), BF16),
                   jax.ShapeDtypeStruct((b, W_KV, t), BF16),
                   jax.ShapeDtypeStruct((b, t, 1024), F32),
                   jax.ShapeDtypeStruct((b, t, W_D), F32)),
        grid=(b // NB, nblk),
        in_specs=_residual_specs(xs, tm, ncb) + [
                  pl.BlockSpec((NB, 1, 1, N_MOD * d), lambda bi, j: (bi, jnp.where(j >= ncb, 1, 0), 0, 0)),
                  pl.BlockSpec((1, d), const),
                  pl.BlockSpec((d, W_ALL), const),
                  pl.BlockSpec((3, GROUP_WIDTH), const),
                  pl.BlockSpec((tm, LANES), lambda bi, j: (j, 0)),
                  pl.BlockSpec((tm, LANES), lambda bi, j: (j, 0)),
                  pl.BlockSpec((LANES, 2 * GLA_KW), const),
                  pl.BlockSpec((1, 2 * GLA_KW), const),
                  pl.BlockSpec((1, LANES), const)],
        out_specs=(pl.BlockSpec((NB, tm, W_Q), lambda bi, j: (bi, j, 0)),
                   pl.BlockSpec((NB, tm, W_KV), lambda bi, j: (bi, j, 0)),
                   pl.BlockSpec((NB, W_KV, tm), lambda bi, j: (bi, 0, j)),
                   pl.BlockSpec((NB, tm, 1024), lambda bi, j: (bi, j, 0)),
                   pl.BlockSpec((NB, tm, W_D), lambda bi, j: (bi, j, 0))),
        compiler_params=_cparams(("arbitrary", "arbitrary")),
        name="projection",
    )(*xs, modsel, g1, w_big, gains, cos_t, sin_t, wg, bg, bm)


ONES_ROWS = 16


def _masked_query_tiles(q_ref):
    low = _iota((q_ref.shape[1], LANES), 1) < HEAD_DIM
    tiles = []
    for n in range(GROUP_WIDTH // LANES):
        q128 = q_ref[0, :, n * LANES:(n + 1) * LANES]
        zero = jnp.zeros_like(q128)
        tiles += [jnp.where(low, q128, zero), jnp.where(low, zero, q128)]
    return tiles


def _heads_to_rows(o_t):
    halves = [jnp.concatenate(o_t[2 * n:2 * n + 2], axis=0).T for n in range(len(o_t) // 2)]
    return jnp.concatenate(halves, axis=1)


def _window_stages(sink_ref, q_ref, k_refs, v_refs, o_ref, j, nctx, nblk):
    tq = q_ref.shape[1]
    halo = Q_BLOCK
    n_win = tq + 2 * halo
    n_keys = n_win + k_refs[3].shape[1]
    is_lat = j >= nctx
    off = 4 * tq
    lo_prev = jnp.where(j > nctx, -halo, off)
    lo_cur = jnp.where(is_lat, -halo, off)
    lo_next = jnp.where(jnp.logical_and(is_lat, j + 1 <= nblk - 1), -halo, off)
    row = _iota((n_keys, tq), 0)
    diff = row - halo - _iota((n_keys, tq), 1)
    lo = jnp.where(row < halo, lo_prev, jnp.where(row < halo + tq, lo_cur, lo_next))
    ok = jnp.logical_or(row >= n_win, jnp.logical_and(diff >= lo, diff <= halo))
    keys = jnp.concatenate([r[0] for r in k_refs], axis=0)
    v_t = jnp.concatenate([r[0] for r in v_refs], axis=1)
    ones = jnp.ones((ONES_ROWS, n_keys), BF16)
    tiles = _masked_query_tiles(q_ref)
    yield
    scores = [jnp.where(ok, _dot_nt(keys, qp), NEG) for qp in tiles]
    yield
    o_t = []
    for c, s_t in enumerate(scores):
        sink = sink_ref[ATTN_HEAD_ORDER[c]] * LOG2E
        m = jnp.maximum(jnp.max(s_t, axis=0, keepdims=True), sink)
        kv = c % KV_HEADS
        vx = jnp.concatenate([v_t[kv * HEAD_DIM:(kv + 1) * HEAD_DIM], ones], axis=0)
        acc = _dot(vx, jnp.exp2(s_t - m).astype(BF16))
        o_t.append(acc[0:HEAD_DIM] / (acc[HEAD_DIM:HEAD_DIM + 1] + jnp.exp2(sink - m)))
        yield
    o_ref[0] = _heads_to_rows(o_t).astype(BF16)


def _attention_kernel(sink_ref, qa_ref, qb_ref, ka_ref, vta_ref, kp_ref, kc_ref, kn_ref, kx_ref,
                      vp_ref, vc_ref, vn_ref, vx_ref, oa_ref, ob_ref, *, lc, tk, nk, nctx, nblk, first):
    n_tiles = GROUP_HEADS
    j = pl.program_id(1) + first
    tiles = _masked_query_tiles(qa_ref)
    ones = jnp.ones((ONES_ROWS, tk), BF16)

    def window():
        return _window_stages(sink_ref, qb_ref, (kp_ref, kc_ref, kn_ref, kx_ref),
                              (vp_ref, vc_ref, vn_ref, vx_ref), ob_ref, j, nctx, nblk)

    def scores(rows):
        k = ka_ref[0, rows, :]
        return [_dot_nt(k, qp) for qp in tiles]

    def chunk(s_t, rows, m, accs):
        m_out, acc_out = [], []
        for c in range(n_tiles):
            col_max = jnp.max(s_t[c], axis=0, keepdims=True)
            m_new = col_max if m is None else jnp.maximum(m[c], col_max)
            p_t = jnp.exp2(s_t[c] - m_new).astype(BF16)
            kv = c % KV_HEADS
            vx = jnp.concatenate([vta_ref[0, kv * HEAD_DIM:(kv + 1) * HEAD_DIM, rows], ones], axis=0)
            upd = _dot(vx, p_t)
            m_out.append(m_new)
            acc_out.append(upd if m is None else jnp.exp2(m[c] - m_new) * accs[c] + upd)
        return m_out, acc_out

    def key_chunks(first_chunk, count, m, accs, side=None):
        rows = [slice((first_chunk + c) * tk, (first_chunk + c + 1) * tk) for c in range(count)]
        s_next = scores(rows[0])
        for c in range(count):
            s_t = s_next
            if c + 1 < count:
                s_next = scores(rows[c + 1])
            if side is not None:
                next(side, None)
            m, accs = chunk(s_t, rows[c], m, accs)
        return m, accs

    n_ctx = lc // tk
    m, accs = key_chunks(0, n_ctx, None, None)

    def finish(accs):
        return _heads_to_rows([a[0:HEAD_DIM] / a[HEAD_DIM:HEAD_DIM + 1] for a in accs])

    def latent_queries():
        side = window()
        out = finish(key_chunks(n_ctx, nk, m, accs, side)[1])
        for _ in side:
            pass
        return out

    def context_queries():
        for _ in window():
            pass
        return finish(accs)

    o = lax.cond(j >= nctx, latent_queries, context_queries) if first < nctx else latent_queries()
    oa_ref[0] = o.astype(BF16)


def _attention(q, k, vt, sink, lc, tq, tk, context_queries):
    b, t, _ = q.shape
    nblk = t // tq
    nctx = lc // tq
    nk = (t - lc) // tk
    assert lc % tk == 0 and (t - lc) % tk == 0
    first = 0 if context_queries else nctx
    per = tq // Q_BLOCK
    prev_blk = lambda j: jnp.maximum(j * per - 1, 0)
    next_blk = lambda j: jnp.minimum((j + 1) * per, nblk * per - 1)
    spec = lambda shape, index: pl.BlockSpec(shape, lambda bi, s: index(bi, s + first))
    k_specs = [spec((NB, Q_BLOCK, LANES), lambda bi, j: (bi, prev_blk(j), 1)),
               spec((NB, tq, LANES), lambda bi, j: (bi, j, 1)),
               spec((NB, Q_BLOCK, LANES), lambda bi, j: (bi, next_blk(j), 1)),
               spec((NB, lc, LANES), lambda bi, j: (bi, 0, 1))]
    v_specs = [spec((NB, LANES, Q_BLOCK), lambda bi, j: (bi, 1, prev_blk(j))),
               spec((NB, LANES, tq), lambda bi, j: (bi, 1, j)),
               spec((NB, LANES, Q_BLOCK), lambda bi, j: (bi, 1, next_blk(j))),
               spec((NB, LANES, lc), lambda bi, j: (bi, 1, 0))]
    out = jax.ShapeDtypeStruct((b, t - first * tq, GROUP_WIDTH), BF16)
    out_spec = pl.BlockSpec((NB, tq, GROUP_WIDTH), lambda bi, s: (bi, s, 0))
    q_spec = lambda col: spec((NB, tq, GROUP_WIDTH), lambda bi, j: (bi, j, col))
    return pl.pallas_call(
        _per_batch_row(functools.partial(_attention_kernel, lc=lc, tk=tk, nk=nk, nctx=nctx, nblk=nblk, first=first),
                       "-" + "b" * 14),
        out_shape=(out, out),
        grid=(b // NB, nblk - first),
        in_specs=[pl.BlockSpec(memory_space=pltpu.SMEM), q_spec(0), q_spec(1),
                  spec((NB, t, LANES), lambda bi, j: (bi, 0, 0)),
                  spec((NB, LANES, t), lambda bi, j: (bi, 0, 0))] + k_specs + v_specs,
        out_specs=(out_spec, out_spec),
        compiler_params=_cparams(("arbitrary", "arbitrary")),
        name="attention",
    )(sink, q, q, k, vt, k, k, k, k, vt, vt, vt, vt)


def _chunk_tri(rows, reverse):
    t = _iota((rows, rows), 0)
    u = _iota((rows, rows), 1)
    same = t // CHUNK == u // CHUNK
    return jnp.logical_and(same, (u >= t) if reverse else (u <= t)).astype(BF16)


def _tile4(x):
    return jnp.concatenate([x, x, x, x], axis=0)


def _rev_block(i, ncb, nblk):
    return jnp.where(i < ncb, ncb - 1 - i, nblk - 1 - (i - ncb))


class _Item:
    pass


def _scan_items(directions, nchunk):
    items = []
    for pos in range(nchunk):
        for d in directions:
            it = _Item()
            it.d = d
            it.c = nchunk - 1 - pos if d.reverse else pos
            it.sl = slice(it.c * CHUNK, (it.c + 1) * CHUNK)
            items.append(it)
    return items


def _gla_stages(qkv_f_ref, la_f_ref, qkv_b_ref, la_b_ref, of_ref, ob_ref, sf_ref, sb_ref):
    @pl.when(pl.program_id(1) == 0)
    def _():
        sf_ref[...] = jnp.zeros_like(sf_ref)
        sb_ref[...] = jnp.zeros_like(sb_ref)

    rows = qkv_f_ref.shape[1]
    same_kd = _iota((GROUP_WIDTH, GLA_KW), 0) // CHUNK == _iota((GROUP_WIDTH, GLA_KW), 1) // GLA_DK
    same_kv = _iota((GROUP_WIDTH, GROUP_WIDTH), 0) // CHUNK == _iota((GROUP_WIDTH, GROUP_WIDTH), 1) // HEAD_DIM
    t_pos = _iota((CHUNK, GROUP_WIDTH), 0)
    s_pos = _iota((CHUNK, GROUP_WIDTH), 1) % CHUNK

    directions = []
    for qkv_ref, la_ref, o_ref, s_ref, reverse in ((qkv_f_ref, la_f_ref, of_ref, sf_ref, False),
                                                   (qkv_b_ref, la_b_ref, ob_ref, sb_ref, True)):
        d = _Item()
        d.reverse, d.o_ref, d.s_ref = reverse, o_ref, s_ref
        d.b = _dot01(_chunk_tri(rows, reverse), la_ref[0])
        d.q = qkv_ref[0, :, 0:GLA_KW]
        d.k = qkv_ref[0, :, GLA_KW:2 * GLA_KW]
        d.v = qkv_ref[0, :, 2 * GLA_KW:2 * GLA_KW + GROUP_WIDTH].astype(BF16)
        d.causal = (s_pos >= t_pos) if reverse else (s_pos <= t_pos)
        d.end = 0 if reverse else CHUNK - 1
        d.state = s_ref[...]
        directions.append(d)
    items = _scan_items(directions, rows // CHUNK)

    yield
    for it in items:
        d = it.d
        bc = d.b[it.sl]
        b_end = bc[d.end:d.end + 1]
        b_mid = bc[CHUNK // 2:CHUNK // 2 + 1]
        qc, kc = d.q[it.sl], d.k[it.sl]
        it.vc = d.v[it.sl]
        it.q_inter = (qc * jnp.exp(bc)).astype(BF16)
        it.q_intra = (qc * jnp.exp(bc - b_mid)).astype(BF16)
        k_intra = (kc * jnp.exp(b_mid - bc)).astype(BF16)
        it.k_state = (kc * jnp.exp(b_end - bc)).astype(BF16)
        it.decay = jnp.exp(b_end)
        it.k_bd = jnp.where(same_kd, _tile4(k_intra), jnp.zeros((), BF16))
        it.v_bd = jnp.where(same_kv, _tile4(it.vc), jnp.zeros((), BF16))
    yield
    for it in items:
        it.scores = jnp.where(it.d.causal, _dot_nt(it.q_intra, it.k_bd), 0.0).astype(BF16)
    yield
    for it in items:
        it.o_intra = _dot(it.scores, it.v_bd)
        it.update = jnp.where(same_kd, _dot_tn(it.vc, it.k_state), 0.0)
    yield
    for it in items:
        it.state_in = it.d.state
        it.d.state = it.d.state * it.decay + it.update
    yield
    for it in items:
        it.d.o_ref[0, it.sl, :] = it.o_intra + _dot_nt(it.q_inter, it.state_in.astype(BF16))
    yield
    for d in directions:
        d.s_ref[...] = d.state


def _expand_heads(g, base, rows):
    hid = _iota((rows, GROUP_WIDTH), 1) // HEAD_DIM
    cols = [jnp.broadcast_to(g[:, base + h:base + h + 1], (rows, GROUP_WIDTH)) for h in range(GROUP_HEADS)]
    return jnp.where(hid == 0, cols[0], jnp.where(hid == 1, cols[1], jnp.where(hid == 2, cols[2], cols[3])))


def _mlstm_stages(qkv_f_ref, gate_f_ref, qkv_b_ref, gate_b_ref, of_ref, ob_ref,
                  cf_ref, nf_ref, mf_ref, cb_ref, nb_ref, mb_ref):
    @pl.when(pl.program_id(1) == 0)
    def _():
        for r in (cf_ref, nf_ref, mf_ref, cb_ref, nb_ref, mb_ref):
            r[...] = jnp.zeros_like(r)

    rows = qkv_f_ref.shape[1]
    gw = GROUP_WIDTH
    same_head = _iota((gw, gw), 0) // HEAD_DIM == _iota((gw, gw), 1) // HEAD_DIM
    ones_bd = same_head.astype(BF16)
    hid = _iota((CHUNK, gw), 1) // HEAD_DIM
    t_pos = _iota((CHUNK, gw), 0)
    s_pos = _iota((CHUNK, gw), 1) % CHUNK
    diag = (s_pos == t_pos).astype(F32)
    ones_cc = jnp.ones((CHUNK, CHUNK), BF16)

    directions = []
    for index, (qkv_ref, gate_ref, o_ref, c_ref, n_ref, m_ref) in enumerate(
            ((qkv_f_ref, gate_f_ref, of_ref, cf_ref, nf_ref, mf_ref),
             (qkv_b_ref, gate_b_ref, ob_ref, cb_ref, nb_ref, mb_ref))):
        d = _Item()
        d.reverse = index == 1
        d.o_ref, d.c_ref, d.n_ref, d.m_ref = o_ref, c_ref, n_ref, m_ref
        g = gate_ref[0]
        i_col = GROUP_HEADS * index
        f_col = 2 * GROUP_HEADS + GROUP_HEADS * index
        f_all = _dot01(_chunk_tri(rows, d.reverse), g)
        d.ig = _expand_heads(g, i_col, rows)
        d.f_cum = _expand_heads(f_all, f_col, rows)
        d.q = qkv_ref[0, :, 0:gw]
        d.k = qkv_ref[0, :, gw:2 * gw]
        d.v = qkv_ref[0, :, 2 * gw:3 * gw].astype(BF16)
        d.causal = (s_pos >= t_pos) if d.reverse else (s_pos <= t_pos)
        d.end = 0 if d.reverse else CHUNK - 1
        d.c_state = c_ref[...]
        d.n_state = n_ref[...]
        d.m_state = m_ref[...]
        directions.append(d)
    items = _scan_items(directions, rows // CHUNK)

    yield
    for it in items:
        d = it.d
        it.fc, it.ic = d.f_cum[it.sl], d.ig[it.sl]
        it.qc, it.kc, it.vc = d.q[it.sl], d.k[it.sl], d.v[it.sl]
        it.qb = it.qc.astype(BF16)
        it.key_term = _dot01(ones_cc, (it.fc - it.ic) * diag)
    yield
    for it in items:
        it.logw = jnp.where(it.d.causal, it.fc - it.key_term, NEG)
        head_max = [jnp.max(it.logw[:, h * HEAD_DIM:(h + 1) * HEAD_DIM], axis=-1, keepdims=True)
                    for h in range(GROUP_HEADS)]
        it.row_max = jnp.where(hid < 2, jnp.where(hid == 0, head_max[0], head_max[1]),
                               jnp.where(hid == 2, head_max[2], head_max[3]))
        it.k_bd = jnp.where(same_head, _tile4(it.kc.astype(BF16)), jnp.zeros((), BF16))
        it.v_bd = jnp.where(same_head, _tile4(it.vc), jnp.zeros((), BF16))
    yield
    for it in items:
        d = it.d
        it.log_inter = it.fc + d.m_state
        it.m_t = jnp.maximum(it.log_inter, it.row_max)
        d.m_state = it.m_new = it.m_t[d.end:d.end + 1]
    yield
    for it in items:
        end = it.d.end
        it.w = jnp.exp(it.logw - it.m_t)
        it.w_inter = jnp.exp(it.log_inter - it.m_t)
        it.decay = it.w_inter[end:end + 1]
        it.k_end = it.kc * jnp.exp(it.fc[end:end + 1] - it.fc + it.ic - it.m_new)
    yield
    for it in items:
        it.qk = _dot_nt(it.qb, it.k_bd) * it.w
    yield
    for it in items:
        it.num = _dot(it.qk.astype(BF16), it.v_bd)
        it.den = _dot01_r(it.qk, ones_bd)
        it.update = jnp.where(same_head, _dot_tn(it.k_end.astype(BF16), it.vc), 0.0)
    yield
    for it in items:
        d = it.d
        it.c_in, it.n_in = d.c_state, d.n_state
        d.c_state = d.c_state * it.decay + it.update
        d.n_state = d.n_state * it.decay + jnp.sum(it.k_end, axis=0, keepdims=True)
    yield
    for it in items:
        num = it.num + it.w_inter * _dot(it.qb, it.c_in.astype(BF16))
        den = it.den + it.w_inter * _dot((it.qc * it.n_in).astype(BF16), ones_bd)
        it.d.o_ref[0, it.sl, :] = num / jnp.maximum(jnp.abs(den), jnp.exp(-it.m_t))
    yield
    for d in directions:
        d.c_ref[...] = d.c_state
        d.n_ref[...] = d.n_state
        d.m_ref[...] = d.m_state


def _scans_kernel(*refs):
    running = []
    for i in range(NB):
        blocks = [r.at[i:i + 1] for r in refs[0:12]]
        state = [r.at[i] for r in refs[12:20]]
        running.append(_gla_stages(*blocks[0:4], *blocks[8:10], *state[0:2]))
        running.append(_mlstm_stages(*blocks[4:8], *blocks[10:12], *state[2:8]))
    while running:
        running = [g for g in running if next(g, StopIteration) is not StopIteration]


def _scans(zc, zd, lc, rows):
    b, t, _ = zc.shape
    nblk = t // rows
    ncb = lc // rows
    gw = GROUP_WIDTH
    fwd = lambda col: (lambda bi, i: (bi, i, col))
    rev = lambda col: (lambda bi, i: (bi, _rev_block(i, ncb, nblk), col))
    gla_qkv_w = 2 * GLA_KW + gw
    gate_col = 4 * gw // LANES
    mlstm_state = [pltpu.VMEM((NB, gw, gw), F32), pltpu.VMEM((NB, 1, gw), F32), pltpu.VMEM((NB, 1, gw), F32)]
    gla_state = [pltpu.VMEM((NB, gw, GLA_KW), F32)]
    out = jax.ShapeDtypeStruct((b, t, gw), F32)
    return pl.pallas_call(
        _scans_kernel,
        out_shape=(out, out, out, out),
        grid=(b // NB, nblk),
        in_specs=[pl.BlockSpec((NB, rows, gla_qkv_w), fwd(0)),
                  pl.BlockSpec((NB, rows, GLA_KW), fwd(6)),
                  pl.BlockSpec((NB, rows, gla_qkv_w), rev(0)),
                  pl.BlockSpec((NB, rows, GLA_KW), rev(7)),
                  pl.BlockSpec((NB, rows, 3 * gw), fwd(0)),
                  pl.BlockSpec((NB, rows, LANES), fwd(gate_col)),
                  pl.BlockSpec((NB, rows, 3 * gw), rev(0)),
                  pl.BlockSpec((NB, rows, LANES), rev(gate_col))],
        out_specs=(pl.BlockSpec((NB, rows, gw), fwd(0)), pl.BlockSpec((NB, rows, gw), rev(0)),
                   pl.BlockSpec((NB, rows, gw), fwd(0)), pl.BlockSpec((NB, rows, gw), rev(0))),
        scratch_shapes=gla_state + gla_state + mlstm_state + mlstm_state,
        compiler_params=_cparams(("arbitrary", "arbitrary")),
        name="scans",
    )(zc, zc, zc, zc, zd, zd, zd, zd)


def _head_rms(o, gain, ones_bd):
    ss = _dot01_r2(o * o, ones_bd)
    return o * lax.rsqrt(ss * (1.0 / HEAD_DIM) + EPS) * gain


def _out_kernel(*refs, n_x, ncb, skip_blocks):
    x_refs = refs[:n_x]
    (mod_ref, oa_ref, ob_ref, gf_ref, gb_ref, r_ref, mf_ref, mb_ref, og_ref,
     gains_ref, g2_ref, wo_ref, w1_ref, w2_ref, y_ref) = refs[n_x:]
    d = x_refs[0].shape[2]
    x = _residual_rows(x_refs, ncb, skip_blocks)
    mod = mod_ref[0, 0]
    ga1 = mod[:, 2 * d:3 * d]
    sh2 = mod[:, 3 * d:4 * d]
    sc2 = mod[:, 4 * d:5 * d]
    ga2 = mod[:, 5 * d:6 * d]
    ones_bd = _group_ones(GROUP_WIDTH, HEAD_DIM)
    gains = gains_ref[...]

    r = r_ref[0]
    o_c = _head_rms(gf_ref[0] + gb_ref[0], gains[0:1], ones_bd) * (r * _sigmoid(r))
    o_d = _sigmoid(og_ref[0]) * _head_rms(mf_ref[0] + mb_ref[0], gains[1:2], ones_bd)
    o_cat = jnp.concatenate([oa_ref[0], ob_ref[0], o_c.astype(BF16), o_d.astype(BF16)], axis=1)
    x = x + ga1 * _dot(o_cat, wo_ref[...])

    ms = jnp.mean(x * x, axis=-1, keepdims=True)
    h = (x * lax.rsqrt(ms + EPS)) * g2_ref[...]
    hb = (h * (1.0 + sc2) + sh2).astype(BF16)
    hid = jnp.maximum(_dot(hb, w1_ref[...]), 0.0)
    hid = (hid * hid).astype(BF16)
    y_ref[0] = x + ga2 * _dot(hid, w2_ref[...])


def _out_mlp(xs, modsel, oa, ob, gf, gb, zc, mf, mb, zd, gains, g2, wo, w1, w2, tm, ncb, skip_blocks):
    b, _, d = xs[0].shape
    t = sum(a.shape[1] for a in xs)
    nblk = t // tm - skip_blocks
    gw = GROUP_WIDTH
    row = lambda col: (lambda bi, j: (bi, j + skip_blocks, col))
    attn_skip = (t - oa.shape[1]) // tm
    attn_row = lambda bi, j: (bi, j + skip_blocks - attn_skip, 0)
    const = lambda bi, j: (0, 0)
    resident = lambda shape: pl.BlockSpec(shape, const, pipeline_mode=pl.Buffered(1))
    return pl.pallas_call(
        _per_batch_row(functools.partial(_out_kernel, n_x=len(xs), ncb=ncb, skip_blocks=skip_blocks),
                       "b" * (len(xs) + 9) + "-" * 5 + "b"),
        out_shape=jax.ShapeDtypeStruct((b, nblk * tm, d), F32),
        grid=(b // NB, nblk),
        in_specs=_residual_specs(xs, tm, ncb, skip_blocks) + [
                  pl.BlockSpec((NB, 1, 1, N_MOD * d),
                               lambda bi, j: (bi, jnp.where(j + skip_blocks >= ncb, 1, 0), 0, 0)),
                  pl.BlockSpec((NB, tm, gw), attn_row),
                  pl.BlockSpec((NB, tm, gw), attn_row),
                  pl.BlockSpec((NB, tm, gw), row(0)),
                  pl.BlockSpec((NB, tm, gw), row(0)),
                  pl.BlockSpec((NB, tm, gw), row(2)),
                  pl.BlockSpec((NB, tm, gw), row(0)),
                  pl.BlockSpec((NB, tm, gw), row(0)),
                  pl.BlockSpec((NB, tm, gw), row(3)),
                  pl.BlockSpec((2, gw), const),
                  pl.BlockSpec((1, d), const),
                  resident((d, d)),
                  resident((d, 4 * d)),
                  resident((4 * d, d))],
        out_specs=pl.BlockSpec((NB, tm, d), lambda bi, j: (bi, j, 0)),
        compiler_params=_cparams(("arbitrary", "arbitrary")),
        name="out_mlp",
    )(*xs, modsel, oa, ob, gf, gb, zc, mf, mb, zd, gains, g2, wo, w1, w2)


def _projection_columns():
    gw, kw = GROUP_WIDTH, KV_HEADS * HEAD_DIM
    splits = (gw, kw, kw, gw, kw, kw, GLA_KW, GLA_KW, gw, gw, 2 * GLA_GATE_RANK,
              gw, gw, gw, gw, 2 * GROUP_HEADS, 2 * GROUP_HEADS)
    offs = np.concatenate([[0], np.cumsum(splits)])
    in_width = int(offs[-1])
    half_split = np.concatenate([np.arange(0, HEAD_DIM, 2), np.arange(1, HEAD_DIM, 2)])
    plain = np.arange(HEAD_DIM)
    cols = []
    mixers = (0, 3)
    for base in mixers:
        cols += [offs[base] + h * HEAD_DIM + half_split for h in ATTN_HEAD_ORDER]
    for base in mixers:
        cols += [offs[base + 1] + g * HEAD_DIM + half_split for g in range(KV_HEADS)]
    for base in mixers:
        cols += [offs[base + 2] + g * HEAD_DIM + plain for g in range(KV_HEADS)]
    cols.append(np.arange(offs[6], offs[11]))
    cols.append(np.full(W_C - (offs[11] - offs[6]), in_width))
    cols.append(np.arange(offs[11], offs[17]))
    cols.append(np.full(W_D - (offs[17] - offs[11]), in_width))
    cols = np.concatenate(cols).astype(np.int32)
    assert cols.shape[0] == W_ALL
    return cols, in_width, half_split


def _rope_tables(seq, lc):
    rows = seq // GRID_W
    row = jnp.repeat(jnp.arange(rows, dtype=F32), GRID_W)
    col = jnp.tile(jnp.arange(GRID_W, dtype=F32), rows)
    n_freq = HEAD_DIM // 4
    inv = ROPE_THETA ** (-jnp.arange(n_freq, dtype=F32) / n_freq)
    ang = jnp.concatenate([row[:, None] * inv, col[:, None] * inv], axis=-1)
    cos, sin = jnp.cos(ang), jnp.sin(ang)
    cos_t = jnp.tile(cos, (1, 4))
    sin_t = jnp.tile(jnp.concatenate([-sin, sin], axis=-1), (1, 2))
    cos_t = jnp.concatenate([jnp.ones((lc, LANES), F32), cos_t], axis=0)
    sin_t = jnp.concatenate([jnp.zeros((lc, LANES), F32), sin_t], axis=0)
    return cos_t, sin_t


def kernel(x, c, ctx, c_ctx, w_mod, b_mod, g_norm1, g_norm2, w_in, g_q_a, g_k_a, g_q_b, g_k_b, sink_b,
           w_gla_gate, b_gla_gate, g_gla_out, b_mlstm_i, b_mlstm_f, g_mlstm_out, w_out, w_mlp1, w_mlp2):
    b, seq, d = x.shape
    lc = ctx.shape[1]
    depth = w_mod.shape[0]
    tm = ROW_TILE
    assert b % NB == 0 and d == 4 * GROUP_WIDTH and lc % tm == 0 and seq % tm == 0 and seq % GRID_W == 0 and b + 1 <= MOD_ROWS

    cols, in_width, half_split = _projection_columns()
    cos_t, sin_t = _rope_tables(seq, lc)
    attn_rows = np.concatenate([h * HEAD_DIM + np.arange(HEAD_DIM) for h in ATTN_HEAD_ORDER])
    out_rows = np.concatenate([attn_rows, GROUP_WIDTH + attn_rows, np.arange(2 * GROUP_WIDTH, d)])

    cc = jnp.zeros((MOD_ROWS, d), F32).at[0:b].set(c).at[b].set(c_ctx)
    mod_all = _modulation(cc, w_mod, b_mod)

    xs = (ctx, x)
    for l in range(depth):
        modsel = jnp.stack([jnp.broadcast_to(mod_all[l, b], (b, N_MOD * d)), mod_all[l, 0:b]], axis=1)
        modsel = modsel.reshape(b, 2, 1, N_MOD * d)

        w_ext = jnp.concatenate([w_in[l], jnp.zeros((d, 1), F32)], axis=1)
        w_big = jnp.take(w_ext, cols, axis=1).astype(BF16)
        tiled = lambda g, n: jnp.tile(g[half_split], n)
        gains_qk = jnp.stack([tiled(g_q_a[l], GROUP_HEADS), tiled(g_q_b[l], GROUP_HEADS),
                              jnp.concatenate([tiled(g_k_a[l], KV_HEADS), tiled(g_k_b[l], KV_HEADS)])])
        wg = jnp.zeros((LANES, 2 * GLA_KW), F32)
        wg = wg.at[0:GLA_GATE_RANK, 0:GLA_KW].set(w_gla_gate[l, 0])
        wg = wg.at[GLA_GATE_RANK:2 * GLA_GATE_RANK, GLA_KW:].set(w_gla_gate[l, 1]).astype(BF16)
        bg = b_gla_gate[l].reshape(1, 2 * GLA_KW)
        bm = jnp.zeros((1, LANES), F32)
        bm = bm.at[0, 0:2 * GROUP_HEADS].set(b_mlstm_i[l].reshape(-1))
        bm = bm.at[0, 2 * GROUP_HEADS:4 * GROUP_HEADS].set(b_mlstm_f[l].reshape(-1))

        q, k, vt, zc, zd = _projection(xs, modsel, g_norm1[l].reshape(1, d), w_big, gains_qk,
                                      cos_t, sin_t, wg, bg, bm, tm, lc // tm)
        last = l == depth - 1
        oa, ob = _attention(q, k, vt, sink_b[l], lc, tq=ROW_TILE, tk=KEY_CHUNK, context_queries=not last)
        gf, gb, mf, mb = _scans(zc, zd, lc, tm)

        gains_out = jnp.stack([jnp.tile(g_gla_out[l], 4), jnp.tile(g_mlstm_out[l], 4)])
        xs = (_out_mlp(xs, modsel, oa, ob, gf, gb, zc, mf, mb, zd, gains_out, g_norm2[l].reshape(1, d),
                       w_out[l][out_rows].astype(BF16), w_mlp1[l].astype(BF16), w_mlp2[l].astype(BF16), tm,
                       ncb=lc // tm, skip_blocks=lc // tm if last else 0),)
    return xs[0]
```

```python
import functools

import numpy as np
import jax
import jax.numpy as jnp
from jax import lax
from jax.experimental import pallas as pl
from jax.experimental.pallas import tpu as pltpu

F32 = jnp.float32
BF16 = jnp.bfloat16

HEAD_DIM = 64
GROUP_HEADS = 4
GROUP_WIDTH = GROUP_HEADS * HEAD_DIM
KV_HEADS = 2
GRID_W = 64
Q_BLOCK = 128
ROPE_THETA = 10000.0
GLA_DK = 32
GLA_KW = GROUP_HEADS * GLA_DK
GLA_GATE_RANK = 16
GLA_GATE_TAU = 16.0
N_MOD = 6
EPS = 1e-6
LOG2E = 1.4426950408889634
NEG = -1e30
CHUNK = 64
LANES = 128
MOD_ROWS = 16
VMEM_LIMIT = 56 * 1024 * 1024

ATTN_HEAD_ORDER = (0, 2, 1, 3)
W_Q = 2 * GROUP_WIDTH
W_KV = 2 * KV_HEADS * HEAD_DIM
W_C = 896
W_D = 1152
W_ALL = W_Q + 2 * W_KV + W_C + W_D


NB = 2
ROW_TILE = 256
KEY_CHUNK = 256


def _per_batch_row(body, layout):
    def kernel(*refs):
        assert len(refs) == len(layout)
        for i in range(NB):
            body(*[r.at[i:i + 1] if how == "b" else r for r, how in zip(refs, layout)])
    return kernel


def _cparams(sem):
    return pltpu.CompilerParams(dimension_semantics=sem, vmem_limit_bytes=VMEM_LIMIT)


def _dot(a, b):
    return jnp.dot(a, b, preferred_element_type=F32)


def _dot_nt(a, b):
    return lax.dot_general(a, b, (((1,), (1,)), ((), ())), preferred_element_type=F32)


def _dot_tn(a, b):
    return lax.dot_general(a, b, (((0,), (0,)), ((), ())), preferred_element_type=F32)


def _split3(x):
    hi = x.astype(BF16)
    r1 = x - hi.astype(F32)
    mid = r1.astype(BF16)
    lo = (r1 - mid.astype(F32)).astype(BF16)
    return hi, mid, lo


def _dot01(a01, x):
    hi, mid, lo = _split3(x)
    return _dot(a01, hi) + _dot(a01, mid) + _dot(a01, lo)


def _dot01_r2(x, b01):
    hi = x.astype(BF16)
    lo = (x - hi.astype(F32)).astype(BF16)
    return _dot(hi, b01) + _dot(lo, b01)


def _log_sigmoid(x):
    return jnp.minimum(x, 0.0) - jnp.log(1.0 + jnp.exp(-jnp.abs(x)))


def _sigmoid(x):
    return 1.0 / (1.0 + jnp.exp(-x))


def _iota(shape, dim):
    return lax.broadcasted_iota(jnp.int32, shape, dim)


def _group_ones(n, group):
    return (_iota((n, n), 0) // group == _iota((n, n), 1) // group).astype(BF16)


def _mod_kernel(c_ref, w_ref, b_ref, o_ref):
    c = c_ref[...]
    s = (c * _sigmoid(c)).astype(BF16)
    o_ref[0] = _dot(s, w_ref[0].astype(BF16)) + b_ref[0]


def _modulation(cc, w_mod, b_mod):
    depth, d, n = w_mod.shape
    tn = 1536
    return pl.pallas_call(
        _mod_kernel,
        out_shape=jax.ShapeDtypeStruct((depth, MOD_ROWS, n), F32),
        grid=(depth, n // tn),
        in_specs=[pl.BlockSpec((MOD_ROWS, d), lambda l, j: (0, 0)),
                  pl.BlockSpec((1, d, tn), lambda l, j: (l, 0, j)),
                  pl.BlockSpec((1, 1, tn), lambda l, j: (l, 0, j))],
        out_specs=pl.BlockSpec((1, MOD_ROWS, tn), lambda l, j: (l, 0, j)),
        compiler_params=_cparams(("arbitrary", "arbitrary")),
        name="modulation",
    )(cc, w_mod, b_mod.reshape(depth, 1, n))


def _qk_norm_rope(z, gain, cos, sin, ones_bd, scale):
    ss = _dot01_r2(z * z, ones_bd)
    y = z * lax.rsqrt(ss * (1.0 / HEAD_DIM) + EPS) * gain
    first_half = (_iota((z.shape[0], LANES), 1) % HEAD_DIM) < (HEAD_DIM // 2)
    outs = []
    for cb in range(z.shape[1] // LANES):
        yc = y[:, cb * LANES:(cb + 1) * LANES]
        partner = jnp.where(first_half, pltpu.roll(yc, LANES - HEAD_DIM // 2, 1),
                            pltpu.roll(yc, HEAD_DIM // 2, 1))
        outs.append((yc * cos + partner * sin) * scale)
    return jnp.concatenate(outs, axis=1)


def _residual_specs(xs, tm, ncb, skip_blocks=0):
    d = xs[0].shape[2]
    if len(xs) == 1:
        return [pl.BlockSpec((NB, tm, d), lambda bi, j: (bi, j + skip_blocks, 0))]
    return [pl.BlockSpec((NB, tm, d), lambda bi, j: (bi, jnp.minimum(j + skip_blocks, ncb - 1), 0)),
            pl.BlockSpec((NB, tm, d), lambda bi, j: (bi, jnp.maximum(j + skip_blocks - ncb, 0), 0))]


def _residual_rows(x_refs, ncb, skip_blocks=0):
    if len(x_refs) == 1:
        return x_refs[0][0]
    return jnp.where(pl.program_id(1) + skip_blocks < ncb, x_refs[0][0], x_refs[1][0])


def _proj_kernel(*refs, n_x, ncb):
    x_refs = refs[:n_x]
    (mod_ref, g1_ref, w_ref, gains_ref, cos_ref, sin_ref, wg_ref, bg_ref, bm_ref,
     q_ref, k_ref, vt_ref, zc_ref, zd_ref) = refs[n_x:]
    d = x_refs[0].shape[2]
    tm = x_refs[0].shape[1]
    x = _residual_rows(x_refs, ncb)
    mod = mod_ref[0, 0]
    sh1 = mod[:, 0:d]
    sc1 = mod[:, d:2 * d]
    ms = jnp.mean(x * x, axis=-1, keepdims=True)
    h = (x * lax.rsqrt(ms + EPS)) * g1_ref[...]
    hb = (h * (1.0 + sc1) + sh1).astype(BF16)

    ones_bd = _group_ones(GROUP_WIDTH, HEAD_DIM)
    cos = cos_ref[...]
    sin = sin_ref[...]
    gains = gains_ref[...]
    gw = GROUP_WIDTH

    z = _dot(hb, w_ref[:, 0:W_Q + 2 * W_KV])
    q_scale = HEAD_DIM ** -0.5 * LOG2E
    for n in range(2):
        q = _qk_norm_rope(z[:, n * gw:(n + 1) * gw], gains[n:n + 1], cos, sin, ones_bd, q_scale)
        q_ref[0, :, n * gw:(n + 1) * gw] = q.astype(BF16)
    k_ref[0] = _qk_norm_rope(z[:, W_Q:W_Q + W_KV], gains[2:3], cos, sin, ones_bd, 1.0).astype(BF16)
    vt_ref[0] = z[:, W_Q + W_KV:W_Q + 2 * W_KV].T.astype(BF16)

    base = W_Q + 2 * W_KV
    z = _dot(hb, w_ref[:, base:base + W_C])
    zc_ref[0, :, 0:GLA_KW] = z[:, 0:GLA_KW] * (GLA_DK ** -0.5)
    zc_ref[0, :, GLA_KW:768] = z[:, GLA_KW:768]
    pre = _dot(z[:, 768:896].astype(BF16), wg_ref[...]) + bg_ref[...]
    zc_ref[0, :, 768:1024] = _log_sigmoid(pre) * (1.0 / GLA_GATE_TAU)

    z = _dot(hb, w_ref[:, base + W_C:W_ALL])
    zd_ref[0, :, 0:gw] = z[:, 0:gw]
    zd_ref[0, :, gw:2 * gw] = z[:, gw:2 * gw] * (HEAD_DIM ** -0.5)
    zd_ref[0, :, 2 * gw:4 * gw] = z[:, 2 * gw:4 * gw]
    gates = z[:, 4 * gw:4 * gw + LANES] + bm_ref[...]
    is_input_gate = _iota((tm, LANES), 1) < 2 * GROUP_HEADS
    zd_ref[0, :, 4 * gw:4 * gw + LANES] = jnp.where(is_input_gate, gates, _log_sigmoid(gates))


def _projection(xs, modsel, g1, w_big, gains, cos_t, sin_t, wg, bg, bm, tm, ncb):
    b, _, d = xs[0].shape
    t = sum(a.shape[1] for a in xs)
    nblk = t // tm
    const = lambda bi, j: (0, 0)
    return pl.pallas_call(
        _per_batch_row(functools.partial(_proj_kernel, n_x=len(xs), ncb=ncb), "b" * (len(xs) + 1) + "-" * 8 + "b" * 5),
        out_shape=(jax.ShapeDtypeStruct((b, t, W_Q), BF16),
                   jax.ShapeDtypeStruct((b, t, W_KV), BF16),
                   jax.ShapeDtypeStruct((b, W_KV, t), BF16),
                   jax.ShapeDtypeStruct((b, t, 1024), F32),
                   jax.ShapeDtypeStruct((b, t, W_D), F32)),
        grid=(b // NB, nblk),
        in_specs=_residual_specs(xs, tm, ncb) + [
                  pl.BlockSpec((NB, 1, 1, N_MOD * d), lambda bi, j: (bi, jnp.where(j >= ncb, 1, 0), 0, 0)),
                  pl.BlockSpec((1, d), const),
                  pl.BlockSpec((d, W_ALL), const),
                  pl.BlockSpec((3, GROUP_WIDTH), const),
                  pl.BlockSpec((tm, LANES), lambda bi, j: (j, 0)),
                  pl.BlockSpec((tm, LANES), lambda bi, j: (j, 0)),
                  pl.BlockSpec((LANES, 2 * GLA_KW), const),
                  pl.BlockSpec((1, 2 * GLA_KW), const),
                  pl.BlockSpec((1, LANES), const)],
        out_specs=(pl.BlockSpec((NB, tm, W_Q), lambda bi, j: (bi, j, 0)),
                   pl.BlockSpec((NB, tm, W_KV), lambda bi, j: (bi, j, 0)),
                   pl.BlockSpec((NB, W_KV, tm), lambda bi, j: (bi, 0, j)),
                   pl.BlockSpec((NB, tm, 1024), lambda bi, j: (bi, j, 0)),
                   pl.BlockSpec((NB, tm, W_D), lambda bi, j: (bi, j, 0))),
        compiler_params=_cparams(("arbitrary", "arbitrary")),
        name="projection",
    )(*xs, modsel, g1, w_big, gains, cos_t, sin_t, wg, bg, bm)


ONES_ROWS = 16


def _masked_query_tiles(q_ref):
    low = _iota((q_ref.shape[1], LANES), 1) < HEAD_DIM
    tiles = []
    for n in range(GROUP_WIDTH // LANES):
        q128 = q_ref[0, :, n * LANES:(n + 1) * LANES]
        zero = jnp.zeros_like(q128)
        tiles += [jnp.where(low, q128, zero), jnp.where(low, zero, q128)]
    return tiles


def _heads_to_rows(o_t):
    halves = [jnp.concatenate(o_t[2 * n:2 * n + 2], axis=0).T for n in range(len(o_t) // 2)]
    return jnp.concatenate(halves, axis=1)


def _window_stages(sink_ref, q_ref, k_refs, v_refs, o_ref, j, nctx, nblk):
    tq = q_ref.shape[1]
    halo = Q_BLOCK
    n_win = tq + 2 * halo
    n_keys = n_win + k_refs[3].shape[1]
    is_lat = j >= nctx
    off = 4 * tq
    lo_prev = jnp.where(j > nctx, -halo, off)
    lo_cur = jnp.where(is_lat, -halo, off)
    lo_next = jnp.where(jnp.logical_and(is_lat, j + 1 <= nblk - 1), -halo, off)
    row = _iota((n_keys, tq), 0)
    diff = row - halo - _iota((n_keys, tq), 1)
    lo = jnp.where(row < halo, lo_prev, jnp.where(row < halo + tq, lo_cur, lo_next))
    ok = jnp.logical_or(row >= n_win, jnp.logical_and(diff >= lo, diff <= halo))
    keys = jnp.concatenate([r[0] for r in k_refs], axis=0)
    v_t = jnp.concatenate([r[0] for r in v_refs], axis=1)
    ones = jnp.ones((ONES_ROWS, n_keys), BF16)
    tiles = _masked_query_tiles(q_ref)
    yield
    scores = [jnp.where(ok, _dot_nt(keys, qp), NEG) for qp in tiles]
    yield
    o_t = []
    for c, s_t in enumerate(scores):
        sink = sink_ref[ATTN_HEAD_ORDER[c]] * LOG2E
        m = jnp.maximum(jnp.max(s_t, axis=0, keepdims=True), sink)
        kv = c % KV_HEADS
        vx = jnp.concatenate([v_t[kv * HEAD_DIM:(kv + 1) * HEAD_DIM], ones], axis=0)
        acc = _dot(vx, jnp.exp2(s_t - m).astype(BF16))
        o_t.append(acc[0:HEAD_DIM] / (acc[HEAD_DIM:HEAD_DIM + 1] + jnp.exp2(sink - m)))
        yield
    o_ref[0] = _heads_to_rows(o_t).astype(BF16)


def _attention_kernel(sink_ref, qa_ref, qb_ref, ka_ref, vta_ref, kp_ref, kc_ref, kn_ref, kx_ref,
                      vp_ref, vc_ref, vn_ref, vx_ref, oa_ref, ob_ref, *, lc, tk, nk, nctx, nblk, first):
    n_tiles = GROUP_HEADS
    j = pl.program_id(1) + first
    tiles = _masked_query_tiles(qa_ref)
    ones = jnp.ones((ONES_ROWS, tk), BF16)

    def window():
        return _window_stages(sink_ref, qb_ref, (kp_ref, kc_ref, kn_ref, kx_ref),
                              (vp_ref, vc_ref, vn_ref, vx_ref), ob_ref, j, nctx, nblk)

    def scores(rows):
        k = ka_ref[0, rows, :]
        return [_dot_nt(k, qp) for qp in tiles]

    def chunk(s_t, rows, m, accs):
        m_out, acc_out = [], []
        for c in range(n_tiles):
            col_max = jnp.max(s_t[c], axis=0, keepdims=True)
            m_new = col_max if m is None else jnp.maximum(m[c], col_max)
            p_t = jnp.exp2(s_t[c] - m_new).astype(BF16)
            kv = c % KV_HEADS
            vx = jnp.concatenate([vta_ref[0, kv * HEAD_DIM:(kv + 1) * HEAD_DIM, rows], ones], axis=0)
            upd = _dot(vx, p_t)
            m_out.append(m_new)
            acc_out.append(upd if m is None else jnp.exp2(m[c] - m_new) * accs[c] + upd)
        return m_out, acc_out

    def key_chunks(first_chunk, count, m, accs, side=None):
        rows = [slice((first_chunk + c) * tk, (first_chunk + c + 1) * tk) for c in range(count)]
        s_next = scores(rows[0])
        for c in range(count):
            s_t = s_next
            if c + 1 < count:
                s_next = scores(rows[c + 1])
            if side is not None:
                next(side, None)
            m, accs = chunk(s_t, rows[c], m, accs)
        return m, accs

    n_ctx = lc // tk
    m, accs = key_chunks(0, n_ctx, None, None)

    def finish(accs):
        return _heads_to_rows([a[0:HEAD_DIM] / a[HEAD_DIM:HEAD_DIM + 1] for a in accs])

    def latent_queries():
        side = window()
        out = finish(key_chunks(n_ctx, nk, m, accs, side)[1])
        for _ in side:
            pass
        return out

    def context_queries():
        for _ in window():
            pass
        return finish(accs)

    o = lax.cond(j >= nctx, latent_queries, context_queries) if first < nctx else latent_queries()
    oa_ref[0] = o.astype(BF16)


def _attention(q, k, vt, sink, lc, tq, tk, context_queries):
    b, t, _ = q.shape
    nblk = t // tq
    nctx = lc // tq
    nk = (t - lc) // tk
    assert lc % tk == 0 and (t - lc) % tk == 0
    first = 0 if context_queries else nctx
    per = tq // Q_BLOCK
    prev_blk = lambda j: jnp.maximum(j * per - 1, 0)
    next_blk = lambda j: jnp.minimum((j + 1) * per, nblk * per - 1)
    spec = lambda shape, index: pl.BlockSpec(shape, lambda bi, s: index(bi, s + first))
    k_specs = [spec((NB, Q_BLOCK, LANES), lambda bi, j: (bi, prev_blk(j), 1)),
               spec((NB, tq, LANES), lambda bi, j: (bi, j, 1)),
               spec((NB, Q_BLOCK, LANES), lambda bi, j: (bi, next_blk(j), 1)),
               spec((NB, lc, LANES), lambda bi, j: (bi, 0, 1))]
    v_specs = [spec((NB, LANES, Q_BLOCK), lambda bi, j: (bi, 1, prev_blk(j))),
               spec((NB, LANES, tq), lambda bi, j: (bi, 1, j)),
               spec((NB, LANES, Q_BLOCK), lambda bi, j: (bi, 1, next_blk(j))),
               spec((NB, LANES, lc), lambda bi, j: (bi, 1, 0))]
    out = jax.ShapeDtypeStruct((b, t - first * tq, GROUP_WIDTH), BF16)
    out_spec = pl.BlockSpec((NB, tq, GROUP_WIDTH), lambda bi, s: (bi, s, 0))
    q_spec = lambda col: spec((NB, tq, GROUP_WIDTH), lambda bi, j: (bi, j, col))
    return pl.pallas_call(
        _per_batch_row(functools.partial(_attention_kernel, lc=lc, tk=tk, nk=nk, nctx=nctx, nblk=nblk, first=first),
                       "-" + "b" * 14),
        out_shape=(out, out),
        grid=(b // NB, nblk - first),
        in_specs=[pl.BlockSpec(memory_space=pltpu.SMEM), q_spec(0), q_spec(1),
                  spec((NB, t, LANES), lambda bi, j: (bi, 0, 0)),
                  spec((NB, LANES, t), lambda bi, j: (bi, 0, 0))] + k_specs + v_specs,
        out_specs=(out_spec, out_spec),
        compiler_params=_cparams(("arbitrary", "arbitrary")),
        name="attention",
    )(sink, q, q, k, vt, k, k, k, k, vt, vt, vt, vt)


def _chunk_tri(rows, reverse):
    t = _iota((rows, rows), 0)
    u = _iota((rows, rows), 1)
    same = t // CHUNK == u // CHUNK
    return jnp.logical_and(same, (u >= t) if reverse else (u <= t)).astype(BF16)


def _tile4(x):
    return jnp.concatenate([x, x, x, x], axis=0)


def _rev_block(i, ncb, nblk):
    return jnp.where(i < ncb, ncb - 1 - i, nblk - 1 - (i - ncb))


class _Item:
    pass


def _scan_items(directions, nchunk):
    items = []
    for pos in range(nchunk):
        for d in directions:
            it = _Item()
            it.d = d
            it.c = nchunk - 1 - pos if d.reverse else pos
            it.sl = slice(it.c * CHUNK, (it.c + 1) * CHUNK)
            items.append(it)
    return items


def _gla_stages(qkv_f_ref, la_f_ref, qkv_b_ref, la_b_ref, of_ref, ob_ref, sf_ref, sb_ref):
    @pl.when(pl.program_id(1) == 0)
    def _():
        sf_ref[...] = jnp.zeros_like(sf_ref)
        sb_ref[...] = jnp.zeros_like(sb_ref)

    rows = qkv_f_ref.shape[1]
    same_kd = _iota((GROUP_WIDTH, GLA_KW), 0) // CHUNK == _iota((GROUP_WIDTH, GLA_KW), 1) // GLA_DK
    same_kv = _iota((GROUP_WIDTH, GROUP_WIDTH), 0) // CHUNK == _iota((GROUP_WIDTH, GROUP_WIDTH), 1) // HEAD_DIM
    t_pos = _iota((CHUNK, GROUP_WIDTH), 0)
    s_pos = _iota((CHUNK, GROUP_WIDTH), 1) % CHUNK

    directions = []
    for qkv_ref, la_ref, o_ref, s_ref, reverse in ((qkv_f_ref, la_f_ref, of_ref, sf_ref, False),
                                                   (qkv_b_ref, la_b_ref, ob_ref, sb_ref, True)):
        d = _Item()
        d.reverse, d.o_ref, d.s_ref = reverse, o_ref, s_ref
        d.b = _dot01(_chunk_tri(rows, reverse), la_ref[0])
        d.q = qkv_ref[0, :, 0:GLA_KW]
        d.k = qkv_ref[0, :, GLA_KW:2 * GLA_KW]
        d.v = qkv_ref[0, :, 2 * GLA_KW:2 * GLA_KW + GROUP_WIDTH].astype(BF16)
        d.causal = (s_pos >= t_pos) if reverse else (s_pos <= t_pos)
        d.end = 0 if reverse else CHUNK - 1
        d.state = s_ref[...]
        directions.append(d)
    items = _scan_items(directions, rows // CHUNK)

    yield
    for it in items:
        d = it.d
        bc = d.b[it.sl]
        b_end = bc[d.end:d.end + 1]
        b_mid = bc[CHUNK // 2:CHUNK // 2 + 1]
        qc, kc = d.q[it.sl], d.k[it.sl]
        it.vc = d.v[it.sl]
        it.q_inter = (qc * jnp.exp(bc)).astype(BF16)
        it.q_intra = (qc * jnp.exp(bc - b_mid)).astype(BF16)
        k_intra = (kc * jnp.exp(b_mid - bc)).astype(BF16)
        it.k_state = (kc * jnp.exp(b_end - bc)).astype(BF16)
        it.decay = jnp.exp(b_end)
        it.k_bd = jnp.where(same_kd, _tile4(k_intra), jnp.zeros((), BF16))
        it.v_bd = jnp.where(same_kv, _tile4(it.vc), jnp.zeros((), BF16))
    yield
    for it in items:
        it.scores = jnp.where(it.d.causal, _dot_nt(it.q_intra, it.k_bd), 0.0).astype(BF16)
    yield
    for it in items:
        it.o_intra = _dot(it.scores, it.v_bd)
        it.update = jnp.where(same_kd, _dot_tn(it.vc, it.k_state), 0.0)
    yield
    for it in items:
        it.state_in = it.d.state
        it.d.state = it.d.state * it.decay + it.update
    yield
    for it in items:
        it.d.o_ref[0, it.sl, :] = it.o_intra + _dot_nt(it.q_inter, it.state_in.astype(BF16))
    yield
    for d in directions:
        d.s_ref[...] = d.state


def _expand_heads(g, base, rows):
    hid = _iota((rows, GROUP_WIDTH), 1) // HEAD_DIM
    cols = [jnp.broadcast_to(g[:, base + h:base + h + 1], (rows, GROUP_WIDTH)) for h in range(GROUP_HEADS)]
    return jnp.where(hid == 0, cols[0], jnp.where(hid == 1, cols[1], jnp.where(hid == 2, cols[2], cols[3])))


def _mlstm_stages(qkv_f_ref, gate_f_ref, qkv_b_ref, gate_b_ref, of_ref, ob_ref,
                  cf_ref, nf_ref, mf_ref, cb_ref, nb_ref, mb_ref):
    @pl.when(pl.program_id(1) == 0)
    def _():
        for r in (cf_ref, nf_ref, mf_ref, cb_ref, nb_ref, mb_ref):
            r[...] = jnp.zeros_like(r)

    rows = qkv_f_ref.shape[1]
    gw = GROUP_WIDTH
    same_head = _iota((gw, gw), 0) // HEAD_DIM == _iota((gw, gw), 1) // HEAD_DIM
    ones_bd = same_head.astype(BF16)
    hid = _iota((CHUNK, gw), 1) // HEAD_DIM
    t_pos = _iota((CHUNK, gw), 0)
    s_pos = _iota((CHUNK, gw), 1) % CHUNK
    diag = (s_pos == t_pos).astype(F32)
    ones_cc = jnp.ones((CHUNK, CHUNK), BF16)

    directions = []
    for index, (qkv_ref, gate_ref, o_ref, c_ref, n_ref, m_ref) in enumerate(
            ((qkv_f_ref, gate_f_ref, of_ref, cf_ref, nf_ref, mf_ref),
             (qkv_b_ref, gate_b_ref, ob_ref, cb_ref, nb_ref, mb_ref))):
        d = _Item()
        d.reverse = index == 1
        d.o_ref, d.c_ref, d.n_ref, d.m_ref = o_ref, c_ref, n_ref, m_ref
        g = gate_ref[0]
        i_col = GROUP_HEADS * index
        f_col = 2 * GROUP_HEADS + GROUP_HEADS * index
        f_all = _dot01(_chunk_tri(rows, d.reverse), g)
        d.ig = _expand_heads(g, i_col, rows)
        d.f_cum = _expand_heads(f_all, f_col, rows)
        d.q = qkv_ref[0, :, 0:gw]
        d.k = qkv_ref[0, :, gw:2 * gw]
        d.v = qkv_ref[0, :, 2 * gw:3 * gw].astype(BF16)
        d.causal = (s_pos >= t_pos) if d.reverse else (s_pos <= t_pos)
        d.end = 0 if d.reverse else CHUNK - 1
        d.c_state = c_ref[...]
        d.n_state = n_ref[...]
        d.m_state = m_ref[...]
        directions.append(d)
    items = _scan_items(directions, rows // CHUNK)

    yield
    for it in items:
        d = it.d
        it.fc, it.ic = d.f_cum[it.sl], d.ig[it.sl]
        it.qc, it.kc, it.vc = d.q[it.sl], d.k[it.sl], d.v[it.sl]
        it.qb = it.qc.astype(BF16)
        it.key_term = _dot01(ones_cc, (it.fc - it.ic) * diag)
    yield
    for it in items:
        it.logw = jnp.where(it.d.causal, it.fc - it.key_term, NEG)
        head_max = [jnp.max(it.logw[:, h * HEAD_DIM:(h + 1) * HEAD_DIM], axis=-1, keepdims=True)
                    for h in range(GROUP_HEADS)]
        it.row_max = jnp.where(hid < 2, jnp.where(hid == 0, head_max[0], head_max[1]),
                               jnp.where(hid == 2, head_max[2], head_max[3]))
        it.k_bd = jnp.where(same_head, _tile4(it.kc.astype(BF16)), jnp.zeros((), BF16))
        it.v_bd = jnp.where(same_head, _tile4(it.vc), jnp.zeros((), BF16))
    yield
    for it in items:
        d = it.d
        it.log_inter = it.fc + d.m_state
        it.m_t = jnp.maximum(it.log_inter, it.row_max)
        d.m_state = it.m_new = it.m_t[d.end:d.end + 1]
    yield
    for it in items:
        end = it.d.end
        it.w = jnp.exp(it.logw - it.m_t)
        it.w_inter = jnp.exp(it.log_inter - it.m_t)
        it.decay = it.w_inter[end:end + 1]
        it.k_end = it.kc * jnp.exp(it.fc[end:end + 1] - it.fc + it.ic - it.m_new)
    yield
    for it in items:
        it.qk = _dot_nt(it.qb, it.k_bd) * it.w
    yield
    for it in items:
        it.num = _dot(it.qk.astype(BF16), it.v_bd)
        it.den = _dot01_r2(it.qk, ones_bd)
        it.update = jnp.where(same_head, _dot_tn(it.k_end.astype(BF16), it.vc), 0.0)
    yield
    for it in items:
        d = it.d
        it.c_in, it.n_in = d.c_state, d.n_state
        d.c_state = d.c_state * it.decay + it.update
        d.n_state = d.n_state * it.decay + jnp.sum(it.k_end, axis=0, keepdims=True)
    yield
    for it in items:
        num = it.num + it.w_inter * _dot(it.qb, it.c_in.astype(BF16))
        den = it.den + it.w_inter * _dot((it.qc * it.n_in).astype(BF16), ones_bd)
        it.d.o_ref[0, it.sl, :] = num / jnp.maximum(jnp.abs(den), jnp.exp(-it.m_t))
    yield
    for d in directions:
        d.c_ref[...] = d.c_state
        d.n_ref[...] = d.n_state
        d.m_ref[...] = d.m_state


def _scans_kernel(*refs):
    running = []
    for i in range(NB):
        blocks = [r.at[i:i + 1] for r in refs[0:12]]
        state = [r.at[i] for r in refs[12:20]]
        running.append(_gla_stages(*blocks[0:4], *blocks[8:10], *state[0:2]))
        running.append(_mlstm_stages(*blocks[4:8], *blocks[10:12], *state[2:8]))
    while running:
        running = [g for g in running if next(g, StopIteration) is not StopIteration]


def _scans(zc, zd, lc, rows):
    b, t, _ = zc.shape
    nblk = t // rows
    ncb = lc // rows
    gw = GROUP_WIDTH
    fwd = lambda col: (lambda bi, i: (bi, i, col))
    rev = lambda col: (lambda bi, i: (bi, _rev_block(i, ncb, nblk), col))
    gla_qkv_w = 2 * GLA_KW + gw
    gate_col = 4 * gw // LANES
    mlstm_state = [pltpu.VMEM((NB, gw, gw), F32), pltpu.VMEM((NB, 1, gw), F32), pltpu.VMEM((NB, 1, gw), F32)]
    gla_state = [pltpu.VMEM((NB, gw, GLA_KW), F32)]
    out = jax.ShapeDtypeStruct((b, t, gw), F32)
    return pl.pallas_call(
        _scans_kernel,
        out_shape=(out, out, out, out),
        grid=(b // NB, nblk),
        in_specs=[pl.BlockSpec((NB, rows, gla_qkv_w), fwd(0)),
                  pl.BlockSpec((NB, rows, GLA_KW), fwd(6)),
                  pl.BlockSpec((NB, rows, gla_qkv_w), rev(0)),
                  pl.BlockSpec((NB, rows, GLA_KW), rev(7)),
                  pl.BlockSpec((NB, rows, 3 * gw), fwd(0)),
                  pl.BlockSpec((NB, rows, LANES), fwd(gate_col)),
                  pl.BlockSpec((NB, rows, 3 * gw), rev(0)),
                  pl.BlockSpec((NB, rows, LANES), rev(gate_col))],
        out_specs=(pl.BlockSpec((NB, rows, gw), fwd(0)), pl.BlockSpec((NB, rows, gw), rev(0)),
                   pl.BlockSpec((NB, rows, gw), fwd(0)), pl.BlockSpec((NB, rows, gw), rev(0))),
        scratch_shapes=gla_state + gla_state + mlstm_state + mlstm_state,
        compiler_params=_cparams(("arbitrary", "arbitrary")),
        name="scans",
    )(zc, zc, zc, zc, zd, zd, zd, zd)


def _head_rms(o, gain, ones_bd):
    ss = _dot01_r2(o * o, ones_bd)
    return o * lax.rsqrt(ss * (1.0 / HEAD_DIM) + EPS) * gain


def _out_kernel(*refs, n_x, ncb, skip_blocks):
    x_refs = refs[:n_x]
    (mod_ref, oa_ref, ob_ref, gf_ref, gb_ref, r_ref, mf_ref, mb_ref, og_ref,
     gains_ref, g2_ref, wo_ref, w1_ref, w2_ref, y_ref) = refs[n_x:]
    d = x_refs[0].shape[2]
    x = _residual_rows(x_refs, ncb, skip_blocks)
    mod = mod_ref[0, 0]
    ga1 = mod[:, 2 * d:3 * d]
    sh2 = mod[:, 3 * d:4 * d]
    sc2 = mod[:, 4 * d:5 * d]
    ga2 = mod[:, 5 * d:6 * d]
    ones_bd = _group_ones(GROUP_WIDTH, HEAD_DIM)
    gains = gains_ref[...]

    r = r_ref[0]
    o_c = _head_rms(gf_ref[0] + gb_ref[0], gains[0:1], ones_bd) * (r * _sigmoid(r))
    o_d = _sigmoid(og_ref[0]) * _head_rms(mf_ref[0] + mb_ref[0], gains[1:2], ones_bd)
    o_cat = jnp.concatenate([oa_ref[0], ob_ref[0], o_c.astype(BF16), o_d.astype(BF16)], axis=1)
    x = x + ga1 * _dot(o_cat, wo_ref[...])

    ms = jnp.mean(x * x, axis=-1, keepdims=True)
    h = (x * lax.rsqrt(ms + EPS)) * g2_ref[...]
    hb = (h * (1.0 + sc2) + sh2).astype(BF16)
    hid = jnp.maximum(_dot(hb, w1_ref[...]), 0.0)
    hid = (hid * hid).astype(BF16)
    y_ref[0] = x + ga2 * _dot(hid, w2_ref[...])


def _out_mlp(xs, modsel, oa, ob, gf, gb, zc, mf, mb, zd, gains, g2, wo, w1, w2, tm, ncb, skip_blocks):
    b, _, d = xs[0].shape
    t = sum(a.shape[1] for a in xs)
    nblk = t // tm - skip_blocks
    gw = GROUP_WIDTH
    row = lambda col: (lambda bi, j: (bi, j + skip_blocks, col))
    attn_skip = (t - oa.shape[1]) // tm
    attn_row = lambda bi, j: (bi, j + skip_blocks - attn_skip, 0)
    const = lambda bi, j: (0, 0)
    resident = lambda shape: pl.BlockSpec(shape, const, pipeline_mode=pl.Buffered(1))
    return pl.pallas_call(
        _per_batch_row(functools.partial(_out_kernel, n_x=len(xs), ncb=ncb, skip_blocks=skip_blocks),
                       "b" * (len(xs) + 9) + "-" * 5 + "b"),
        out_shape=jax.ShapeDtypeStruct((b, nblk * tm, d), F32),
        grid=(b // NB, nblk),
        in_specs=_residual_specs(xs, tm, ncb, skip_blocks) + [
                  pl.BlockSpec((NB, 1, 1, N_MOD * d),
                               lambda bi, j: (bi, jnp.where(j + skip_blocks >= ncb, 1, 0), 0, 0)),
                  pl.BlockSpec((NB, tm, gw), attn_row),
                  pl.BlockSpec((NB, tm, gw), attn_row),
                  pl.BlockSpec((NB, tm, gw), row(0)),
                  pl.BlockSpec((NB, tm, gw), row(0)),
                  pl.BlockSpec((NB, tm, gw), row(2)),
                  pl.BlockSpec((NB, tm, gw), row(0)),
                  pl.BlockSpec((NB, tm, gw), row(0)),
                  pl.BlockSpec((NB, tm, gw), row(3)),
                  pl.BlockSpec((2, gw), const),
                  pl.BlockSpec((1, d), const),
                  resident((d, d)),
                  resident((d, 4 * d)),
                  resident((4 * d, d))],
        out_specs=pl.BlockSpec((NB, tm, d), lambda bi, j: (bi, j, 0)),
        compiler_params=_cparams(("arbitrary", "arbitrary")),
        name="out_mlp",
    )(*xs, modsel, oa, ob, gf, gb, zc, mf, mb, zd, gains, g2, wo, w1, w2)


def _projection_columns():
    gw, kw = GROUP_WIDTH, KV_HEADS * HEAD_DIM
    splits = (gw, kw, kw, gw, kw, kw, GLA_KW, GLA_KW, gw, gw, 2 * GLA_GATE_RANK,
              gw, gw, gw, gw, 2 * GROUP_HEADS, 2 * GROUP_HEADS)
    offs = np.concatenate([[0], np.cumsum(splits)])
    in_width = int(offs[-1])
    half_split = np.concatenate([np.arange(0, HEAD_DIM, 2), np.arange(1, HEAD_DIM, 2)])
    plain = np.arange(HEAD_DIM)
    cols = []
    mixers = (0, 3)
    for base in mixers:
        cols += [offs[base] + h * HEAD_DIM + half_split for h in ATTN_HEAD_ORDER]
    for base in mixers:
        cols += [offs[base + 1] + g * HEAD_DIM + half_split for g in range(KV_HEADS)]
    for base in mixers:
        cols += [offs[base + 2] + g * HEAD_DIM + plain for g in range(KV_HEADS)]
    cols.append(np.arange(offs[6], offs[11]))
    cols.append(np.full(W_C - (offs[11] - offs[6]), in_width))
    cols.append(np.arange(offs[11], offs[17]))
    cols.append(np.full(W_D - (offs[17] - offs[11]), in_width))
    cols = np.concatenate(cols).astype(np.int32)
    assert cols.shape[0] == W_ALL
    return cols, in_width, half_split


def _rope_tables(seq, lc):
    rows = seq // GRID_W
    row = jnp.repeat(jnp.arange(rows, dtype=F32), GRID_W)
    col = jnp.tile(jnp.arange(GRID_W, dtype=F32), rows)
    n_freq = HEAD_DIM // 4
    inv = ROPE_THETA ** (-jnp.arange(n_freq, dtype=F32) / n_freq)
    ang = jnp.concatenate([row[:, None] * inv, col[:, None] * inv], axis=-1)
    cos, sin = jnp.cos(ang), jnp.sin(ang)
    cos_t = jnp.tile(cos, (1, 4))
    sin_t = jnp.tile(jnp.concatenate([-sin, sin], axis=-1), (1, 2))
    cos_t = jnp.concatenate([jnp.ones((lc, LANES), F32), cos_t], axis=0)
    sin_t = jnp.concatenate([jnp.zeros((lc, LANES), F32), sin_t], axis=0)
    return cos_t, sin_t


def kernel(x, c, ctx, c_ctx, w_mod, b_mod, g_norm1, g_norm2, w_in, g_q_a, g_k_a, g_q_b, g_k_b, sink_b,
           w_gla_gate, b_gla_gate, g_gla_out, b_mlstm_i, b_mlstm_f, g_mlstm_out, w_out, w_mlp1, w_mlp2):
    b, seq, d = x.shape
    lc = ctx.shape[1]
    depth = w_mod.shape[0]
    tm = ROW_TILE
    assert b % NB == 0 and d == 4 * GROUP_WIDTH and lc % tm == 0 and seq % tm == 0 and seq % GRID_W == 0 and b + 1 <= MOD_ROWS

    cols, in_width, half_split = _projection_columns()
    cos_t, sin_t = _rope_tables(seq, lc)
    attn_rows = np.concatenate([h * HEAD_DIM + np.arange(HEAD_DIM) for h in ATTN_HEAD_ORDER])
    out_rows = np.concatenate([attn_rows, GROUP_WIDTH + attn_rows, np.arange(2 * GROUP_WIDTH, d)])

    cc = jnp.zeros((MOD_ROWS, d), F32).at[0:b].set(c).at[b].set(c_ctx)
    mod_all = _modulation(cc, w_mod, b_mod)

    xs = (ctx, x)
    for l in range(depth):
        modsel = jnp.stack([jnp.broadcast_to(mod_all[l, b], (b, N_MOD * d)), mod_all[l, 0:b]], axis=1)
        modsel = modsel.reshape(b, 2, 1, N_MOD * d)

        w_ext = jnp.concatenate([w_in[l], jnp.zeros((d, 1), F32)], axis=1)
        w_big = jnp.take(w_ext, cols, axis=1).astype(BF16)
        tiled = lambda g, n: jnp.tile(g[half_split], n)
        gains_qk = jnp.stack([tiled(g_q_a[l], GROUP_HEADS), tiled(g_q_b[l], GROUP_HEADS),
                              jnp.concatenate([tiled(g_k_a[l], KV_HEADS), tiled(g_k_b[l], KV_HEADS)])])
        wg = jnp.zeros((LANES, 2 * GLA_KW), F32)
        wg = wg.at[0:GLA_GATE_RANK, 0:GLA_KW].set(w_gla_gate[l, 0])
        wg = wg.at[GLA_GATE_RANK:2 * GLA_GATE_RANK, GLA_KW:].set(w_gla_gate[l, 1]).astype(BF16)
        bg = b_gla_gate[l].reshape(1, 2 * GLA_KW)
        bm = jnp.zeros((1, LANES), F32)
        bm = bm.at[0, 0:2 * GROUP_HEADS].set(b_mlstm_i[l].reshape(-1))
        bm = bm.at[0, 2 * GROUP_HEADS:4 * GROUP_HEADS].set(b_mlstm_f[l].reshape(-1))

        q, k, vt, zc, zd = _projection(xs, modsel, g_norm1[l].reshape(1, d), w_big, gains_qk,
                                      cos_t, sin_t, wg, bg, bm, tm, lc // tm)
        last = l == depth - 1
        oa, ob = _attention(q, k, vt, sink_b[l], lc, tq=ROW_TILE, tk=KEY_CHUNK, context_queries=not last)
        gf, gb, mf, mb = _scans(zc, zd, lc, tm)

        gains_out = jnp.stack([jnp.tile(g_gla_out[l], 4), jnp.tile(g_mlstm_out[l], 4)])
        xs = (_out_mlp(xs, modsel, oa, ob, gf, gb, zc, mf, mb, zd, gains_out, g_norm2[l].reshape(1, d),
                       w_out[l][out_rows].astype(BF16), w_mlp1[l].astype(BF16), w_mlp2[l].astype(BF16), tm,
                       ncb=lc // tm, skip_blocks=lc // tm if last else 0),)
    return xs[0]
```

```python
import functools

import numpy as np
import jax
import jax.numpy as jnp
from jax import lax
from jax.experimental import pallas as pl
from jax.experimental.pallas import tpu as pltpu

F32 = jnp.float32
BF16 = jnp.bfloat16

HEAD_DIM = 64
GROUP_HEADS = 4
GROUP_WIDTH = GROUP_HEADS * HEAD_DIM
KV_HEADS = 2
GRID_W = 64
Q_BLOCK = 128
ROPE_THETA = 10000.0
GLA_DK = 32
GLA_KW = GROUP_HEADS * GLA_DK
GLA_GATE_RANK = 16
GLA_GATE_TAU = 16.0
N_MOD = 6
EPS = 1e-6
LOG2E = 1.4426950408889634
NEG = -1e30
CHUNK = 64
LANES = 128
MOD_ROWS = 16
VMEM_LIMIT = 56 * 1024 * 1024

ATTN_HEAD_ORDER = (0, 2, 1, 3)
W_Q = 2 * GROUP_WIDTH
W_KV = 2 * KV_HEADS * HEAD_DIM
W_C = 896
W_D = 1152
W_ALL = W_Q + 2 * W_KV + W_C + W_D


NB = 2
ROW_TILE = 256
KEY_CHUNK = 256


def _per_batch_row(body, layout):
    def kernel(*refs):
        assert len(refs) == len(layout)
        for i in range(NB):
            body(*[r.at[i:i + 1] if how == "b" else r for r, how in zip(refs, layout)])
    return kernel


def _cparams(sem):
    return pltpu.CompilerParams(dimension_semantics=sem, vmem_limit_bytes=VMEM_LIMIT)


def _dot(a, b):
    return jnp.dot(a, b, preferred_element_type=F32)


def _dot_nt(a, b):
    return lax.dot_general(a, b, (((1,), (1,)), ((), ())), preferred_element_type=F32)


def _dot_tn(a, b):
    return lax.dot_general(a, b, (((0,), (0,)), ((), ())), preferred_element_type=F32)


def _split3(x):
    hi = x.astype(BF16)
    r1 = x - hi.astype(F32)
    mid = r1.astype(BF16)
    lo = (r1 - mid.astype(F32)).astype(BF16)
    return hi, mid, lo


def _dot01(a01, x):
    hi, mid, lo = _split3(x)
    return _dot(a01, hi) + _dot(a01, mid) + _dot(a01, lo)


def _dot01_l2(a01, x):
    hi = x.astype(BF16)
    lo = (x - hi.astype(F32)).astype(BF16)
    return _dot(a01, hi) + _dot(a01, lo)


def _dot01_r2(x, b01):
    hi = x.astype(BF16)
    lo = (x - hi.astype(F32)).astype(BF16)
    return _dot(hi, b01) + _dot(lo, b01)


def _log_sigmoid(x):
    return jnp.minimum(x, 0.0) - jnp.log(1.0 + jnp.exp(-jnp.abs(x)))


def _sigmoid(x):
    return 1.0 / (1.0 + jnp.exp(-x))


def _iota(shape, dim):
    return lax.broadcasted_iota(jnp.int32, shape, dim)


def _group_ones(n, group):
    return (_iota((n, n), 0) // group == _iota((n, n), 1) // group).astype(BF16)


def _mod_kernel(c_ref, w_ref, b_ref, o_ref):
    c = c_ref[...]
    s = (c * _sigmoid(c)).astype(BF16)
    o_ref[0] = _dot(s, w_ref[0].astype(BF16)) + b_ref[0]


def _modulation(cc, w_mod, b_mod):
    depth, d, n = w_mod.shape
    tn = 1536
    return pl.pallas_call(
        _mod_kernel,
        out_shape=jax.ShapeDtypeStruct((depth, MOD_ROWS, n), F32),
        grid=(depth, n // tn),
        in_specs=[pl.BlockSpec((MOD_ROWS, d), lambda l, j: (0, 0)),
                  pl.BlockSpec((1, d, tn), lambda l, j: (l, 0, j)),
                  pl.BlockSpec((1, 1, tn), lambda l, j: (l, 0, j))],
        out_specs=pl.BlockSpec((1, MOD_ROWS, tn), lambda l, j: (l, 0, j)),
        compiler_params=_cparams(("arbitrary", "arbitrary")),
        name="modulation",
    )(cc, w_mod, b_mod.reshape(depth, 1, n))


def _qk_norm_rope(z, gain, cos, sin, ones_bd, scale):
    ss = _dot01_r2(z * z, ones_bd)
    y = z * lax.rsqrt(ss * (1.0 / HEAD_DIM) + EPS) * gain
    first_half = (_iota((z.shape[0], LANES), 1) % HEAD_DIM) < (HEAD_DIM // 2)
    outs = []
    for cb in range(z.shape[1] // LANES):
        yc = y[:, cb * LANES:(cb + 1) * LANES]
        partner = jnp.where(first_half, pltpu.roll(yc, LANES - HEAD_DIM // 2, 1),
                            pltpu.roll(yc, HEAD_DIM // 2, 1))
        outs.append((yc * cos + partner * sin) * scale)
    return jnp.concatenate(outs, axis=1)


def _residual_specs(xs, tm, ncb, skip_blocks=0):
    d = xs[0].shape[2]
    if len(xs) == 1:
        return [pl.BlockSpec((NB, tm, d), lambda bi, j: (bi, j + skip_blocks, 0))]
    return [pl.BlockSpec((NB, tm, d), lambda bi, j: (bi, jnp.minimum(j + skip_blocks, ncb - 1), 0)),
            pl.BlockSpec((NB, tm, d), lambda bi, j: (bi, jnp.maximum(j + skip_blocks - ncb, 0), 0))]


def _residual_rows(x_refs, ncb, skip_blocks=0):
    if len(x_refs) == 1:
        return x_refs[0][0]
    return jnp.where(pl.program_id(1) + skip_blocks < ncb, x_refs[0][0], x_refs[1][0])


def _proj_kernel(*refs, n_x, ncb):
    x_refs = refs[:n_x]
    (mod_ref, g1_ref, w_ref, gains_ref, cos_ref, sin_ref, wg_ref, bg_ref, bm_ref,
     q_ref, k_ref, vt_ref, zc_ref, zd_ref) = refs[n_x:]
    d = x_refs[0].shape[2]
    tm = x_refs[0].shape[1]
    x = _residual_rows(x_refs, ncb)
    mod = mod_ref[0, 0]
    sh1 = mod[:, 0:d]
    sc1 = mod[:, d:2 * d]
    ms = jnp.mean(x * x, axis=-1, keepdims=True)
    h = (x * lax.rsqrt(ms + EPS)) * g1_ref[...]
    hb = (h * (1.0 + sc1) + sh1).astype(BF16)

    ones_bd = _group_ones(GROUP_WIDTH, HEAD_DIM)
    cos = cos_ref[...]
    sin = sin_ref[...]
    gains = gains_ref[...]
    gw = GROUP_WIDTH

    z = _dot(hb, w_ref[:, 0:W_Q + 2 * W_KV])
    q_scale = HEAD_DIM ** -0.5 * LOG2E
    for n in range(2):
        q = _qk_norm_rope(z[:, n * gw:(n + 1) * gw], gains[n:n + 1], cos, sin, ones_bd, q_scale)
        q_ref[0, :, n * gw:(n + 1) * gw] = q.astype(BF16)
    k_ref[0] = _qk_norm_rope(z[:, W_Q:W_Q + W_KV], gains[2:3], cos, sin, ones_bd, 1.0).astype(BF16)
    vt_ref[0] = z[:, W_Q + W_KV:W_Q + 2 * W_KV].T.astype(BF16)

    base = W_Q + 2 * W_KV
    z = _dot(hb, w_ref[:, base:base + W_C])
    zc_ref[0, :, 0:GLA_KW] = z[:, 0:GLA_KW] * (GLA_DK ** -0.5)
    zc_ref[0, :, GLA_KW:768] = z[:, GLA_KW:768]
    pre = _dot(z[:, 768:896].astype(BF16), wg_ref[...]) + bg_ref[...]
    zc_ref[0, :, 768:1024] = _log_sigmoid(pre) * (1.0 / GLA_GATE_TAU)

    z = _dot(hb, w_ref[:, base + W_C:W_ALL])
    zd_ref[0, :, 0:gw] = z[:, 0:gw]
    zd_ref[0, :, gw:2 * gw] = z[:, gw:2 * gw] * (HEAD_DIM ** -0.5)
    zd_ref[0, :, 2 * gw:4 * gw] = z[:, 2 * gw:4 * gw]
    gates = z[:, 4 * gw:4 * gw + LANES] + bm_ref[...]
    is_input_gate = _iota((tm, LANES), 1) < 2 * GROUP_HEADS
    zd_ref[0, :, 4 * gw:4 * gw + LANES] = jnp.where(is_input_gate, gates, _log_sigmoid(gates))


def _projection(xs, modsel, g1, w_big, gains, cos_t, sin_t, wg, bg, bm, tm, ncb):
    b, _, d = xs[0].shape
    t = sum(a.shape[1] for a in xs)
    nblk = t // tm
    const = lambda bi, j: (0, 0)
    return pl.pallas_call(
        _per_batch_row(functools.partial(_proj_kernel, n_x=len(xs), ncb=ncb), "b" * (len(xs) + 1) + "-" * 8 + "b" * 5),
        out_shape=(jax.ShapeDtypeStruct((b, t, W_Q), BF16),
                   jax.ShapeDtypeStruct((b, t, W_KV), BF16),
                   jax.ShapeDtypeStruct((b, W_KV, t), BF16),
                   jax.ShapeDtypeStruct((b, t, 1024), F32),
                   jax.ShapeDtypeStruct((b, t, W_D), F32)),
        grid=(b // NB, nblk),
        in_specs=_residual_specs(xs, tm, ncb) + [
                  pl.BlockSpec((NB, 1, 1, N_MOD * d), lambda bi, j: (bi, jnp.where(j >= ncb, 1, 0), 0, 0)),
                  pl.BlockSpec((1, d), const),
                  pl.BlockSpec((d, W_ALL), const),
                  pl.BlockSpec((3, GROUP_WIDTH), const),
                  pl.BlockSpec((tm, LANES), lambda bi, j: (j, 0)),
                  pl.BlockSpec((tm, LANES), lambda bi, j: (j, 0)),
                  pl.BlockSpec((LANES, 2 * GLA_KW), const),
                  pl.BlockSpec((1, 2 * GLA_KW), const),
                  pl.BlockSpec((1, LANES), const)],
        out_specs=(pl.BlockSpec((NB, tm, W_Q), lambda bi, j: (bi, j, 0)),
                   pl.BlockSpec((NB, tm, W_KV), lambda bi, j: (bi, j, 0)),
                   pl.BlockSpec((NB, W_KV, tm), lambda bi, j: (bi, 0, j)),
                   pl.BlockSpec((NB, tm, 1024), lambda bi, j: (bi, j, 0)),
                   pl.BlockSpec((NB, tm, W_D), lambda bi, j: (bi, j, 0))),
        compiler_params=_cparams(("arbitrary", "arbitrary")),
        name="projection",
    )(*xs, modsel, g1, w_big, gains, cos_t, sin_t, wg, bg, bm)


ONES_ROWS = 16


def _masked_query_tiles(q_ref):
    low = _iota((q_ref.shape[1], LANES), 1) < HEAD_DIM
    tiles = []
    for n in range(GROUP_WIDTH // LANES):
        q128 = q_ref[0, :, n * LANES:(n + 1) * LANES]
        zero = jnp.zeros_like(q128)
        tiles += [jnp.where(low, q128, zero), jnp.where(low, zero, q128)]
    return tiles


def _heads_to_rows(o_t):
    halves = [jnp.concatenate(o_t[2 * n:2 * n + 2], axis=0).T for n in range(len(o_t) // 2)]
    return jnp.concatenate(halves, axis=1)


def _window_stages(sink_ref, q_ref, k_refs, v_refs, o_ref, j, nctx, nblk):
    tq = q_ref.shape[1]
    halo = Q_BLOCK
    n_win = tq + 2 * halo
    n_keys = n_win + k_refs[3].shape[1]
    is_lat = j >= nctx
    off = 4 * tq
    lo_prev = jnp.where(j > nctx, -halo, off)
    lo_cur = jnp.where(is_lat, -halo, off)
    lo_next = jnp.where(jnp.logical_and(is_lat, j + 1 <= nblk - 1), -halo, off)
    row = _iota((n_keys, tq), 0)
    diff = row - halo - _iota((n_keys, tq), 1)
    lo = jnp.where(row < halo, lo_prev, jnp.where(row < halo + tq, lo_cur, lo_next))
    ok = jnp.logical_or(row >= n_win, jnp.logical_and(diff >= lo, diff <= halo))
    keys = jnp.concatenate([r[0] for r in k_refs], axis=0)
    v_t = jnp.concatenate([r[0] for r in v_refs], axis=1)
    ones = jnp.ones((ONES_ROWS, n_keys), BF16)
    tiles = _masked_query_tiles(q_ref)
    yield
    scores = [jnp.where(ok, _dot_nt(keys, qp), NEG) for qp in tiles]
    yield
    o_t = []
    for c, s_t in enumerate(scores):
        sink = sink_ref[ATTN_HEAD_ORDER[c]] * LOG2E
        m = jnp.maximum(jnp.max(s_t, axis=0, keepdims=True), sink)
        kv = c % KV_HEADS
        vx = jnp.concatenate([v_t[kv * HEAD_DIM:(kv + 1) * HEAD_DIM], ones], axis=0)
        acc = _dot(vx, jnp.exp2(s_t - m).astype(BF16))
        o_t.append(acc[0:HEAD_DIM] / (acc[HEAD_DIM:HEAD_DIM + 1] + jnp.exp2(sink - m)))
        yield
    o_ref[0] = _heads_to_rows(o_t).astype(BF16)


def _attention_kernel(sink_ref, qa_ref, qb_ref, ka_ref, vta_ref, kp_ref, kc_ref, kn_ref, kx_ref,
                      vp_ref, vc_ref, vn_ref, vx_ref, oa_ref, ob_ref, *, lc, tk, nk, nctx, nblk, first):
    n_tiles = GROUP_HEADS
    j = pl.program_id(1) + first
    tiles = _masked_query_tiles(qa_ref)
    ones = jnp.ones((ONES_ROWS, tk), BF16)

    def window():
        return _window_stages(sink_ref, qb_ref, (kp_ref, kc_ref, kn_ref, kx_ref),
                              (vp_ref, vc_ref, vn_ref, vx_ref), ob_ref, j, nctx, nblk)

    def scores(rows):
        k = ka_ref[0, rows, :]
        return [_dot_nt(k, qp) for qp in tiles]

    def chunk(s_t, rows, m, accs):
        m_out, acc_out = [], []
        for c in range(n_tiles):
            col_max = jnp.max(s_t[c], axis=0, keepdims=True)
            m_new = col_max if m is None else jnp.maximum(m[c], col_max)
            p_t = jnp.exp2(s_t[c] - m_new).astype(BF16)
            kv = c % KV_HEADS
            vx = jnp.concatenate([vta_ref[0, kv * HEAD_DIM:(kv + 1) * HEAD_DIM, rows], ones], axis=0)
            upd = _dot(vx, p_t)
            m_out.append(m_new)
            acc_out.append(upd if m is None else jnp.exp2(m[c] - m_new) * accs[c] + upd)
        return m_out, acc_out

    def key_chunks(first_chunk, count, m, accs, side=None):
        rows = [slice((first_chunk + c) * tk, (first_chunk + c + 1) * tk) for c in range(count)]
        s_next = scores(rows[0])
        for c in range(count):
            s_t = s_next
            if c + 1 < count:
                s_next = scores(rows[c + 1])
            if side is not None:
                next(side, None)
            m, accs = chunk(s_t, rows[c], m, accs)
        return m, accs

    n_ctx = lc // tk
    m, accs = key_chunks(0, n_ctx, None, None)

    def finish(accs):
        return _heads_to_rows([a[0:HEAD_DIM] / a[HEAD_DIM:HEAD_DIM + 1] for a in accs])

    def latent_queries():
        side = window()
        out = finish(key_chunks(n_ctx, nk, m, accs, side)[1])
        for _ in side:
            pass
        return out

    def context_queries():
        for _ in window():
            pass
        return finish(accs)

    o = lax.cond(j >= nctx, latent_queries, context_queries) if first < nctx else latent_queries()
    oa_ref[0] = o.astype(BF16)


def _attention(q, k, vt, sink, lc, tq, tk, context_queries):
    b, t, _ = q.shape
    nblk = t // tq
    nctx = lc // tq
    nk = (t - lc) // tk
    assert lc % tk == 0 and (t - lc) % tk == 0
    first = 0 if context_queries else nctx
    per = tq // Q_BLOCK
    prev_blk = lambda j: jnp.maximum(j * per - 1, 0)
    next_blk = lambda j: jnp.minimum((j + 1) * per, nblk * per - 1)
    spec = lambda shape, index: pl.BlockSpec(shape, lambda bi, s: index(bi, s + first))
    k_specs = [spec((NB, Q_BLOCK, LANES), lambda bi, j: (bi, prev_blk(j), 1)),
               spec((NB, tq, LANES), lambda bi, j: (bi, j, 1)),
               spec((NB, Q_BLOCK, LANES), lambda bi, j: (bi, next_blk(j), 1)),
               spec((NB, lc, LANES), lambda bi, j: (bi, 0, 1))]
    v_specs = [spec((NB, LANES, Q_BLOCK), lambda bi, j: (bi, 1, prev_blk(j))),
               spec((NB, LANES, tq), lambda bi, j: (bi, 1, j)),
               spec((NB, LANES, Q_BLOCK), lambda bi, j: (bi, 1, next_blk(j))),
               spec((NB, LANES, lc), lambda bi, j: (bi, 1, 0))]
    out = jax.ShapeDtypeStruct((b, t - first * tq, GROUP_WIDTH), BF16)
    out_spec = pl.BlockSpec((NB, tq, GROUP_WIDTH), lambda bi, s: (bi, s, 0))
    q_spec = lambda col: spec((NB, tq, GROUP_WIDTH), lambda bi, j: (bi, j, col))
    return pl.pallas_call(
        _per_batch_row(functools.partial(_attention_kernel, lc=lc, tk=tk, nk=nk, nctx=nctx, nblk=nblk, first=first),
                       "-" + "b" * 14),
        out_shape=(out, out),
        grid=(b // NB, nblk - first),
        in_specs=[pl.BlockSpec(memory_space=pltpu.SMEM), q_spec(0), q_spec(1),
                  spec((NB, t, LANES), lambda bi, j: (bi, 0, 0)),
                  spec((NB, LANES, t), lambda bi, j: (bi, 0, 0))] + k_specs + v_specs,
        out_specs=(out_spec, out_spec),
        compiler_params=_cparams(("arbitrary", "arbitrary")),
        name="attention",
    )(sink, q, q, k, vt, k, k, k, k, vt, vt, vt, vt)


def _chunk_tri(rows, reverse):
    t = _iota((rows, rows), 0)
    u = _iota((rows, rows), 1)
    same = t // CHUNK == u // CHUNK
    return jnp.logical_and(same, (u >= t) if reverse else (u <= t)).astype(BF16)


def _tile4(x):
    return jnp.concatenate([x, x, x, x], axis=0)


def _rev_block(i, ncb, nblk):
    return jnp.where(i < ncb, ncb - 1 - i, nblk - 1 - (i - ncb))


class _Item:
    pass


def _scan_items(directions, nchunk):
    items = []
    for pos in range(nchunk):
        for d in directions:
            it = _Item()
            it.d = d
            it.c = nchunk - 1 - pos if d.reverse else pos
            it.sl = slice(it.c * CHUNK, (it.c + 1) * CHUNK)
            items.append(it)
    return items


def _gla_stages(qkv_f_ref, la_f_ref, qkv_b_ref, la_b_ref, of_ref, ob_ref, sf_ref, sb_ref):
    @pl.when(pl.program_id(1) == 0)
    def _():
        sf_ref[...] = jnp.zeros_like(sf_ref)
        sb_ref[...] = jnp.zeros_like(sb_ref)

    rows = qkv_f_ref.shape[1]
    same_kd = _iota((GROUP_WIDTH, GLA_KW), 0) // CHUNK == _iota((GROUP_WIDTH, GLA_KW), 1) // GLA_DK
    same_kv = _iota((GROUP_WIDTH, GROUP_WIDTH), 0) // CHUNK == _iota((GROUP_WIDTH, GROUP_WIDTH), 1) // HEAD_DIM
    t_pos = _iota((CHUNK, GROUP_WIDTH), 0)
    s_pos = _iota((CHUNK, GROUP_WIDTH), 1) % CHUNK

    directions = []
    for qkv_ref, la_ref, o_ref, s_ref, reverse in ((qkv_f_ref, la_f_ref, of_ref, sf_ref, False),
                                                   (qkv_b_ref, la_b_ref, ob_ref, sb_ref, True)):
        d = _Item()
        d.reverse, d.o_ref, d.s_ref = reverse, o_ref, s_ref
        d.b = _dot01_l2(_chunk_tri(rows, reverse), la_ref[0])
        d.q = qkv_ref[0, :, 0:GLA_KW]
        d.k = qkv_ref[0, :, GLA_KW:2 * GLA_KW]
        d.v = qkv_ref[0, :, 2 * GLA_KW:2 * GLA_KW + GROUP_WIDTH].astype(BF16)
        d.causal = (s_pos >= t_pos) if reverse else (s_pos <= t_pos)
        d.end = 0 if reverse else CHUNK - 1
        d.state = s_ref[...]
        directions.append(d)
    items = _scan_items(directions, rows // CHUNK)

    yield
    for it in items:
        d = it.d
        bc = d.b[it.sl]
        b_end = bc[d.end:d.end + 1]
        b_mid = bc[CHUNK // 2:CHUNK // 2 + 1]
        qc, kc = d.q[it.sl], d.k[it.sl]
        it.vc = d.v[it.sl]
        it.q_inter = (qc * jnp.exp(bc)).astype(BF16)
        it.q_intra = (qc * jnp.exp(bc - b_mid)).astype(BF16)
        k_intra = (kc * jnp.exp(b_mid - bc)).astype(BF16)
        it.k_state = (kc * jnp.exp(b_end - bc)).astype(BF16)
        it.decay = jnp.exp(b_end)
        it.k_bd = jnp.where(same_kd, _tile4(k_intra), jnp.zeros((), BF16))
        it.v_bd = jnp.where(same_kv, _tile4(it.vc), jnp.zeros((), BF16))
    yield
    for it in items:
        it.scores = jnp.where(it.d.causal, _dot_nt(it.q_intra, it.k_bd), 0.0).astype(BF16)
    yield
    for it in items:
        it.o_intra = _dot(it.scores, it.v_bd)
        it.update = jnp.where(same_kd, _dot_tn(it.vc, it.k_state), 0.0)
    yield
    for it in items:
        it.state_in = it.d.state
        it.d.state = it.d.state * it.decay + it.update
    yield
    for it in items:
        it.d.o_ref[0, it.sl, :] = it.o_intra + _dot_nt(it.q_inter, it.state_in.astype(BF16))
    yield
    for d in directions:
        d.s_ref[...] = d.state


def _expand_heads(g, base, rows):
    hid = _iota((rows, GROUP_WIDTH), 1) // HEAD_DIM
    cols = [jnp.broadcast_to(g[:, base + h:base + h + 1], (rows, GROUP_WIDTH)) for h in range(GROUP_HEADS)]
    return jnp.where(hid == 0, cols[0], jnp.where(hid == 1, cols[1], jnp.where(hid == 2, cols[2], cols[3])))


def _mlstm_stages(qkv_f_ref, gate_f_ref, qkv_b_ref, gate_b_ref, of_ref, ob_ref,
                  cf_ref, nf_ref, mf_ref, cb_ref, nb_ref, mb_ref):
    @pl.when(pl.program_id(1) == 0)
    def _():
        for r in (cf_ref, nf_ref, mf_ref, cb_ref, nb_ref, mb_ref):
            r[...] = jnp.zeros_like(r)

    rows = qkv_f_ref.shape[1]
    gw = GROUP_WIDTH
    same_head = _iota((gw, gw), 0) // HEAD_DIM == _iota((gw, gw), 1) // HEAD_DIM
    ones_bd = same_head.astype(BF16)
    hid = _iota((CHUNK, gw), 1) // HEAD_DIM
    t_pos = _iota((CHUNK, gw), 0)
    s_pos = _iota((CHUNK, gw), 1) % CHUNK
    diag = (s_pos == t_pos).astype(F32)
    ones_cc = jnp.ones((CHUNK, CHUNK), BF16)

    directions = []
    for index, (qkv_ref, gate_ref, o_ref, c_ref, n_ref, m_ref) in enumerate(
            ((qkv_f_ref, gate_f_ref, of_ref, cf_ref, nf_ref, mf_ref),
             (qkv_b_ref, gate_b_ref, ob_ref, cb_ref, nb_ref, mb_ref))):
        d = _Item()
        d.reverse = index == 1
        d.o_ref, d.c_ref, d.n_ref, d.m_ref = o_ref, c_ref, n_ref, m_ref
        g = gate_ref[0]
        i_col = GROUP_HEADS * index
        f_col = 2 * GROUP_HEADS + GROUP_HEADS * index
        f_all = _dot01_l2(_chunk_tri(rows, d.reverse), g)
        d.ig = _expand_heads(g, i_col, rows)
        d.f_cum = _expand_heads(f_all, f_col, rows)
        d.q = qkv_ref[0, :, 0:gw]
        d.k = qkv_ref[0, :, gw:2 * gw]
        d.v = qkv_ref[0, :, 2 * gw:3 * gw].astype(BF16)
        d.causal = (s_pos >= t_pos) if d.reverse else (s_pos <= t_pos)
        d.end = 0 if d.reverse else CHUNK - 1
        d.c_state = c_ref[...]
        d.n_state = n_ref[...]
        d.m_state = m_ref[...]
        directions.append(d)
    items = _scan_items(directions, rows // CHUNK)

    yield
    for it in items:
        d = it.d
        it.fc, it.ic = d.f_cum[it.sl], d.ig[it.sl]
        it.qc, it.kc, it.vc = d.q[it.sl], d.k[it.sl], d.v[it.sl]
        it.qb = it.qc.astype(BF16)
        it.key_term = _dot01(ones_cc, (it.fc - it.ic) * diag)
    yield
    for it in items:
        it.logw = jnp.where(it.d.causal, it.fc - it.key_term, NEG)
        head_max = [jnp.max(it.logw[:, h * HEAD_DIM:(h + 1) * HEAD_DIM], axis=-1, keepdims=True)
                    for h in range(GROUP_HEADS)]
        it.row_max = jnp.where(hid < 2, jnp.where(hid == 0, head_max[0], head_max[1]),
                               jnp.where(hid == 2, head_max[2], head_max[3]))
        it.k_bd = jnp.where(same_head, _tile4(it.kc.astype(BF16)), jnp.zeros((), BF16))
        it.v_bd = jnp.where(same_head, _tile4(it.vc), jnp.zeros((), BF16))
    yield
    for it in items:
        d = it.d
        it.log_inter = it.fc + d.m_state
        it.m_t = jnp.maximum(it.log_inter, it.row_max)
        d.m_state = it.m_new = it.m_t[d.end:d.end + 1]
    yield
    for it in items:
        end = it.d.end
        it.w = jnp.exp(it.logw - it.m_t)
        it.w_inter = jnp.exp(it.log_inter - it.m_t)
        it.decay = it.w_inter[end:end + 1]
        it.k_end = it.kc * jnp.exp(it.fc[end:end + 1] - it.fc + it.ic - it.m_new)
    yield
    for it in items:
        it.qk = _dot_nt(it.qb, it.k_bd) * it.w
    yield
    for it in items:
        it.num = _dot(it.qk.astype(BF16), it.v_bd)
        it.den = _dot01_r2(it.qk, ones_bd)
        it.update = jnp.where(same_head, _dot_tn(it.k_end.astype(BF16), it.vc), 0.0)
    yield
    for it in items:
        d = it.d
        it.c_in, it.n_in = d.c_state, d.n_state
        d.c_state = d.c_state * it.decay + it.update
        d.n_state = d.n_state * it.decay + jnp.sum(it.k_end, axis=0, keepdims=True)
    yield
    for it in items:
        num = it.num + it.w_inter * _dot(it.qb, it.c_in.astype(BF16))
        den = it.den + it.w_inter * _dot((it.qc * it.n_in).astype(BF16), ones_bd)
        it.d.o_ref[0, it.sl, :] = num / jnp.maximum(jnp.abs(den), jnp.exp(-it.m_t))
    yield
    for d in directions:
        d.c_ref[...] = d.c_state
        d.n_ref[...] = d.n_state
        d.m_ref[...] = d.m_state


def _scans_kernel(*refs):
    running = []
    for i in range(NB):
        blocks = [r.at[i:i + 1] for r in refs[0:12]]
        state = [r.at[i] for r in refs[12:20]]
        running.append(_gla_stages(*blocks[0:4], *blocks[8:10], *state[0:2]))
        running.append(_mlstm_stages(*blocks[4:8], *blocks[10:12], *state[2:8]))
    while running:
        running = [g for g in running if next(g, StopIteration) is not StopIteration]


def _scans(zc, zd, lc, rows):
    b, t, _ = zc.shape
    nblk = t // rows
    ncb = lc // rows
    gw = GROUP_WIDTH
    fwd = lambda col: (lambda bi, i: (bi, i, col))
    rev = lambda col: (lambda bi, i: (bi, _rev_block(i, ncb, nblk), col))
    gla_qkv_w = 2 * GLA_KW + gw
    gate_col = 4 * gw // LANES
    mlstm_state = [pltpu.VMEM((NB, gw, gw), F32), pltpu.VMEM((NB, 1, gw), F32), pltpu.VMEM((NB, 1, gw), F32)]
    gla_state = [pltpu.VMEM((NB, gw, GLA_KW), F32)]
    out = jax.ShapeDtypeStruct((b, t, gw), F32)
    return pl.pallas_call(
        _scans_kernel,
        out_shape=(out, out, out, out),
        grid=(b // NB, nblk),
        in_specs=[pl.BlockSpec((NB, rows, gla_qkv_w), fwd(0)),
                  pl.BlockSpec((NB, rows, GLA_KW), fwd(6)),
                  pl.BlockSpec((NB, rows, gla_qkv_w), rev(0)),
                  pl.BlockSpec((NB, rows, GLA_KW), rev(7)),
                  pl.BlockSpec((NB, rows, 3 * gw), fwd(0)),
                  pl.BlockSpec((NB, rows, LANES), fwd(gate_col)),
                  pl.BlockSpec((NB, rows, 3 * gw), rev(0)),
                  pl.BlockSpec((NB, rows, LANES), rev(gate_col))],
        out_specs=(pl.BlockSpec((NB, rows, gw), fwd(0)), pl.BlockSpec((NB, rows, gw), rev(0)),
                   pl.BlockSpec((NB, rows, gw), fwd(0)), pl.BlockSpec((NB, rows, gw), rev(0))),
        scratch_shapes=gla_state + gla_state + mlstm_state + mlstm_state,
        compiler_params=_cparams(("arbitrary", "arbitrary")),
        name="scans",
    )(zc, zc, zc, zc, zd, zd, zd, zd)


def _head_rms(o, gain, ones_bd):
    ss = _dot01_r2(o * o, ones_bd)
    return o * lax.rsqrt(ss * (1.0 / HEAD_DIM) + EPS) * gain


def _out_kernel(*refs, n_x, ncb, skip_blocks):
    x_refs = refs[:n_x]
    (mod_ref, oa_ref, ob_ref, gf_ref, gb_ref, r_ref, mf_ref, mb_ref, og_ref,
     gains_ref, g2_ref, wo_ref, w1_ref, w2_ref, y_ref) = refs[n_x:]
    d = x_refs[0].shape[2]
    x = _residual_rows(x_refs, ncb, skip_blocks)
    mod = mod_ref[0, 0]
    ga1 = mod[:, 2 * d:3 * d]
    sh2 = mod[:, 3 * d:4 * d]
    sc2 = mod[:, 4 * d:5 * d]
    ga2 = mod[:, 5 * d:6 * d]
    ones_bd = _group_ones(GROUP_WIDTH, HEAD_DIM)
    gains = gains_ref[...]

    r = r_ref[0]
    o_c = _head_rms(gf_ref[0] + gb_ref[0], gains[0:1], ones_bd) * (r * _sigmoid(r))
    o_d = _sigmoid(og_ref[0]) * _head_rms(mf_ref[0] + mb_ref[0], gains[1:2], ones_bd)
    o_cat = jnp.concatenate([oa_ref[0], ob_ref[0], o_c.astype(BF16), o_d.astype(BF16)], axis=1)
    x = x + ga1 * _dot(o_cat, wo_ref[...])

    ms = jnp.mean(x * x, axis=-1, keepdims=True)
    h = (x * lax.rsqrt(ms + EPS)) * g2_ref[...]
    hb = (h * (1.0 + sc2) + sh2).astype(BF16)
    hid = jnp.maximum(_dot(hb, w1_ref[...]), 0.0)
    hid = (hid * hid).astype(BF16)
    y_ref[0] = x + ga2 * _dot(hid, w2_ref[...])


def _out_mlp(xs, modsel, oa, ob, gf, gb, zc, mf, mb, zd, gains, g2, wo, w1, w2, tm, ncb, skip_blocks):
    b, _, d = xs[0].shape
    t = sum(a.shape[1] for a in xs)
    nblk = t // tm - skip_blocks
    gw = GROUP_WIDTH
    row = lambda col: (lambda bi, j: (bi, j + skip_blocks, col))
    attn_skip = (t - oa.shape[1]) // tm
    attn_row = lambda bi, j: (bi, j + skip_blocks - attn_skip, 0)
    const = lambda bi, j: (0, 0)
    resident = lambda shape: pl.BlockSpec(shape, const, pipeline_mode=pl.Buffered(1))
    return pl.pallas_call(
        _per_batch_row(functools.partial(_out_kernel, n_x=len(xs), ncb=ncb, skip_blocks=skip_blocks),
                       "b" * (len(xs) + 9) + "-" * 5 + "b"),
        out_shape=jax.ShapeDtypeStruct((b, nblk * tm, d), F32),
        grid=(b // NB, nblk),
        in_specs=_residual_specs(xs, tm, ncb, skip_blocks) + [
                  pl.BlockSpec((NB, 1, 1, N_MOD * d),
                               lambda bi, j: (bi, jnp.where(j + skip_blocks >= ncb, 1, 0), 0, 0)),
                  pl.BlockSpec((NB, tm, gw), attn_row),
                  pl.BlockSpec((NB, tm, gw), attn_row),
                  pl.BlockSpec((NB, tm, gw), row(0)),
                  pl.BlockSpec((NB, tm, gw), row(0)),
                  pl.BlockSpec((NB, tm, gw), row(2)),
                  pl.BlockSpec((NB, tm, gw), row(0)),
                  pl.BlockSpec((NB, tm, gw), row(0)),
                  pl.BlockSpec((NB, tm, gw), row(3)),
                  pl.BlockSpec((2, gw), const),
                  pl.BlockSpec((1, d), const),
                  resident((d, d)),
                  resident((d, 4 * d)),
                  resident((4 * d, d))],
        out_specs=pl.BlockSpec((NB, tm, d), lambda bi, j: (bi, j, 0)),
        compiler_params=_cparams(("arbitrary", "arbitrary")),
        name="out_mlp",
    )(*xs, modsel, oa, ob, gf, gb, zc, mf, mb, zd, gains, g2, wo, w1, w2)


def _projection_columns():
    gw, kw = GROUP_WIDTH, KV_HEADS * HEAD_DIM
    splits = (gw, kw, kw, gw, kw, kw, GLA_KW, GLA_KW, gw, gw, 2 * GLA_GATE_RANK,
              gw, gw, gw, gw, 2 * GROUP_HEADS, 2 * GROUP_HEADS)
    offs = np.concatenate([[0], np.cumsum(splits)])
    in_width = int(offs[-1])
    half_split = np.concatenate([np.arange(0, HEAD_DIM, 2), np.arange(1, HEAD_DIM, 2)])
    plain = np.arange(HEAD_DIM)
    cols = []
    mixers = (0, 3)
    for base in mixers:
        cols += [offs[base] + h * HEAD_DIM + half_split for h in ATTN_HEAD_ORDER]
    for base in mixers:
        cols += [offs[base + 1] + g * HEAD_DIM + half_split for g in range(KV_HEADS)]
    for base in mixers:
        cols += [offs[base + 2] + g * HEAD_DIM + plain for g in range(KV_HEADS)]
    cols.append(np.arange(offs[6], offs[11]))
    cols.append(np.full(W_C - (offs[11] - offs[6]), in_width))
    cols.append(np.arange(offs[11], offs[17]))
    cols.append(np.full(W_D - (offs[17] - offs[11]), in_width))
    cols = np.concatenate(cols).astype(np.int32)
    assert cols.shape[0] == W_ALL
    return cols, in_width, half_split


def _rope_tables(seq, lc):
    rows = seq // GRID_W
    row = jnp.repeat(jnp.arange(rows, dtype=F32), GRID_W)
    col = jnp.tile(jnp.arange(GRID_W, dtype=F32), rows)
    n_freq = HEAD_DIM // 4
    inv = ROPE_THETA ** (-jnp.arange(n_freq, dtype=F32) / n_freq)
    ang = jnp.concatenate([row[:, None] * inv, col[:, None] * inv], axis=-1)
    cos, sin = jnp.cos(ang), jnp.sin(ang)
    cos_t = jnp.tile(cos, (1, 4))
    sin_t = jnp.tile(jnp.concatenate([-sin, sin], axis=-1), (1, 2))
    cos_t = jnp.concatenate([jnp.ones((lc, LANES), F32), cos_t], axis=0)
    sin_t = jnp.concatenate([jnp.zeros((lc, LANES), F32), sin_t], axis=0)
    return cos_t, sin_t


def kernel(x, c, ctx, c_ctx, w_mod, b_mod, g_norm1, g_norm2, w_in, g_q_a, g_k_a, g_q_b, g_k_b, sink_b,
           w_gla_gate, b_gla_gate, g_gla_out, b_mlstm_i, b_mlstm_f, g_mlstm_out, w_out, w_mlp1, w_mlp2):
    b, seq, d = x.shape
    lc = ctx.shape[1]
    depth = w_mod.shape[0]
    tm = ROW_TILE
    assert b % NB == 0 and d == 4 * GROUP_WIDTH and lc % tm == 0 and seq % tm == 0 and seq % GRID_W == 0 and b + 1 <= MOD_ROWS

    cols, in_width, half_split = _projection_columns()
    cos_t, sin_t = _rope_tables(seq, lc)
    attn_rows = np.concatenate([h * HEAD_DIM + np.arange(HEAD_DIM) for h in ATTN_HEAD_ORDER])
    out_rows = np.concatenate([attn_rows, GROUP_WIDTH + attn_rows, np.arange(2 * GROUP_WIDTH, d)])

    cc = jnp.zeros((MOD_ROWS, d), F32).at[0:b].set(c).at[b].set(c_ctx)
    mod_all = _modulation(cc, w_mod, b_mod)

    xs = (ctx, x)
    for l in range(depth):
        modsel = jnp.stack([jnp.broadcast_to(mod_all[l, b], (b, N_MOD * d)), mod_all[l, 0:b]], axis=1)
        modsel = modsel.reshape(b, 2, 1, N_MOD * d)

        w_ext = jnp.concatenate([w_in[l], jnp.zeros((d, 1), F32)], axis=1)
        w_big = jnp.take(w_ext, cols, axis=1).astype(BF16)
        tiled = lambda g, n: jnp.tile(g[half_split], n)
        gains_qk = jnp.stack([tiled(g_q_a[l], GROUP_HEADS), tiled(g_q_b[l], GROUP_HEADS),
                              jnp.concatenate([tiled(g_k_a[l], KV_HEADS), tiled(g_k_b[l], KV_HEADS)])])
        wg = jnp.zeros((LANES, 2 * GLA_KW), F32)
        wg = wg.at[0:GLA_GATE_RANK, 0:GLA_KW].set(w_gla_gate[l, 0])
        wg = wg.at[GLA_GATE_RANK:2 * GLA_GATE_RANK, GLA_KW:].set(w_gla_gate[l, 1]).astype(BF16)
        bg = b_gla_gate[l].reshape(1, 2 * GLA_KW)
        bm = jnp.zeros((1, LANES), F32)
        bm = bm.at[0, 0:2 * GROUP_HEADS].set(b_mlstm_i[l].reshape(-1))
        bm = bm.at[0, 2 * GROUP_HEADS:4 * GROUP_HEADS].set(b_mlstm_f[l].reshape(-1))

        q, k, vt, zc, zd = _projection(xs, modsel, g_norm1[l].reshape(1, d), w_big, gains_qk,
                                      cos_t, sin_t, wg, bg, bm, tm, lc // tm)
        last = l == depth - 1
        oa, ob = _attention(q, k, vt, sink_b[l], lc, tq=ROW_TILE, tk=KEY_CHUNK, context_queries=not last)
        gf, gb, mf, mb = _scans(zc, zd, lc, tm)

        gains_out = jnp.stack([jnp.tile(g_gla_out[l], 4), jnp.tile(g_mlstm_out[l], 4)])
        xs = (_out_mlp(xs, modsel, oa, ob, gf, gb, zc, mf, mb, zd, gains_out, g_norm2[l].reshape(1, d),
                       w_out[l][out_rows].astype(BF16), w_mlp1[l].astype(BF16), w_mlp2[l].astype(BF16), tm,
                       ncb=lc // tm, skip_blocks=lc // tm if last else 0),)
    return xs[0]
```

```python
import functools

import numpy as np
import jax
import jax.numpy as jnp
from jax import lax
from jax.experimental import pallas as pl
from jax.experimental.pallas import tpu as pltpu

F32 = jnp.float32
BF16 = jnp.bfloat16

HEAD_DIM = 64
GROUP_HEADS = 4
GROUP_WIDTH = GROUP_HEADS * HEAD_DIM
KV_HEADS = 2
GRID_W = 64
Q_BLOCK = 128
ROPE_THETA = 10000.0
GLA_DK = 32
GLA_KW = GROUP_HEADS * GLA_DK
GLA_GATE_RANK = 16
GLA_GATE_TAU = 16.0
N_MOD = 6
EPS = 1e-6
LOG2E = 1.4426950408889634
NEG = -1e30
CHUNK = 64
LANES = 128
MOD_ROWS = 16
VMEM_LIMIT = 56 * 1024 * 1024

ATTN_HEAD_ORDER = (0, 2, 1, 3)
W_Q = 2 * GROUP_WIDTH
W_KV = 2 * KV_HEADS * HEAD_DIM
W_C = 896
W_D = 1152
W_ALL = W_Q + 2 * W_KV + W_C + W_D


NB = 2
ROW_TILE = 256
KEY_CHUNK = 256
FF_CHUNK = 1024


def _per_batch_row(body, layout):
    def kernel(*refs):
        assert len(refs) == len(layout)
        for i in range(NB):
            body(*[r.at[i:i + 1] if how == "b" else r for r, how in zip(refs, layout)])
    return kernel


def _cparams(sem):
    return pltpu.CompilerParams(dimension_semantics=sem, vmem_limit_bytes=VMEM_LIMIT)


def _dot(a, b):
    return jnp.dot(a, b, preferred_element_type=F32)


def _dot_nt(a, b):
    return lax.dot_general(a, b, (((1,), (1,)), ((), ())), preferred_element_type=F32)


def _dot_tn(a, b):
    return lax.dot_general(a, b, (((0,), (0,)), ((), ())), preferred_element_type=F32)


def _split3(x):
    hi = x.astype(BF16)
    r1 = x - hi.astype(F32)
    mid = r1.astype(BF16)
    lo = (r1 - mid.astype(F32)).astype(BF16)
    return hi, mid, lo


def _dot01(a01, x):
    hi, mid, lo = _split3(x)
    return _dot(a01, hi) + _dot(a01, mid) + _dot(a01, lo)


def _dot01_l2(a01, x):
    hi = x.astype(BF16)
    lo = (x - hi.astype(F32)).astype(BF16)
    return _dot(a01, hi) + _dot(a01, lo)


def _dot01_r2(x, b01):
    hi = x.astype(BF16)
    lo = (x - hi.astype(F32)).astype(BF16)
    return _dot(hi, b01) + _dot(lo, b01)


def _log_sigmoid(x):
    return jnp.minimum(x, 0.0) - jnp.log(1.0 + jnp.exp(-jnp.abs(x)))


def _sigmoid(x):
    return 1.0 / (1.0 + jnp.exp(-x))


def _iota(shape, dim):
    return lax.broadcasted_iota(jnp.int32, shape, dim)


def _group_ones(n, group):
    return (_iota((n, n), 0) // group == _iota((n, n), 1) // group).astype(BF16)


def _mod_kernel(c_ref, w_ref, b_ref, o_ref):
    c = c_ref[...]
    s = (c * _sigmoid(c)).astype(BF16)
    o_ref[0] = _dot(s, w_ref[0].astype(BF16)) + b_ref[0]


def _modulation(cc, w_mod, b_mod):
    depth, d, n = w_mod.shape
    tn = 1536
    return pl.pallas_call(
        _mod_kernel,
        out_shape=jax.ShapeDtypeStruct((depth, MOD_ROWS, n), F32),
        grid=(depth, n // tn),
        in_specs=[pl.BlockSpec((MOD_ROWS, d), lambda l, j: (0, 0)),
                  pl.BlockSpec((1, d, tn), lambda l, j: (l, 0, j)),
                  pl.BlockSpec((1, 1, tn), lambda l, j: (l, 0, j))],
        out_specs=pl.BlockSpec((1, MOD_ROWS, tn), lambda l, j: (l, 0, j)),
        compiler_params=_cparams(("arbitrary", "arbitrary")),
        name="modulation",
    )(cc, w_mod, b_mod.reshape(depth, 1, n))


def _qk_norm_rope(z, gain, cos, sin, ones_bd, scale):
    ss = _dot01_r2(z * z, ones_bd)
    y = z * lax.rsqrt(ss * (1.0 / HEAD_DIM) + EPS) * gain
    first_half = (_iota((z.shape[0], LANES), 1) % HEAD_DIM) < (HEAD_DIM // 2)
    outs = []
    for cb in range(z.shape[1] // LANES):
        yc = y[:, cb * LANES:(cb + 1) * LANES]
        partner = jnp.where(first_half, pltpu.roll(yc, LANES - HEAD_DIM // 2, 1),
                            pltpu.roll(yc, HEAD_DIM // 2, 1))
        outs.append((yc * cos + partner * sin) * scale)
    return jnp.concatenate(outs, axis=1)


def _residual_specs(xs, tm, ncb, skip_blocks=0):
    d = xs[0].shape[2]
    if len(xs) == 1:
        return [pl.BlockSpec((NB, tm, d), lambda bi, j: (bi, j + skip_blocks, 0))]
    return [pl.BlockSpec((NB, tm, d), lambda bi, j: (bi, jnp.minimum(j + skip_blocks, ncb - 1), 0)),
            pl.BlockSpec((NB, tm, d), lambda bi, j: (bi, jnp.maximum(j + skip_blocks - ncb, 0), 0))]


def _residual_rows(x_refs, ncb, skip_blocks=0):
    if len(x_refs) == 1:
        return x_refs[0][0]
    return jnp.where(pl.program_id(1) + skip_blocks < ncb, x_refs[0][0], x_refs[1][0])


def _proj_kernel(*refs, n_x, ncb):
    x_refs = refs[:n_x]
    (mod_ref, g1_ref, w_ref, gains_ref, cos_ref, sin_ref, wg_ref, bg_ref, bm_ref,
     q_ref, k_ref, vt_ref, zc_ref, zd_ref) = refs[n_x:]
    d = x_refs[0].shape[2]
    tm = x_refs[0].shape[1]
    x = _residual_rows(x_refs, ncb)
    mod = mod_ref[0, 0]
    sh1 = mod[:, 0:d]
    sc1 = mod[:, d:2 * d]
    ms = jnp.mean(x * x, axis=-1, keepdims=True)
    h = (x * lax.rsqrt(ms + EPS)) * g1_ref[...]
    hb = (h * (1.0 + sc1) + sh1).astype(BF16)

    ones_bd = _group_ones(GROUP_WIDTH, HEAD_DIM)
    cos = cos_ref[...]
    sin = sin_ref[...]
    gains = gains_ref[...]
    gw = GROUP_WIDTH

    z = _dot(hb, w_ref[:, 0:W_Q + 2 * W_KV])
    q_scale = HEAD_DIM ** -0.5 * LOG2E
    for n in range(2):
        q = _qk_norm_rope(z[:, n * gw:(n + 1) * gw], gains[n:n + 1], cos, sin, ones_bd, q_scale)
        q_ref[0, :, n * gw:(n + 1) * gw] = q.astype(BF16)
    k_ref[0] = _qk_norm_rope(z[:, W_Q:W_Q + W_KV], gains[2:3], cos, sin, ones_bd, 1.0).astype(BF16)
    vt_ref[0] = z[:, W_Q + W_KV:W_Q + 2 * W_KV].T.astype(BF16)

    base = W_Q + 2 * W_KV
    z = _dot(hb, w_ref[:, base:base + W_C])
    zc_ref[0, :, 0:GLA_KW] = z[:, 0:GLA_KW] * (GLA_DK ** -0.5)
    zc_ref[0, :, GLA_KW:768] = z[:, GLA_KW:768]
    pre = _dot(z[:, 768:896].astype(BF16), wg_ref[...]) + bg_ref[...]
    zc_ref[0, :, 768:1024] = _log_sigmoid(pre) * (1.0 / GLA_GATE_TAU)

    z = _dot(hb, w_ref[:, base + W_C:W_ALL])
    zd_ref[0, :, 0:gw] = z[:, 0:gw]
    zd_ref[0, :, gw:2 * gw] = z[:, gw:2 * gw] * (HEAD_DIM ** -0.5)
    zd_ref[0, :, 2 * gw:4 * gw] = z[:, 2 * gw:4 * gw]
    gates = z[:, 4 * gw:4 * gw + LANES] + bm_ref[...]
    is_input_gate = _iota((tm, LANES), 1) < 2 * GROUP_HEADS
    zd_ref[0, :, 4 * gw:4 * gw + LANES] = jnp.where(is_input_gate, gates, _log_sigmoid(gates))


def _projection(xs, modsel, g1, w_big, gains, cos_t, sin_t, wg, bg, bm, tm, ncb):
    b, _, d = xs[0].shape
    t = sum(a.shape[1] for a in xs)
    nblk = t // tm
    const = lambda bi, j: (0, 0)
    return pl.pallas_call(
        _per_batch_row(functools.partial(_proj_kernel, n_x=len(xs), ncb=ncb), "b" * (len(xs) + 1) + "-" * 8 + "b" * 5),
        out_shape=(jax.ShapeDtypeStruct((b, t, W_Q), BF16),
                   jax.ShapeDtypeStruct((b, t, W_KV), BF16),
                   jax.ShapeDtypeStruct((b, W_KV, t), BF16),
                   jax.ShapeDtypeStruct((b, t, 1024), F32),
                   jax.ShapeDtypeStruct((b, t, W_D), F32)),
        grid=(b // NB, nblk),
        in_specs=_residual_specs(xs, tm, ncb) + [
                  pl.BlockSpec((NB, 1, 1, N_MOD * d), lambda bi, j: (bi, jnp.where(j >= ncb, 1, 0), 0, 0)),
                  pl.BlockSpec((1, d), const),
                  pl.BlockSpec((d, W_ALL), const),
                  pl.BlockSpec((3, GROUP_WIDTH), const),
                  pl.BlockSpec((tm, LANES), lambda bi, j: (j, 0)),
                  pl.BlockSpec((tm, LANES), lambda bi, j: (j, 0)),
                  pl.BlockSpec((LANES, 2 * GLA_KW), const),
                  pl.BlockSpec((1, 2 * GLA_KW), const),
                  pl.BlockSpec((1, LANES), const)],
        out_specs=(pl.BlockSpec((NB, tm, W_Q), lambda bi, j: (bi, j, 0)),
                   pl.BlockSpec((NB, tm, W_KV), lambda bi, j: (bi, j, 0)),
                   pl.BlockSpec((NB, W_KV, tm), lambda bi, j: (bi, 0, j)),
                   pl.BlockSpec((NB, tm, 1024), lambda bi, j: (bi, j, 0)),
                   pl.BlockSpec((NB, tm, W_D), lambda bi, j: (bi, j, 0))),
        compiler_params=_cparams(("arbitrary", "arbitrary")),
        name="projection",
    )(*xs, modsel, g1, w_big, gains, cos_t, sin_t, wg, bg, bm)


ONES_ROWS = 16


def _masked_query_tiles(q_ref):
    low = _iota((q_ref.shape[1], LANES), 1) < HEAD_DIM
    tiles = []
    for n in range(GROUP_WIDTH // LANES):
        q128 = q_ref[0, :, n * LANES:(n + 1) * LANES]
        zero = jnp.zeros_like(q128)
        tiles += [jnp.where(low, q128, zero), jnp.where(low, zero, q128)]
    return tiles


def _heads_to_rows(o_t):
    halves = [jnp.concatenate(o_t[2 * n:2 * n + 2], axis=0).T for n in range(len(o_t) // 2)]
    return jnp.concatenate(halves, axis=1)


def _window_stages(sink_ref, q_ref, k_refs, v_refs, o_ref, j, nctx, nblk):
    tq = q_ref.shape[1]
    halo = Q_BLOCK
    n_win = tq + 2 * halo
    n_keys = n_win + k_refs[3].shape[1]
    is_lat = j >= nctx
    off = 4 * tq
    lo_prev = jnp.where(j > nctx, -halo, off)
    lo_cur = jnp.where(is_lat, -halo, off)
    lo_next = jnp.where(jnp.logical_and(is_lat, j + 1 <= nblk - 1), -halo, off)
    row = _iota((n_keys, tq), 0)
    diff = row - halo - _iota((n_keys, tq), 1)
    lo = jnp.where(row < halo, lo_prev, jnp.where(row < halo + tq, lo_cur, lo_next))
    ok = jnp.logical_or(row >= n_win, jnp.logical_and(diff >= lo, diff <= halo))
    keys = jnp.concatenate([r[0] for r in k_refs], axis=0)
    v_t = jnp.concatenate([r[0] for r in v_refs], axis=1)
    ones = jnp.ones((ONES_ROWS, n_keys), BF16)
    tiles = _masked_query_tiles(q_ref)
    yield
    scores = [jnp.where(ok, _dot_nt(keys, qp), NEG) for qp in tiles]
    yield
    o_t = []
    for c, s_t in enumerate(scores):
        sink = sink_ref[ATTN_HEAD_ORDER[c]] * LOG2E
        m = jnp.maximum(jnp.max(s_t, axis=0, keepdims=True), sink)
        kv = c % KV_HEADS
        vx = jnp.concatenate([v_t[kv * HEAD_DIM:(kv + 1) * HEAD_DIM], ones], axis=0)
        acc = _dot(vx, jnp.exp2(s_t - m).astype(BF16))
        o_t.append(acc[0:HEAD_DIM] / (acc[HEAD_DIM:HEAD_DIM + 1] + jnp.exp2(sink - m)))
        yield
    o_ref[0] = _heads_to_rows(o_t).astype(BF16)


def _attention_kernel(sink_ref, qa_ref, qb_ref, ka_ref, vta_ref, kp_ref, kc_ref, kn_ref, kx_ref,
                      vp_ref, vc_ref, vn_ref, vx_ref, oa_ref, ob_ref, *, lc, tk, nk, nctx, nblk, first):
    n_tiles = GROUP_HEADS
    j = pl.program_id(1) + first
    tiles = _masked_query_tiles(qa_ref)
    ones = jnp.ones((ONES_ROWS, tk), BF16)

    def window():
        return _window_stages(sink_ref, qb_ref, (kp_ref, kc_ref, kn_ref, kx_ref),
                              (vp_ref, vc_ref, vn_ref, vx_ref), ob_ref, j, nctx, nblk)

    def scores(rows):
        k = ka_ref[0, rows, :]
        return [_dot_nt(k, qp) for qp in tiles]

    def chunk(s_t, rows, m, accs):
        m_out, acc_out = [], []
        for c in range(n_tiles):
            col_max = jnp.max(s_t[c], axis=0, keepdims=True)
            m_new = col_max if m is None else jnp.maximum(m[c], col_max)
            p_t = jnp.exp2(s_t[c] - m_new).astype(BF16)
            kv = c % KV_HEADS
            vx = jnp.concatenate([vta_ref[0, kv * HEAD_DIM:(kv + 1) * HEAD_DIM, rows], ones], axis=0)
            upd = _dot(vx, p_t)
            m_out.append(m_new)
            acc_out.append(upd if m is None else jnp.exp2(m[c] - m_new) * accs[c] + upd)
        return m_out, acc_out

    def key_chunks(first_chunk, count, m, accs, side=None):
        rows = [slice((first_chunk + c) * tk, (first_chunk + c + 1) * tk) for c in range(count)]
        s_next = scores(rows[0])
        for c in range(count):
            s_t = s_next
            if c + 1 < count:
                s_next = scores(rows[c + 1])
            if side is not None:
                next(side, None)
            m, accs = chunk(s_t, rows[c], m, accs)
        return m, accs

    n_ctx = lc // tk
    m, accs = key_chunks(0, n_ctx, None, None)

    def finish(accs):
        return _heads_to_rows([a[0:HEAD_DIM] / a[HEAD_DIM:HEAD_DIM + 1] for a in accs])

    def latent_queries():
        side = window()
        out = finish(key_chunks(n_ctx, nk, m, accs, side)[1])
        for _ in side:
            pass
        return out

    def context_queries():
        for _ in window():
            pass
        return finish(accs)

    o = lax.cond(j >= nctx, latent_queries, context_queries) if first < nctx else latent_queries()
    oa_ref[0] = o.astype(BF16)


def _attention(q, k, vt, sink, lc, tq, tk, context_queries):
    b, t, _ = q.shape
    nblk = t // tq
    nctx = lc // tq
    nk = (t - lc) // tk
    assert lc % tk == 0 and (t - lc) % tk == 0
    first = 0 if context_queries else nctx
    per = tq // Q_BLOCK
    prev_blk = lambda j: jnp.maximum(j * per - 1, 0)
    next_blk = lambda j: jnp.minimum((j + 1) * per, nblk * per - 1)
    spec = lambda shape, index: pl.BlockSpec(shape, lambda bi, s: index(bi, s + first))
    k_specs = [spec((NB, Q_BLOCK, LANES), lambda bi, j: (bi, prev_blk(j), 1)),
               spec((NB, tq, LANES), lambda bi, j: (bi, j, 1)),
               spec((NB, Q_BLOCK, LANES), lambda bi, j: (bi, next_blk(j), 1)),
               spec((NB, lc, LANES), lambda bi, j: (bi, 0, 1))]
    v_specs = [spec((NB, LANES, Q_BLOCK), lambda bi, j: (bi, 1, prev_blk(j))),
               spec((NB, LANES, tq), lambda bi, j: (bi, 1, j)),
               spec((NB, LANES, Q_BLOCK), lambda bi, j: (bi, 1, next_blk(j))),
               spec((NB, LANES, lc), lambda bi, j: (bi, 1, 0))]
    out = jax.ShapeDtypeStruct((b, t - first * tq, GROUP_WIDTH), BF16)
    out_spec = pl.BlockSpec((NB, tq, GROUP_WIDTH), lambda bi, s: (bi, s, 0))
    q_spec = lambda col: spec((NB, tq, GROUP_WIDTH), lambda bi, j: (bi, j, col))
    return pl.pallas_call(
        _per_batch_row(functools.partial(_attention_kernel, lc=lc, tk=tk, nk=nk, nctx=nctx, nblk=nblk, first=first),
                       "-" + "b" * 14),
        out_shape=(out, out),
        grid=(b // NB, nblk - first),
        in_specs=[pl.BlockSpec(memory_space=pltpu.SMEM), q_spec(0), q_spec(1),
                  spec((NB, t, LANES), lambda bi, j: (bi, 0, 0)),
                  spec((NB, LANES, t), lambda bi, j: (bi, 0, 0))] + k_specs + v_specs,
        out_specs=(out_spec, out_spec),
        compiler_params=_cparams(("arbitrary", "arbitrary")),
        name="attention",
    )(sink, q, q, k, vt, k, k, k, k, vt, vt, vt, vt)


def _chunk_tri(rows, reverse):
    t = _iota((rows, rows), 0)
    u = _iota((rows, rows), 1)
    same = t // CHUNK == u // CHUNK
    return jnp.logical_and(same, (u >= t) if reverse else (u <= t)).astype(BF16)


def _tile4(x):
    return jnp.concatenate([x, x, x, x], axis=0)


def _rev_block(i, ncb, nblk):
    return jnp.where(i < ncb, ncb - 1 - i, nblk - 1 - (i - ncb))


class _Item:
    pass


def _scan_items(directions, nchunk):
    items = []
    for pos in range(nchunk):
        for d in directions:
            it = _Item()
            it.d = d
            it.c = nchunk - 1 - pos if d.reverse else pos
            it.sl = slice(it.c * CHUNK, (it.c + 1) * CHUNK)
            items.append(it)
    return items


def _gla_stages(qkv_f_ref, la_f_ref, qkv_b_ref, la_b_ref, of_ref, ob_ref, sf_ref, sb_ref):
    @pl.when(pl.program_id(1) == 0)
    def _():
        sf_ref[...] = jnp.zeros_like(sf_ref)
        sb_ref[...] = jnp.zeros_like(sb_ref)

    rows = qkv_f_ref.shape[1]
    same_kd = _iota((GROUP_WIDTH, GLA_KW), 0) // CHUNK == _iota((GROUP_WIDTH, GLA_KW), 1) // GLA_DK
    same_kv = _iota((GROUP_WIDTH, GROUP_WIDTH), 0) // CHUNK == _iota((GROUP_WIDTH, GROUP_WIDTH), 1) // HEAD_DIM
    t_pos = _iota((CHUNK, GROUP_WIDTH), 0)
    s_pos = _iota((CHUNK, GROUP_WIDTH), 1) % CHUNK

    directions = []
    for qkv_ref, la_ref, o_ref, s_ref, reverse in ((qkv_f_ref, la_f_ref, of_ref, sf_ref, False),
                                                   (qkv_b_ref, la_b_ref, ob_ref, sb_ref, True)):
        d = _Item()
        d.reverse, d.o_ref, d.s_ref = reverse, o_ref, s_ref
        d.b = _dot01_l2(_chunk_tri(rows, reverse), la_ref[0])
        d.q = qkv_ref[0, :, 0:GLA_KW]
        d.k = qkv_ref[0, :, GLA_KW:2 * GLA_KW]
        d.v = qkv_ref[0, :, 2 * GLA_KW:2 * GLA_KW + GROUP_WIDTH].astype(BF16)
        d.causal = (s_pos >= t_pos) if reverse else (s_pos <= t_pos)
        d.end = 0 if reverse else CHUNK - 1
        d.state = s_ref[...]
        directions.append(d)
    items = _scan_items(directions, rows // CHUNK)

    yield
    for it in items:
        d = it.d
        bc = d.b[it.sl]
        b_end = bc[d.end:d.end + 1]
        b_mid = bc[CHUNK // 2:CHUNK // 2 + 1]
        qc, kc = d.q[it.sl], d.k[it.sl]
        it.vc = d.v[it.sl]
        it.q_inter = (qc * jnp.exp(bc)).astype(BF16)
        it.q_intra = (qc * jnp.exp(bc - b_mid)).astype(BF16)
        k_intra = (kc * jnp.exp(b_mid - bc)).astype(BF16)
        it.k_state = (kc * jnp.exp(b_end - bc)).astype(BF16)
        it.decay = jnp.exp(b_end)
        it.k_bd = jnp.where(same_kd, _tile4(k_intra), jnp.zeros((), BF16))
        it.v_bd = jnp.where(same_kv, _tile4(it.vc), jnp.zeros((), BF16))
    yield
    for it in items:
        it.scores = jnp.where(it.d.causal, _dot_nt(it.q_intra, it.k_bd), 0.0).astype(BF16)
    yield
    for it in items:
        it.o_intra = _dot(it.scores, it.v_bd)
        it.update = jnp.where(same_kd, _dot_tn(it.vc, it.k_state), 0.0)
    yield
    for it in items:
        it.state_in = it.d.state
        it.d.state = it.d.state * it.decay + it.update
    yield
    for it in items:
        it.d.o_ref[0, it.sl, :] = it.o_intra + _dot_nt(it.q_inter, it.state_in.astype(BF16))
    yield
    for d in directions:
        d.s_ref[...] = d.state


def _expand_heads(g, base, rows):
    hid = _iota((rows, GROUP_WIDTH), 1) // HEAD_DIM
    cols = [jnp.broadcast_to(g[:, base + h:base + h + 1], (rows, GROUP_WIDTH)) for h in range(GROUP_HEADS)]
    return jnp.where(hid == 0, cols[0], jnp.where(hid == 1, cols[1], jnp.where(hid == 2, cols[2], cols[3])))


def _mlstm_stages(qkv_f_ref, gate_f_ref, qkv_b_ref, gate_b_ref, of_ref, ob_ref,
                  cf_ref, nf_ref, mf_ref, cb_ref, nb_ref, mb_ref):
    @pl.when(pl.program_id(1) == 0)
    def _():
        for r in (cf_ref, nf_ref, mf_ref, cb_ref, nb_ref, mb_ref):
            r[...] = jnp.zeros_like(r)

    rows = qkv_f_ref.shape[1]
    gw = GROUP_WIDTH
    same_head = _iota((gw, gw), 0) // HEAD_DIM == _iota((gw, gw), 1) // HEAD_DIM
    ones_bd = same_head.astype(BF16)
    hid = _iota((CHUNK, gw), 1) // HEAD_DIM
    t_pos = _iota((CHUNK, gw), 0)
    s_pos = _iota((CHUNK, gw), 1) % CHUNK
    diag = (s_pos == t_pos).astype(F32)
    ones_cc = jnp.ones((CHUNK, CHUNK), BF16)

    directions = []
    for index, (qkv_ref, gate_ref, o_ref, c_ref, n_ref, m_ref) in enumerate(
            ((qkv_f_ref, gate_f_ref, of_ref, cf_ref, nf_ref, mf_ref),
             (qkv_b_ref, gate_b_ref, ob_ref, cb_ref, nb_ref, mb_ref))):
        d = _Item()
        d.reverse = index == 1
        d.o_ref, d.c_ref, d.n_ref, d.m_ref = o_ref, c_ref, n_ref, m_ref
        g = gate_ref[0]
        i_col = GROUP_HEADS * index
        f_col = 2 * GROUP_HEADS + GROUP_HEADS * index
        f_all = _dot01_l2(_chunk_tri(rows, d.reverse), g)
        d.ig = _expand_heads(g, i_col, rows)
        d.f_cum = _expand_heads(f_all, f_col, rows)
        d.q = qkv_ref[0, :, 0:gw]
        d.k = qkv_ref[0, :, gw:2 * gw]
        d.v = qkv_ref[0, :, 2 * gw:3 * gw].astype(BF16)
        d.causal = (s_pos >= t_pos) if d.reverse else (s_pos <= t_pos)
        d.end = 0 if d.reverse else CHUNK - 1
        d.c_state = c_ref[...]
        d.n_state = n_ref[...]
        d.m_state = m_ref[...]
        directions.append(d)
    items = _scan_items(directions, rows // CHUNK)

    yield
    for it in items:
        d = it.d
        it.fc, it.ic = d.f_cum[it.sl], d.ig[it.sl]
        it.qc, it.kc, it.vc = d.q[it.sl], d.k[it.sl], d.v[it.sl]
        it.qb = it.qc.astype(BF16)
        it.key_term = _dot01(ones_cc, (it.fc - it.ic) * diag)
    yield
    for it in items:
        it.logw = jnp.where(it.d.causal, it.fc - it.key_term, NEG)
        head_max = [jnp.max(it.logw[:, h * HEAD_DIM:(h + 1) * HEAD_DIM], axis=-1, keepdims=True)
                    for h in range(GROUP_HEADS)]
        it.row_max = jnp.where(hid < 2, jnp.where(hid == 0, head_max[0], head_max[1]),
                               jnp.where(hid == 2, head_max[2], head_max[3]))
        it.k_bd = jnp.where(same_head, _tile4(it.kc.astype(BF16)), jnp.zeros((), BF16))
        it.v_bd = jnp.where(same_head, _tile4(it.vc), jnp.zeros((), BF16))
    yield
    for it in items:
        d = it.d
        it.log_inter = it.fc + d.m_state
        it.m_t = jnp.maximum(it.log_inter, it.row_max)
        d.m_state = it.m_new = it.m_t[d.end:d.end + 1]
    yield
    for it in items:
        end = it.d.end
        it.w = jnp.exp(it.logw - it.m_t)
        it.w_inter = jnp.exp(it.log_inter - it.m_t)
        it.decay = it.w_inter[end:end + 1]
        it.k_end = it.kc * jnp.exp(it.fc[end:end + 1] - it.fc + it.ic - it.m_new)
    yield
    for it in items:
        it.qk = _dot_nt(it.qb, it.k_bd) * it.w
    yield
    for it in items:
        it.num = _dot(it.qk.astype(BF16), it.v_bd)
        it.den = _dot01_r2(it.qk, ones_bd)
        it.update = jnp.where(same_head, _dot_tn(it.k_end.astype(BF16), it.vc), 0.0)
    yield
    for it in items:
        d = it.d
        it.c_in, it.n_in = d.c_state, d.n_state
        d.c_state = d.c_state * it.decay + it.update
        d.n_state = d.n_state * it.decay + jnp.sum(it.k_end, axis=0, keepdims=True)
    yield
    for it in items:
        num = it.num + it.w_inter * _dot(it.qb, it.c_in.astype(BF16))
        den = it.den + it.w_inter * _dot((it.qc * it.n_in).astype(BF16), ones_bd)
        it.d.o_ref[0, it.sl, :] = num / jnp.maximum(jnp.abs(den), jnp.exp(-it.m_t))
    yield
    for d in directions:
        d.c_ref[...] = d.c_state
        d.n_ref[...] = d.n_state
        d.m_ref[...] = d.m_state


def _scans_kernel(*refs):
    running = []
    for i in range(NB):
        blocks = [r.at[i:i + 1] for r in refs[0:12]]
        state = [r.at[i] for r in refs[12:20]]
        running.append(_gla_stages(*blocks[0:4], *blocks[8:10], *state[0:2]))
        running.append(_mlstm_stages(*blocks[4:8], *blocks[10:12], *state[2:8]))
    while running:
        running = [g for g in running if next(g, StopIteration) is not StopIteration]


def _scans(zc, zd, lc, rows):
    b, t, _ = zc.shape
    nblk = t // rows
    ncb = lc // rows
    gw = GROUP_WIDTH
    fwd = lambda col: (lambda bi, i: (bi, i, col))
    rev = lambda col: (lambda bi, i: (bi, _rev_block(i, ncb, nblk), col))
    gla_qkv_w = 2 * GLA_KW + gw
    gate_col = 4 * gw // LANES
    mlstm_state = [pltpu.VMEM((NB, gw, gw), F32), pltpu.VMEM((NB, 1, gw), F32), pltpu.VMEM((NB, 1, gw), F32)]
    gla_state = [pltpu.VMEM((NB, gw, GLA_KW), F32)]
    out = jax.ShapeDtypeStruct((b, t, gw), F32)
    return pl.pallas_call(
        _scans_kernel,
        out_shape=(out, out, out, out),
        grid=(b // NB, nblk),
        in_specs=[pl.BlockSpec((NB, rows, gla_qkv_w), fwd(0)),
                  pl.BlockSpec((NB, rows, GLA_KW), fwd(6)),
                  pl.BlockSpec((NB, rows, gla_qkv_w), rev(0)),
                  pl.BlockSpec((NB, rows, GLA_KW), rev(7)),
                  pl.BlockSpec((NB, rows, 3 * gw), fwd(0)),
                  pl.BlockSpec((NB, rows, LANES), fwd(gate_col)),
                  pl.BlockSpec((NB, rows, 3 * gw), rev(0)),
                  pl.BlockSpec((NB, rows, LANES), rev(gate_col))],
        out_specs=(pl.BlockSpec((NB, rows, gw), fwd(0)), pl.BlockSpec((NB, rows, gw), rev(0)),
                   pl.BlockSpec((NB, rows, gw), fwd(0)), pl.BlockSpec((NB, rows, gw), rev(0))),
        scratch_shapes=gla_state + gla_state + mlstm_state + mlstm_state,
        compiler_params=_cparams(("arbitrary", "arbitrary")),
        name="scans",
    )(zc, zc, zc, zc, zd, zd, zd, zd)


def _head_rms(o, gain, ones_bd):
    ss = _dot01_r2(o * o, ones_bd)
    return o * lax.rsqrt(ss * (1.0 / HEAD_DIM) + EPS) * gain


def _out_kernel(*refs, n_x, ncb, skip_blocks):
    x_refs = refs[:n_x]
    (mod_ref, oa_ref, ob_ref, gf_ref, gb_ref, r_ref, mf_ref, mb_ref, og_ref,
     gains_ref, g2_ref, wo_ref, w1_ref, w2_ref, y_ref) = refs[n_x:]
    d = x_refs[0].shape[2]
    x = _residual_rows(x_refs, ncb, skip_blocks)
    mod = mod_ref[0, 0]
    ga1 = mod[:, 2 * d:3 * d]
    sh2 = mod[:, 3 * d:4 * d]
    sc2 = mod[:, 4 * d:5 * d]
    ga2 = mod[:, 5 * d:6 * d]
    ones_bd = _group_ones(GROUP_WIDTH, HEAD_DIM)
    gains = gains_ref[...]

    r = r_ref[0]
    o_c = _head_rms(gf_ref[0] + gb_ref[0], gains[0:1], ones_bd) * (r * _sigmoid(r))
    o_d = _sigmoid(og_ref[0]) * _head_rms(mf_ref[0] + mb_ref[0], gains[1:2], ones_bd)
    o_cat = jnp.concatenate([oa_ref[0], ob_ref[0], o_c.astype(BF16), o_d.astype(BF16)], axis=1)
    x = x + ga1 * _dot(o_cat, wo_ref[...])

    ms = jnp.mean(x * x, axis=-1, keepdims=True)
    h = (x * lax.rsqrt(ms + EPS)) * g2_ref[...]
    hb = (h * (1.0 + sc2) + sh2).astype(BF16)
    d_ff = w1_ref.shape[1]
    mlp = None
    for c in range(d_ff // FF_CHUNK):
        cols = slice(c * FF_CHUNK, (c + 1) * FF_CHUNK)
        hid = jnp.maximum(_dot(hb, w1_ref[:, cols]), 0.0)
        part = _dot((hid * hid).astype(BF16), w2_ref[cols, :])
        mlp = part if mlp is None else mlp + part
    y_ref[0] = x + ga2 * mlp


def _out_mlp(xs, modsel, oa, ob, gf, gb, zc, mf, mb, zd, gains, g2, wo, w1, w2, tm, ncb, skip_blocks):
    b, _, d = xs[0].shape
    t = sum(a.shape[1] for a in xs)
    nblk = t // tm - skip_blocks
    gw = GROUP_WIDTH
    row = lambda col: (lambda bi, j: (bi, j + skip_blocks, col))
    attn_skip = (t - oa.shape[1]) // tm
    attn_row = lambda bi, j: (bi, j + skip_blocks - attn_skip, 0)
    const = lambda bi, j: (0, 0)
    resident = lambda shape: pl.BlockSpec(shape, const, pipeline_mode=pl.Buffered(1))
    return pl.pallas_call(
        _per_batch_row(functools.partial(_out_kernel, n_x=len(xs), ncb=ncb, skip_blocks=skip_blocks),
                       "b" * (len(xs) + 9) + "-" * 5 + "b"),
        out_shape=jax.ShapeDtypeStruct((b, nblk * tm, d), F32),
        grid=(b // NB, nblk),
        in_specs=_residual_specs(xs, tm, ncb, skip_blocks) + [
                  pl.BlockSpec((NB, 1, 1, N_MOD * d),
                               lambda bi, j: (bi, jnp.where(j + skip_blocks >= ncb, 1, 0), 0, 0)),
                  pl.BlockSpec((NB, tm, gw), attn_row),
                  pl.BlockSpec((NB, tm, gw), attn_row),
                  pl.BlockSpec((NB, tm, gw), row(0)),
                  pl.BlockSpec((NB, tm, gw), row(0)),
                  pl.BlockSpec((NB, tm, gw), row(2)),
                  pl.BlockSpec((NB, tm, gw), row(0)),
                  pl.BlockSpec((NB, tm, gw), row(0)),
                  pl.BlockSpec((NB, tm, gw), row(3)),
                  pl.BlockSpec((2, gw), const),
                  pl.BlockSpec((1, d), const),
                  resident((d, d)),
                  resident((d, 4 * d)),
                  resident((4 * d, d))],
        out_specs=pl.BlockSpec((NB, tm, d), lambda bi, j: (bi, j, 0)),
        compiler_params=_cparams(("arbitrary", "arbitrary")),
        name="out_mlp",
    )(*xs, modsel, oa, ob, gf, gb, zc, mf, mb, zd, gains, g2, wo, w1, w2)


def _projection_columns():
    gw, kw = GROUP_WIDTH, KV_HEADS * HEAD_DIM
    splits = (gw, kw, kw, gw, kw, kw, GLA_KW, GLA_KW, gw, gw, 2 * GLA_GATE_RANK,
              gw, gw, gw, gw, 2 * GROUP_HEADS, 2 * GROUP_HEADS)
    offs = np.concatenate([[0], np.cumsum(splits)])
    in_width = int(offs[-1])
    half_split = np.concatenate([np.arange(0, HEAD_DIM, 2), np.arange(1, HEAD_DIM, 2)])
    plain = np.arange(HEAD_DIM)
    cols = []
    mixers = (0, 3)
    for base in mixers:
        cols += [offs[base] + h * HEAD_DIM + half_split for h in ATTN_HEAD_ORDER]
    for base in mixers:
        cols += [offs[base + 1] + g * HEAD_DIM + half_split for g in range(KV_HEADS)]
    for base in mixers:
        cols += [offs[base + 2] + g * HEAD_DIM + plain for g in range(KV_HEADS)]
    cols.append(np.arange(offs[6], offs[11]))
    cols.append(np.full(W_C - (offs[11] - offs[6]), in_width))
    cols.append(np.arange(offs[11], offs[17]))
    cols.append(np.full(W_D - (offs[17] - offs[11]), in_width))
    cols = np.concatenate(cols).astype(np.int32)
    assert cols.shape[0] == W_ALL
    return cols, in_width, half_split


def _rope_tables(seq, lc):
    rows = seq // GRID_W
    row = jnp.repeat(jnp.arange(rows, dtype=F32), GRID_W)
    col = jnp.tile(jnp.arange(GRID_W, dtype=F32), rows)
    n_freq = HEAD_DIM // 4
    inv = ROPE_THETA ** (-jnp.arange(n_freq, dtype=F32) / n_freq)
    ang = jnp.concatenate([row[:, None] * inv, col[:, None] * inv], axis=-1)
    cos, sin = jnp.cos(ang), jnp.sin(ang)
    cos_t = jnp.tile(cos, (1, 4))
    sin_t = jnp.tile(jnp.concatenate([-sin, sin], axis=-1), (1, 2))
    cos_t = jnp.concatenate([jnp.ones((lc, LANES), F32), cos_t], axis=0)
    sin_t = jnp.concatenate([jnp.zeros((lc, LANES), F32), sin_t], axis=0)
    return cos_t, sin_t


def kernel(x, c, ctx, c_ctx, w_mod, b_mod, g_norm1, g_norm2, w_in, g_q_a, g_k_a, g_q_b, g_k_b, sink_b,
           w_gla_gate, b_gla_gate, g_gla_out, b_mlstm_i, b_mlstm_f, g_mlstm_out, w_out, w_mlp1, w_mlp2):
    b, seq, d = x.shape
    lc = ctx.shape[1]
    depth = w_mod.shape[0]
    tm = ROW_TILE
    assert b % NB == 0 and d == 4 * GROUP_WIDTH and lc % tm == 0 and seq % tm == 0 and seq % GRID_W == 0 and b + 1 <= MOD_ROWS

    cols, in_width, half_split = _projection_columns()
    cos_t, sin_t = _rope_tables(seq, lc)
    attn_rows = np.concatenate([h * HEAD_DIM + np.arange(HEAD_DIM) for h in ATTN_HEAD_ORDER])
    out_rows = np.concatenate([attn_rows, GROUP_WIDTH + attn_rows, np.arange(2 * GROUP_WIDTH, d)])

    cc = jnp.zeros((MOD_ROWS, d), F32).at[0:b].set(c).at[b].set(c_ctx)
    mod_all = _modulation(cc, w_mod, b_mod)

    xs = (ctx, x)
    for l in range(depth):
        modsel = jnp.stack([jnp.broadcast_to(mod_all[l, b], (b, N_MOD * d)), mod_all[l, 0:b]], axis=1)
        modsel = modsel.reshape(b, 2, 1, N_MOD * d)

        w_ext = jnp.concatenate([w_in[l], jnp.zeros((d, 1), F32)], axis=1)
        w_big = jnp.take(w_ext, cols, axis=1).astype(BF16)
        tiled = lambda g, n: jnp.tile(g[half_split], n)
        gains_qk = jnp.stack([tiled(g_q_a[l], GROUP_HEADS), tiled(g_q_b[l], GROUP_HEADS),
                              jnp.concatenate([tiled(g_k_a[l], KV_HEADS), tiled(g_k_b[l], KV_HEADS)])])
        wg = jnp.zeros((LANES, 2 * GLA_KW), F32)
        wg = wg.at[0:GLA_GATE_RANK, 0:GLA_KW].set(w_gla_gate[l, 0])
        wg = wg.at[GLA_GATE_RANK:2 * GLA_GATE_RANK, GLA_KW:].set(w_gla_gate[l, 1]).astype(BF16)
        bg = b_gla_gate[l].reshape(1, 2 * GLA_KW)
        bm = jnp.zeros((1, LANES), F32)
        bm = bm.at[0, 0:2 * GROUP_HEADS].set(b_mlstm_i[l].reshape(-1))
        bm = bm.at[0, 2 * GROUP_HEADS:4 * GROUP_HEADS].set(b_mlstm_f[l].reshape(-1))

        q, k, vt, zc, zd = _projection(xs, modsel, g_norm1[l].reshape(1, d), w_big, gains_qk,
                                      cos_t, sin_t, wg, bg, bm, tm, lc // tm)
        last = l == depth - 1
        oa, ob = _attention(q, k, vt, sink_b[l], lc, tq=ROW_TILE, tk=KEY_CHUNK, context_queries=not last)
        gf, gb, mf, mb = _scans(zc, zd, lc, tm)

        gains_out = jnp.stack([jnp.tile(g_gla_out[l], 4), jnp.tile(g_mlstm_out[l], 4)])
        xs = (_out_mlp(xs, modsel, oa, ob, gf, gb, zc, mf, mb, zd, gains_out, g_norm2[l].reshape(1, d),
                       w_out[l][out_rows].astype(BF16), w_mlp1[l].astype(BF16), w_mlp2[l].astype(BF16), tm,
                       ncb=lc // tm, skip_blocks=lc // tm if last else 0),)
    return xs[0]
```
